```python
import math
import jax, jax.numpy as jnp
from jax import lax
import numpy as np

D_MODEL = 1024
BATCH = 8
SEQ = 2048
DEPTH = 2
DEC_BATCH = 128
DEC_SEQ = 1
PAST_LEN = 2048
PAGE_SIZE = 128

N_EVEN = (DEPTH + 1) // 2
N_ODD = DEPTH // 2
D_POOL = D_MODEL // 2
POOL_WINDOWS = (2, 4, 8, 16)
POOL_GROUP = D_POOL // len(POOL_WINDOWS)
POOL_STATE = max(POOL_WINDOWS) - 1
D_CONV = D_MODEL // 2
CONV_WIDTH = 31
CONV_STATE = CONV_WIDTH - 1
DIL_CONFIGS = ((128, 1), (512, 4), (2048, 16))
N_DIL_GROUPS = len(DIL_CONFIGS)
HEADS_PER_GROUP = 8
HEAD_DIM = 64
D_ATTN_OUT = HEADS_PER_GROUP * HEAD_DIM
QUERY_BLOCK = 128
N_BUCKETS = 32
MAX_DISTANCE = 2048
N_EXPERTS = 256
TOP_K = 8
N_EXPERT_GROUPS = 8
TOPK_GROUPS = 4
D_EXPERT = 256
D_SHARED = 256
ROUTED_SCALE = 2.5
EXPERT_BLOCK = 128
SMALL_EXPERT_BLOCK = 16
DEEPNORM_ALPHA = (2.0 * DEPTH) ** 0.25
DEEPNORM_BETA = (8.0 * DEPTH) ** -0.25
LN_EPS = 1e-5
NEG_INF = -1e30

kernel_name = "hybrid_pool_conv_dilated_moe_step"


def layer_norm(x, g, b):
    xf = x.astype(jnp.float32)
    mu = jnp.mean(xf, -1, keepdims=True)
    var = jnp.mean(jnp.square(xf - mu), -1, keepdims=True)
    return ((xf - mu) * lax.rsqrt(var + LN_EPS) * g + b).astype(x.dtype)


def pool_branch(u, buf, pool_w, pool_scale):
    B, T, C = u.shape
    P = buf.shape[1]
    full = jnp.concatenate([buf, u], axis=1)
    cs = jnp.pad(jnp.cumsum(full.astype(jnp.float32), axis=1), ((0, 0), (1, 0), (0, 0)))
    win = jnp.repeat(jnp.array(POOL_WINDOWS, jnp.int32), POOL_GROUP)
    end = P + jnp.arange(T, dtype=jnp.int32) + 1
    start = jnp.maximum(end[:, None] - win[None, :], 0)
    lo = jnp.take_along_axis(cs, jnp.broadcast_to(start[None], (B, T, C)), axis=1)
    mean = (cs[:, P + 1:] - lo) / (end[:, None] - start).astype(jnp.float32)
    pooled = (mean - u.astype(jnp.float32)).astype(u.dtype)
    mixed = jnp.einsum('btgc,gcd->btgd', pooled.reshape(B, T, len(POOL_WINDOWS), POOL_GROUP), pool_w)
    return mixed.reshape(B, T, C) * pool_scale, full[:, -POOL_STATE:]


def conv_branch(a, gate, buf, conv_w, conv_b, ln_g, ln_b):
    glu = a * jax.nn.sigmoid(gate)
    full = jnp.concatenate([buf, glu], axis=1)
    y = lax.conv_general_dilated(full, conv_w[:, None, :], window_strides=(1,), padding='VALID',
                                 dimension_numbers=('NWC', 'WIO', 'NWC'),
                                 feature_group_count=full.shape[-1])
    y = jax.nn.silu(layer_norm(y + conv_b, ln_g, ln_b))
    return y, full[:, -CONV_STATE:]


def pool_conv_mixer(x, pool_buf, conv_buf, w_in, pool_w, pool_scale, conv_w, conv_b, ln_g, ln_b, w_out):
    proj = x @ w_in
    u = proj[..., :D_POOL]
    a = proj[..., D_POOL:D_POOL + D_CONV]
    gate = proj[..., D_POOL + D_CONV:]
    yp, pool_state = pool_branch(u, pool_buf, pool_w, pool_scale)
    yc, conv_state = conv_branch(a, gate, conv_buf, conv_w, conv_b, ln_g, ln_b)
    return jnp.concatenate([yp, yc], axis=-1) @ w_out, pool_state, conv_state


def rel_bucket(dist):
    exact = N_BUCKETS // 2
    df = jnp.maximum(dist, 1).astype(jnp.float32)
    large = exact + (jnp.log(df / exact) / math.log(MAX_DISTANCE / exact) * (N_BUCKETS - exact)).astype(jnp.int32)
    large = jnp.minimum(large, N_BUCKETS - 1)
    return jnp.where(dist < exact, dist, large)


def dilated_prompt(q, k, v, tab, dil, steps):
    B, S, H, Dh = q.shape
    L = S // dil
    nb = -(-L // QUERY_BLOCK)
    Lp = nb * QUERY_BLOCK

    def phases(a):
        a = a.reshape(B, L, dil, H, Dh).transpose(0, 2, 1, 3, 4).reshape(B * dil, L, H, Dh)
        a = jnp.pad(a, ((0, 0), (0, Lp - L), (0, 0), (0, 0)))
        return a.reshape(B * dil, nb, QUERY_BLOCK, H, Dh)

    def with_prev(a):
        prev = jnp.pad(a, ((0, 0), (1, 0), (0, 0), (0, 0), (0, 0)))[:, :-1]
        return jnp.concatenate([prev, a], axis=2)

    qb = phases(q)
    kk = with_prev(phases(k))
    vv = with_prev(phases(v))
    qi = jnp.arange(QUERY_BLOCK)[:, None]
    kj = jnp.arange(2 * QUERY_BLOCK)[None, :]
    dist = qi + QUERY_BLOCK - kj
    band = (dist >= 0) & (dist <= steps)
    not_before_start = (jnp.arange(nb)[:, None, None] > 0) | (kj >= QUERY_BLOCK)[None]
    valid = band[None] & not_before_start
    bias = tab[rel_bucket(jnp.maximum(dist, 0) * dil)].transpose(2, 0, 1)
    logits = jnp.einsum('bnqhd,bnkhd->bnhqk', qb, kk, preferred_element_type=jnp.float32) * (HEAD_DIM ** -0.5) + bias
    logits = jnp.where(valid[None, :, None], logits, NEG_INF)
    lse = jax.nn.logsumexp(logits, axis=-1)
    p = jnp.exp(logits - lse[..., None])
    o = jnp.einsum('bnhqk,bnkhd->bnqhd', p.astype(v.dtype), vv)
    o = o.reshape(B, dil, Lp, H, Dh)[:, :, :L].transpose(0, 2, 1, 3, 4).reshape(B, S, H, Dh)
    lse = lse.transpose(0, 1, 3, 2).reshape(B, dil, Lp, H)[:, :, :L].transpose(0, 2, 1, 3).reshape(B, S, H)
    return o, lse


def dilated_sample(q, k_new, v_new, k_buf, v_buf, tab, dil, steps):
    DB, T, H, Dh = q.shape
    Wb = k_buf.shape[1]
    kk = jnp.concatenate([k_buf, k_new], axis=1)
    vv = jnp.concatenate([v_buf, v_new], axis=1)
    stride = jnp.arange(steps + 1, dtype=jnp.int32)
    idx = Wb + jnp.arange(T, dtype=jnp.int32)[:, None] - stride[None, :] * dil
    valid = idx >= 0
    idx = jnp.maximum(idx, 0)
    kg = kk[:, idx]
    vg = vv[:, idx]
    bias = tab[rel_bucket(stride * dil)].T
    logits = jnp.einsum('bthd,btjhd->bhtj', q, kg, preferred_element_type=jnp.float32) * (HEAD_DIM ** -0.5) + bias[None, :, None, :]
    logits = jnp.where(valid[None, None], logits, NEG_INF)
    lse = jax.nn.logsumexp(logits, axis=-1)
    p = jnp.exp(logits - lse[..., None])
    o = jnp.einsum('bhtj,btjhd->bthd', p.astype(vg.dtype), vg)
    return o, lse.transpose(0, 2, 1)


def dilated_mixer(x, k_bufs, v_bufs, w_qkv, w_o, rel_bias):
    B, T, _ = x.shape
    H = HEADS_PER_GROUP
    qkv = (x @ w_qkv).reshape(B, T, 3, N_DIL_GROUPS, H, HEAD_DIM)
    outs, lses, new_k, new_v = [], [], [], []
    for g, (window, dil) in enumerate(DIL_CONFIGS):
        q, k, v = qkv[:, :, 0, g], qkv[:, :, 1, g], qkv[:, :, 2, g]
        tab = rel_bias[:, g * H:(g + 1) * H]
        steps = window // dil
        if k_bufs is None:
            o, lse = dilated_prompt(q, k, v, tab, dil, steps)
            keep = min(window, T)
            new_k.append(k[:, T - keep:])
            new_v.append(v[:, T - keep:])
        else:
            o, lse = dilated_sample(q, k, v, k_bufs[g], v_bufs[g], tab, dil, steps)
            new_k.append(k)
            new_v.append(v)
        outs.append(o)
        lses.append(lse)
    wts = jax.nn.softmax(jnp.stack(lses, axis=2), axis=2)
    o = jnp.einsum('btgh,btghd->bthd', wts, jnp.stack(outs, axis=2).astype(jnp.float32))
    y = o.reshape(B, T, D_ATTN_OUT).astype(x.dtype) @ w_o
    return y, new_k, new_v


def route(h, w_router, router_bias):
    N = h.shape[0]
    per_group = N_EXPERTS // N_EXPERT_GROUPS
    scores = jax.nn.sigmoid(jnp.matmul(h, w_router, preferred_element_type=jnp.float32))
    sel = scores + router_bias.astype(jnp.float32)
    group_score = jnp.sum(lax.top_k(sel.reshape(N, N_EXPERT_GROUPS, per_group), 2)[0], axis=-1)
    _, top_groups = lax.top_k(group_score, TOPK_GROUPS)
    gmask = jnp.zeros((N, N_EXPERT_GROUPS), bool).at[jnp.arange(N)[:, None], top_groups].set(True)
    sel = jnp.where(jnp.repeat(gmask, per_group, axis=1), sel, -jnp.inf)
    _, idx = lax.top_k(sel, TOP_K)
    w = jnp.take_along_axis(scores, idx, axis=-1)
    return idx, w / jnp.sum(w, axis=-1, keepdims=True) * ROUTED_SCALE


def routed_experts(h, idx, wts, we_gate, we_up, we_down):
    N, D = h.shape
    K = idx.shape[1]
    E = we_gate.shape[0]
    NK = N * K
    blk = EXPERT_BLOCK if NK >= E * EXPERT_BLOCK else SMALL_EXPERT_BLOCK
    n_blocks = NK // blk + E
    flat_e = idx.reshape(-1)
    flat_w = wts.reshape(-1)
    flat_tok = jnp.arange(NK, dtype=jnp.int32) // K
    order = jnp.argsort(flat_e)
    se = flat_e[order]
    counts = jnp.bincount(flat_e, length=E)
    pcounts = (counts + blk - 1) // blk * blk
    pend = jnp.cumsum(pcounts)
    pstart = pend - pcounts
    cstart = jnp.cumsum(counts) - counts
    dest = pstart[se] + jnp.arange(NK) - cstart[se]
    slot_tok = jnp.full((n_blocks * blk,), N, jnp.int32).at[dest].set(flat_tok[order])
    slot_w = jnp.zeros((n_blocks * blk,), jnp.float32).at[dest].set(flat_w[order])
    block_e = jnp.clip(jnp.searchsorted(pend, jnp.arange(n_blocks) * blk, side='right'), 0, E - 1)
    h_pad = jnp.concatenate([h, jnp.zeros((1, D), h.dtype)], axis=0)

    def run_block(args):
        tok_b, w_b, e = args
        xb = h_pad[tok_b]
        hid = jax.nn.silu(xb @ we_gate[e]) * (xb @ we_up[e])
        return (hid @ we_down[e]) * w_b.astype(xb.dtype)[:, None]

    outs = lax.map(run_block, (slot_tok.reshape(n_blocks, blk), slot_w.reshape(n_blocks, blk), block_e))
    y = jnp.zeros((N + 1, D), h.dtype).at[slot_tok].add(outs.reshape(-1, D).astype(h.dtype))
    return y[:N]


def moe(h, w_router, router_bias, we_gate, we_up, we_down, ws_gate, ws_up, ws_down):
    idx, wts = route(h, w_router, router_bias)
    shared = (jax.nn.silu(h @ ws_gate) * (h @ ws_up)) @ ws_down
    return routed_experts(h, idx, wts, we_gate, we_up, we_down) + shared


def setup_inputs(seed: int = 0) -> dict:
    key = jax.random.key(seed)
    ks = iter(list(jax.random.split(key, 48)))

    def nrm(shape, scale):
        return jax.random.normal(next(ks), shape, jnp.float32) * scale

    D, H, Dh, G = D_MODEL, HEADS_PER_GROUP, HEAD_DIM, N_DIL_GROUPS
    inp = {}
    inp['x_prompt'] = nrm((BATCH, SEQ, D), 1.0)
    inp['x_sample'] = nrm((DEC_BATCH, DEC_SEQ, D), 1.0)
    inp['state_pool'] = nrm((N_EVEN, DEC_BATCH, POOL_STATE, D_POOL), 1.0)
    inp['state_conv'] = nrm((N_EVEN, DEC_BATCH, CONV_STATE, D_CONV), 0.5)
    for g, (window, dil) in enumerate(DIL_CONFIGS):
        wb = min(window, PAST_LEN)
        inp['cache_k%d' % (g + 1)] = nrm((N_ODD, DEC_BATCH, wb, H, Dh), 1.0)
        inp['cache_v%d' % (g + 1)] = nrm((N_ODD, DEC_BATCH, wb, H, Dh), DEEPNORM_BETA)
    inp['w_in_ab'] = nrm((N_EVEN, D, D_POOL + 2 * D_CONV), D ** -0.5)
    inp['pool_w'] = nrm((N_EVEN, len(POOL_WINDOWS), POOL_GROUP, POOL_GROUP), POOL_GROUP ** -0.5)
    inp['pool_scale'] = 1.0 + nrm((N_EVEN, D_POOL), 0.02)
    inp['conv_w'] = nrm((N_EVEN, CONV_WIDTH, D_CONV), CONV_WIDTH ** -0.5)
    inp['conv_b'] = nrm((N_EVEN, D_CONV), 0.02)
    inp['conv_ln_g'] = 1.0 + nrm((N_EVEN, D_CONV), 0.02)
    inp['conv_ln_b'] = nrm((N_EVEN, D_CONV), 0.02)
    inp['w_out_ab'] = nrm((N_EVEN, D_POOL + D_CONV, D), (D_POOL + D_CONV) ** -0.5 * DEEPNORM_BETA)
    col_scale = jnp.concatenate([jnp.ones((2 * G * H * Dh,), jnp.float32),
                                 jnp.full((G * H * Dh,), DEEPNORM_BETA, jnp.float32)])
    inp['w_qkv'] = nrm((N_ODD, D, 3 * G * H * Dh), D ** -0.5) * col_scale
    inp['w_o'] = nrm((N_ODD, D_ATTN_OUT, D), D_ATTN_OUT ** -0.5 * DEEPNORM_BETA)
    inp['rel_bias'] = nrm((N_BUCKETS, G * H), 0.1)
    inp['ln1_g'] = 1.0 + nrm((DEPTH, D), 0.02)
    inp['ln1_b'] = nrm((DEPTH, D), 0.02)
    inp['ln2_g'] = 1.0 + nrm((DEPTH, D), 0.02)
    inp['ln2_b'] = nrm((DEPTH, D), 0.02)
    inp['w_router'] = nrm((DEPTH, D, N_EXPERTS), D ** -0.5)
    inp['router_bias'] = nrm((DEPTH, N_EXPERTS), 0.01)
    inp['we_gate'] = nrm((DEPTH, N_EXPERTS, D, D_EXPERT), D ** -0.5)
    inp['we_up'] = nrm((DEPTH, N_EXPERTS, D, D_EXPERT), D ** -0.5)
    inp['we_down'] = nrm((DEPTH, N_EXPERTS, D_EXPERT, D), D_EXPERT ** -0.5 * DEEPNORM_BETA)
    inp['ws_gate'] = nrm((DEPTH, D, D_SHARED), D ** -0.5)
    inp['ws_up'] = nrm((DEPTH, D, D_SHARED), D ** -0.5)
    inp['ws_down'] = nrm((DEPTH, D_SHARED, D), D_SHARED ** -0.5 * DEEPNORM_BETA)
    return inp


def reference(x_prompt, x_sample, state_pool, state_conv, cache_k1, cache_v1, cache_k2, cache_v2,
              cache_k3, cache_v3, w_in_ab, pool_w, pool_scale, conv_w, conv_b, conv_ln_g, conv_ln_b,
              w_out_ab, w_qkv, w_o, rel_bias, ln1_g, ln1_b, ln2_g, ln2_b, w_router, router_bias,
              we_gate, we_up, we_down, ws_gate, ws_up, ws_down):
    hp, hs = x_prompt, x_sample
    B, S, D = hp.shape
    DB, T, _ = hs.shape
    pool_p, conv_p, pool_s, conv_s = [], [], [], []
    kp = [[] for _ in range(N_DIL_GROUPS)]
    vp = [[] for _ in range(N_DIL_GROUPS)]
    ksm = [[] for _ in range(N_DIL_GROUPS)]
    vsm = [[] for _ in range(N_DIL_GROUPS)]
    for layer in range(DEPTH):
        j = layer // 2
        if layer % 2 == 0:
            ep = (w_in_ab[j], pool_w[j], pool_scale[j], conv_w[j], conv_b[j], conv_ln_g[j], conv_ln_b[j], w_out_ab[j])
            mp, p_pool, p_conv = pool_conv_mixer(hp, jnp.zeros((B, 0, D_POOL), hp.dtype),
                                                 jnp.zeros((B, CONV_STATE, D_CONV), hp.dtype), *ep)
            ms, s_pool, s_conv = pool_conv_mixer(hs, state_pool[j], state_conv[j], *ep)
            pool_p.append(p_pool)
            conv_p.append(p_conv)
            pool_s.append(s_pool)
            conv_s.append(s_conv)
        else:
            mp, nkp, nvp = dilated_mixer(hp, None, None, w_qkv[j], w_o[j], rel_bias)
            ms, nks, nvs = dilated_mixer(hs, (cache_k1[j], cache_k2[j], cache_k3[j]),
                                         (cache_v1[j], cache_v2[j], cache_v3[j]), w_qkv[j], w_o[j], rel_bias)
            for g in range(N_DIL_GROUPS):
                kp[g].append(nkp[g])
                vp[g].append(nvp[g])
                ksm[g].append(nks[g])
                vsm[g].append(nvs[g])
        hp = layer_norm(DEEPNORM_ALPHA * hp + mp, ln1_g[layer], ln1_b[layer])
        hs = layer_norm(DEEPNORM_ALPHA * hs + ms, ln1_g[layer], ln1_b[layer])
        tok = jnp.concatenate([hp.reshape(B * S, D), hs.reshape(DB * T, D)], axis=0)
        f = moe(tok, w_router[layer], router_bias[layer], we_gate[layer], we_up[layer], we_down[layer],
                ws_gate[layer], ws_up[layer], ws_down[layer])
        tok = layer_norm(DEEPNORM_ALPHA * tok + f, ln2_g[layer], ln2_b[layer])
        hp = tok[:B * S].reshape(B, S, D)
        hs = tok[B * S:].reshape(DB, T, D)
    return (hp, hs,
            jnp.stack(pool_p), jnp.stack(conv_p),
            jnp.stack(kp[0]), jnp.stack(vp[0]), jnp.stack(kp[1]), jnp.stack(vp[1]), jnp.stack(kp[2]), jnp.stack(vp[2]),
            jnp.stack(pool_s), jnp.stack(conv_s),
            jnp.stack(ksm[0]), jnp.stack(vsm[0]), jnp.stack(ksm[1]), jnp.stack(vsm[1]), jnp.stack(ksm[2]), jnp.stack(vsm[2]))
```

```python
import functools
import math

import jax
import jax.numpy as jnp
from jax import lax
from jax.experimental import pallas as pl
from jax.experimental.pallas import tpu as pltpu

F32 = jnp.float32
BF16 = jnp.bfloat16
I32 = jnp.int32

D_MODEL = 1024
D_POOL = 512
D_CONV = 512
POOL_WINDOWS = (2, 4, 8, 16)
POOL_GROUP = 128
POOL_STATE = 15
CONV_WIDTH = 31
CONV_STATE = 30
DIL_CONFIGS = ((128, 1), (512, 4), (2048, 16))
N_GROUPS = 3
HEADS = 8
HEAD_DIM = 64
D_ATTN = HEADS * HEAD_DIM
QUERY_BLOCK = 128
N_BUCKETS = 32
MAX_DISTANCE = 2048
N_EXPERTS = 256
TOP_K = 8
N_EXPERT_GROUPS = 8
EXPERTS_PER_GROUP = N_EXPERTS // N_EXPERT_GROUPS
TOPK_GROUPS = 4
D_EXPERT = 256
ROUTED_SCALE = 2.5
DEPTH = 2
DEEPNORM_ALPHA = (2.0 * DEPTH) ** 0.25
LN_EPS = 1e-5
NEG_INF = -1e30

LANES = 128
SUBLANES = 8
VMEM_LIMIT_BYTES = 56 * 1024 * 1024

POOL_HALO = 16
CONV_HALO = 32
L0_TIME_TILE = 256
ROUTER_TILE = 384
DISPATCH_TILE = 384
COMBINE_TILE = 128
EXPERT_BLOCK = 128
MERGE_TILE = 512


def _cparams(sem):
    return pltpu.CompilerParams(dimension_semantics=sem, vmem_limit_bytes=VMEM_LIMIT_BYTES)


def _layer_norm(x, g, b):
    mu = jnp.mean(x, axis=-1, keepdims=True)
    xc = x - mu
    var = jnp.mean(xc * xc, axis=-1, keepdims=True)
    return xc * lax.rsqrt(var + LN_EPS) * g + b


def _const_spec(shape):
    nd = len(shape)
    return pl.BlockSpec(shape, lambda *_: (0,) * nd)


def _l0_prompt_kernel(x_ref, win_ref, pw_ref, ps_ref, cw_ref, cb_ref, cg_ref, cbeta_ref, wout_ref,
                      g1_ref, b1_ref, h_ref, pstate_ref, cstate_ref, ue_ref, ge_ref):
    tt = x_ref.shape[1]
    t = pl.program_id(1)

    @pl.when(t == 0)
    def _():
        ue_ref[0:POOL_HALO, :] = jnp.zeros((POOL_HALO, D_POOL), F32)
        ge_ref[0:CONV_HALO, :] = jnp.zeros((CONV_HALO, D_CONV), F32)

    @pl.when(t > 0)
    def _():
        ue_ref[0:POOL_HALO, :] = ue_ref[tt:tt + POOL_HALO, :]
        ge_ref[0:CONV_HALO, :] = ge_ref[tt:tt + CONV_HALO, :]

    x = x_ref[0]
    proj = jnp.dot(x.astype(BF16), win_ref[...], preferred_element_type=F32)
    u = proj[:, :D_POOL]
    a = proj[:, D_POOL:D_POOL + D_CONV]
    gate = proj[:, D_POOL + D_CONV:]
    glu = a * jax.nn.sigmoid(gate)
    ue_ref[POOL_HALO:POOL_HALO + tt, :] = u
    ge_ref[CONV_HALO:CONV_HALO + tt, :] = glu

    tg = t * tt + lax.broadcasted_iota(I32, (tt, 1), 0)
    parts = []
    for g, w in enumerate(POOL_WINDOWS):
        c0 = g * POOL_GROUP
        ug = u[:, c0:c0 + POOL_GROUP]
        s = ug
        for j in range(1, w):
            s = s + ue_ref[POOL_HALO - j:POOL_HALO - j + tt, c0:c0 + POOL_GROUP]
        cnt = jnp.minimum(tg + 1, w).astype(F32)
        pooled = s / cnt - ug
        parts.append(jnp.dot(pooled.astype(BF16), pw_ref[g], preferred_element_type=F32))
    yp = jnp.concatenate(parts, axis=1) * ps_ref[...]

    acc = glu * cw_ref[CONV_STATE:CONV_STATE + 1, :]
    off = CONV_HALO - CONV_STATE
    for j in range(CONV_STATE):
        acc = acc + ge_ref[off + j:off + j + tt, :] * cw_ref[j:j + 1, :]
    yn = _layer_norm(acc + cb_ref[...], cg_ref[...], cbeta_ref[...])
    yc = yn * jax.nn.sigmoid(yn)

    cat = jnp.concatenate([yp, yc], axis=1).astype(BF16)
    m = jnp.dot(cat, wout_ref[...], preferred_element_type=F32)
    h_ref[0] = _layer_norm(DEEPNORM_ALPHA * x + m, g1_ref[...], b1_ref[...])
    pstate_ref[0] = ue_ref[tt:tt + POOL_HALO, :]
    cstate_ref[0] = ge_ref[tt:tt + CONV_HALO, :]


def _l0_prompt(x, win, pw, ps, cw, cb, cg, cbeta, wout, g1, b1):
    B, S, D = x.shape
    tt = min(L0_TIME_TILE, S)
    assert S % tt == 0 and tt >= CONV_HALO
    return pl.pallas_call(
        _l0_prompt_kernel,
        grid=(B, S // tt),
        in_specs=[
            pl.BlockSpec((1, tt, D), lambda b, t: (b, t, 0)),
            _const_spec(win.shape), _const_spec(pw.shape), _const_spec(ps.shape), _const_spec(cw.shape),
            _const_spec(cb.shape), _const_spec(cg.shape), _const_spec(cbeta.shape), _const_spec(wout.shape),
            _const_spec(g1.shape), _const_spec(b1.shape),
        ],
        out_specs=[
            pl.BlockSpec((1, tt, D), lambda b, t: (b, t, 0)),
            pl.BlockSpec((1, POOL_HALO, D_POOL), lambda b, t: (b, 0, 0)),
            pl.BlockSpec((1, CONV_HALO, D_CONV), lambda b, t: (b, 0, 0)),
        ],
        out_shape=[
            jax.ShapeDtypeStruct((B, S, D), F32),
            jax.ShapeDtypeStruct((B, POOL_HALO, D_POOL), F32),
            jax.ShapeDtypeStruct((B, CONV_HALO, D_CONV), F32),
        ],
        scratch_shapes=[pltpu.VMEM((tt + POOL_HALO, D_POOL), F32), pltpu.VMEM((tt + CONV_HALO, D_CONV), F32)],
        compiler_params=_cparams(("arbitrary", "arbitrary")),
        name="l0_prompt",
    )(x, win, pw, ps, cw, cb, cg, cbeta, wout, g1, b1)


def _l0_sample_kernel(x_ref, sp_ref, sc_ref, win_ref, pw_ref, ps_ref, cw_ref, cb_ref, cg_ref, cbeta_ref,
                      wout_ref, g1_ref, b1_ref, h_ref, u_ref, glu_ref):
    x = x_ref[...]
    proj = jnp.dot(x.astype(BF16), win_ref[...], preferred_element_type=F32)
    u = proj[:, :D_POOL]
    a = proj[:, D_POOL:D_POOL + D_CONV]
    gate = proj[:, D_POOL + D_CONV:]
    glu = a * jax.nn.sigmoid(gate)
    u_ref[...] = u
    glu_ref[...] = glu

    parts = []
    for g, w in enumerate(POOL_WINDOWS):
        c0 = g * POOL_GROUP
        ug = u[:, c0:c0 + POOL_GROUP]
        past = sp_ref[:, POOL_STATE - (w - 1):POOL_STATE, c0:c0 + POOL_GROUP]
        s = ug + jnp.sum(past, axis=1)
        pooled = s / float(w) - ug
        parts.append(jnp.dot(pooled.astype(BF16), pw_ref[g], preferred_element_type=F32))
    yp = jnp.concatenate(parts, axis=1) * ps_ref[...]

    acc = glu * cw_ref[CONV_STATE:CONV_STATE + 1, :]
    acc = acc + jnp.sum(sc_ref[...] * cw_ref[0:CONV_STATE, :][None, :, :], axis=1)
    yn = _layer_norm(acc + cb_ref[...], cg_ref[...], cbeta_ref[...])
    yc = yn * jax.nn.sigmoid(yn)

    cat = jnp.concatenate([yp, yc], axis=1).astype(BF16)
    m = jnp.dot(cat, wout_ref[...], preferred_element_type=F32)
    h_ref[...] = _layer_norm(DEEPNORM_ALPHA * x + m, g1_ref[...], b1_ref[...])


def _l0_sample(x, sp, sc, win, pw, ps, cw, cb, cg, cbeta, wout, g1, b1):
    DB, D = x.shape
    args = (x, sp, sc, win, pw, ps, cw, cb, cg, cbeta, wout, g1, b1)
    return pl.pallas_call(
        _l0_sample_kernel,
        grid=(1,),
        in_specs=[_const_spec(a.shape) for a in args],
        out_specs=[_const_spec((DB, D)), _const_spec((DB, D_POOL)), _const_spec((DB, D_CONV))],
        out_shape=[
            jax.ShapeDtypeStruct((DB, D), F32),
            jax.ShapeDtypeStruct((DB, D_POOL), F32),
            jax.ShapeDtypeStruct((DB, D_CONV), F32),
        ],
        compiler_params=_cparams(("arbitrary",)),
        name="l0_sample",
    )(*args)


def _first_index_of_max(x, iota, size):
    m = jnp.max(x, axis=0, keepdims=True)
    f = jnp.min(jnp.where(x == m, iota, size), axis=0, keepdims=True)
    return m, f


def _router_kernel(h_ref, wrt_ref, bias_ref, tri_ref, idx_ref, w_ref, rank_ref, cnt_ref, run_ref):
    tm = h_ref.shape[0]
    E = N_EXPERTS
    PG = EXPERTS_PER_GROUP

    @pl.when(pl.program_id(0) == 0)
    def _():
        run_ref[...] = jnp.zeros_like(run_ref)

    logits = lax.dot_general(wrt_ref[...], h_ref[...].astype(BF16), (((1,), (1,)), ((), ())),
                             preferred_element_type=F32)
    scores = jax.nn.sigmoid(logits)
    sel = scores + bias_ref[...]

    io_g = lax.broadcasted_iota(I32, (PG, tm), 0)
    rows = []
    for g in range(N_EXPERT_GROUPS):
        blk = sel[g * PG:(g + 1) * PG, :]
        m1, f1 = _first_index_of_max(blk, io_g, PG)
        m2 = jnp.max(jnp.where(io_g == f1, -jnp.inf, blk), axis=0, keepdims=True)
        rows.append(m1 + m2)
    gs = jnp.concatenate(rows, axis=0)

    io_n = lax.broadcasted_iota(I32, (N_EXPERT_GROUPS, tm), 0)
    gsel = jnp.zeros((N_EXPERT_GROUPS, tm), F32)
    cur = gs
    for _ in range(TOPK_GROUPS):
        _, f = _first_index_of_max(cur, io_n, N_EXPERT_GROUPS)
        hit = io_n == f
        gsel = jnp.where(hit, 1.0, gsel)
        cur = jnp.where(hit, -jnp.inf, cur)
    masked = jnp.concatenate(
        [jnp.where(gsel[g:g + 1, :] > 0.5, sel[g * PG:(g + 1) * PG, :], -jnp.inf) for g in range(N_EXPERT_GROUPS)],
        axis=0)

    io_e = lax.broadcasted_iota(I32, (E, tm), 0)
    onehot = jnp.zeros((E, tm), F32)
    idx_rows, sc_rows = [], []
    cur = masked
    for _ in range(TOP_K):
        _, f = _first_index_of_max(cur, io_e, E)
        hit = io_e == f
        idx_rows.append(f)
        sc_rows.append(jnp.sum(jnp.where(hit, scores, 0.0), axis=0, keepdims=True))
        onehot = jnp.where(hit, 1.0, onehot)
        cur = jnp.where(hit, -jnp.inf, cur)
    sc = jnp.concatenate(sc_rows, axis=0)
    idx_ref[...] = jnp.concatenate(idx_rows, axis=0)
    w_ref[...] = sc / jnp.sum(sc, axis=0, keepdims=True) * ROUTED_SCALE

    before = jnp.dot(onehot.astype(BF16), tri_ref[...], preferred_element_type=F32) + run_ref[...]
    rank_rows = [jnp.sum(jnp.where(io_e == f, before, 0.0), axis=0, keepdims=True) for f in idx_rows]
    rank_ref[...] = jnp.concatenate(rank_rows, axis=0).astype(I32)
    run_ref[...] = run_ref[...] + jnp.sum(onehot, axis=1, keepdims=True)
    cnt_ref[...] = run_ref[...]


def _router(tok, wrt, bias):
    N, D = tok.shape
    tm = ROUTER_TILE
    assert N % tm == 0
    tri = (jnp.arange(tm)[:, None] < jnp.arange(tm)[None, :]).astype(BF16)
    return pl.pallas_call(
        _router_kernel,
        grid=(N // tm,),
        in_specs=[
            pl.BlockSpec((tm, D), lambda i: (i, 0)),
            _const_spec(wrt.shape), _const_spec(bias.shape), _const_spec(tri.shape),
        ],
        out_specs=[
            pl.BlockSpec((TOP_K, tm), lambda i: (0, i)),
            pl.BlockSpec((TOP_K, tm), lambda i: (0, i)),
            pl.BlockSpec((TOP_K, tm), lambda i: (0, i)),
            _const_spec((N_EXPERTS, 1)),
        ],
        out_shape=[
            jax.ShapeDtypeStruct((TOP_K, N), I32),
            jax.ShapeDtypeStruct((TOP_K, N), F32),
            jax.ShapeDtypeStruct((TOP_K, N), I32),
            jax.ShapeDtypeStruct((N_EXPERTS, 1), F32),
        ],
        scratch_shapes=[pltpu.VMEM((N_EXPERTS, 1), F32)],
        compiler_params=_cparams(("arbitrary",)),
        name="moe_router",
    )(tok, wrt, bias, tri)


def _dispatch_kernel(dest_hbm, pend_ref, cnt_ref, tok_hbm, xs_hbm, idx_smem, zero_ref, sem_idx, sem_rows):
    i = pl.program_id(0)
    tm = idx_smem.shape[0] // TOP_K
    blk = zero_ref.shape[0]

    @pl.when(i == 0)
    def _():
        zero_ref[...] = jnp.zeros_like(zero_ref)

        def zero_copy(e):
            last_block = pl.multiple_of(pend_ref[e] - blk, blk)
            return pltpu.make_async_copy(zero_ref, xs_hbm.at[pl.ds(last_block, blk)], sem_rows)

        def start(e, c):
            @pl.when(cnt_ref[e] > 0)
            def _():
                zero_copy(e).start()
            return c

        def wait(e, c):
            @pl.when(cnt_ref[e] > 0)
            def _():
                zero_copy(e).wait()
            return c

        lax.fori_loop(0, N_EXPERTS, start, 0)
        lax.fori_loop(0, N_EXPERTS, wait, 0)

    idx_copy = pltpu.make_async_copy(dest_hbm.at[i], idx_smem, sem_idx)
    idx_copy.start()
    idx_copy.wait()

    def row_copy(t, k):
        return pltpu.make_async_copy(tok_hbm.at[pl.ds(i * tm + t, 1)],
                                     xs_hbm.at[pl.ds(idx_smem[t * TOP_K + k], 1)], sem_rows)

    def start(t, c):
        for k in range(TOP_K):
            row_copy(t, k).start()
        return c

    def wait(t, c):
        for k in range(TOP_K):
            row_copy(t, k).wait()
        return c

    lax.fori_loop(0, tm, start, 0)
    lax.fori_loop(0, tm, wait, 0)


def _dispatch(tok, dest_tiles, pend, counts, n_slots):
    N, D = tok.shape
    tm = DISPATCH_TILE
    assert N % tm == 0 and dest_tiles.shape == (N // tm, tm * TOP_K)
    return pl.pallas_call(
        _dispatch_kernel,
        grid_spec=pltpu.PrefetchScalarGridSpec(
            num_scalar_prefetch=0,
            grid=(N // tm,),
            in_specs=[
                pl.BlockSpec(memory_space=pl.ANY),
                pl.BlockSpec(memory_space=pltpu.SMEM),
                pl.BlockSpec(memory_space=pltpu.SMEM),
                pl.BlockSpec(memory_space=pl.ANY),
            ],
            out_specs=pl.BlockSpec(memory_space=pl.ANY),
            scratch_shapes=[
                pltpu.SMEM((tm * TOP_K,), I32),
                pltpu.VMEM((EXPERT_BLOCK, D), F32),
                pltpu.SemaphoreType.DMA,
                pltpu.SemaphoreType.DMA,
            ],
        ),
        out_shape=jax.ShapeDtypeStruct((n_slots, D), F32),
        compiler_params=_cparams(("arbitrary",)),
        name="moe_dispatch",
    )(dest_tiles, pend, counts, tok)


def _ffn_kernel(be_ref, nu_ref, xs_ref, wg_ref, wu_ref, wd_ref, ys_ref, wgu_s, wd_s):
    b = pl.program_id(0)
    prev = be_ref[jnp.maximum(b - 1, 0)]
    new_expert = jnp.logical_or(b == 0, be_ref[b] != prev)

    @pl.when(jnp.logical_and(new_expert, b < nu_ref[0]))
    def _():
        wgu_s[:, :D_EXPERT] = wg_ref[0].astype(BF16)
        wgu_s[:, D_EXPERT:] = wu_ref[0].astype(BF16)
        wd_s[...] = wd_ref[0].astype(BF16)

    @pl.when(b < nu_ref[0])
    def _():
        x = xs_ref[...].astype(BF16)
        gu = jnp.dot(x, wgu_s[...], preferred_element_type=F32)
        gt = gu[:, :D_EXPERT]
        hid = gt * jax.nn.sigmoid(gt) * gu[:, D_EXPERT:]
        ys_ref[...] = jnp.dot(hid.astype(BF16), wd_s[...], preferred_element_type=F32)


def _expert_ffn(xs, block_e, n_used, wg, wu, wd):
    n_slots, D = xs.shape
    blk = EXPERT_BLOCK
    n_blocks = n_slots // blk

    def row_map(b, be, nu):
        return (jnp.minimum(b, nu[0] - 1), 0)

    def w_map(b, be, nu):
        return (be[b], 0, 0)

    return pl.pallas_call(
        _ffn_kernel,
        grid_spec=pltpu.PrefetchScalarGridSpec(
            num_scalar_prefetch=2,
            grid=(n_blocks,),
            in_specs=[
                pl.BlockSpec((blk, D), row_map),
                pl.BlockSpec((1, D, D_EXPERT), w_map),
                pl.BlockSpec((1, D, D_EXPERT), w_map),
                pl.BlockSpec((1, D_EXPERT, D), w_map),
            ],
            out_specs=pl.BlockSpec((blk, D), row_map),
            scratch_shapes=[pltpu.VMEM((D, 2 * D_EXPERT), BF16), pltpu.VMEM((D_EXPERT, D), BF16)],
        ),
        out_shape=jax.ShapeDtypeStruct((n_slots, D), F32),
        compiler_params=_cparams(("arbitrary",)),
        name="moe_ffn",
    )(block_e, n_used, xs, wg, wu, wd)


def _combine_kernel(dest_hbm, ys_hbm, tok_ref, w_ref, wsg_ref, wsu_ref, wsd_ref, g_ref, b_ref, out_ref,
                    idx_smem, buf_ref, sem_idx, sem_rows):
    i = pl.program_id(0)
    tm = tok_ref.shape[0]

    idx_copy = pltpu.make_async_copy(dest_hbm.at[i], idx_smem, sem_idx)
    idx_copy.start()
    idx_copy.wait()

    def row_copy(t, k):
        return pltpu.make_async_copy(ys_hbm.at[pl.ds(idx_smem[t * TOP_K + k], 1)],
                                     buf_ref.at[k, pl.ds(t, 1)], sem_rows)

    def start(t, c):
        for k in range(TOP_K):
            row_copy(t, k).start()
        return c

    def wait(t, c):
        for k in range(TOP_K):
            row_copy(t, k).wait()
        return c

    lax.fori_loop(0, tm, start, 0)

    h = tok_ref[...]
    hb = h.astype(BF16)
    gt = jnp.dot(hb, wsg_ref[...], preferred_element_type=F32)
    up = jnp.dot(hb, wsu_ref[...], preferred_element_type=F32)
    hid = gt * jax.nn.sigmoid(gt) * up
    f = jnp.dot(hid.astype(BF16), wsd_ref[...], preferred_element_type=F32)

    lax.fori_loop(0, tm, wait, 0)
    w = w_ref[...]
    for k in range(TOP_K):
        f = f + buf_ref[k] * w[:, k:k + 1]
    out_ref[...] = _layer_norm(DEEPNORM_ALPHA * h + f, g_ref[...], b_ref[...])


def _combine(tok, ys, dest_tiles, w_tok, wsg, wsu, wsd, g, b):
    N, D = tok.shape
    tm = COMBINE_TILE
    assert N % tm == 0 and dest_tiles.shape == (N // tm, tm * TOP_K)
    return pl.pallas_call(
        _combine_kernel,
        grid_spec=pltpu.PrefetchScalarGridSpec(
            num_scalar_prefetch=0,
            grid=(N // tm,),
            in_specs=[
                pl.BlockSpec(memory_space=pl.ANY),
                pl.BlockSpec(memory_space=pl.ANY),
                pl.BlockSpec((tm, D), lambda i: (i, 0)),
                pl.BlockSpec((tm, TOP_K), lambda i: (i, 0)),
                _const_spec(wsg.shape), _const_spec(wsu.shape), _const_spec(wsd.shape),
                _const_spec(g.shape), _const_spec(b.shape),
            ],
            out_specs=pl.BlockSpec((tm, D), lambda i: (i, 0)),
            scratch_shapes=[
                pltpu.SMEM((tm * TOP_K,), I32),
                pltpu.VMEM((TOP_K, tm, D), F32),
                pltpu.SemaphoreType.DMA,
                pltpu.SemaphoreType.DMA,
            ],
        ),
        out_shape=jax.ShapeDtypeStruct((N, D), F32),
        compiler_params=_cparams(("arbitrary",)),
        name="moe_combine",
    )(dest_tiles, ys, tok, w_tok, wsg, wsu, wsd, g, b)


def _moe_layer(tok, w_router, router_bias, we_gate, we_up, we_down, ws_gate, ws_up, ws_down, ln_g, ln_b):
    N, D = tok.shape
    blk = EXPERT_BLOCK
    idx_t, w_t, rank_t, cnt = _router(tok, w_router.T.astype(BF16), router_bias.reshape(N_EXPERTS, 1))

    counts = cnt[:, 0].astype(I32)
    pcounts = (counts + blk - 1) // blk * blk
    pend = jnp.cumsum(pcounts)
    pstart = pend - pcounts
    dest = (pstart[idx_t] + rank_t).T
    n_blocks = N * TOP_K // blk + N_EXPERTS
    block_e = jnp.clip(jnp.searchsorted(pend, jnp.arange(n_blocks, dtype=I32) * blk, side='right'),
                       0, N_EXPERTS - 1).astype(I32)
    n_used = (pend[-1] // blk).reshape(1).astype(I32)

    xs = _dispatch(tok, dest.reshape(N // DISPATCH_TILE, DISPATCH_TILE * TOP_K), pend.astype(I32), counts,
                   n_blocks * blk)
    ys = _expert_ffn(xs, block_e, n_used, we_gate, we_up, we_down)
    return _combine(tok, ys, dest.reshape(N // COMBINE_TILE, COMBINE_TILE * TOP_K), w_t.T,
                    ws_gate.astype(BF16), ws_up.astype(BF16), ws_down.astype(BF16),
                    ln_g.reshape(1, D), ln_b.reshape(1, D))


def _rel_bucket(dist):
    exact = N_BUCKETS // 2
    df = jnp.maximum(dist, 1).astype(F32)
    large = exact + (jnp.log(df / exact) / math.log(MAX_DISTANCE / exact) * (N_BUCKETS - exact)).astype(I32)
    large = jnp.minimum(large, N_BUCKETS - 1)
    return jnp.where(dist < exact, dist, large)


def _qkv_kernel(x_ref, w_ref, q_ref, k_ref, v_ref, kn_ref, vn_ref):
    y = jnp.dot(x_ref[0].astype(BF16), w_ref[...], preferred_element_type=F32)
    q_ref[0] = y[:, :D_ATTN].astype(BF16)
    k = y[:, D_ATTN:2 * D_ATTN]
    v = y[:, 2 * D_ATTN:]
    k_ref[0] = k.astype(BF16)
    v_ref[0] = v.astype(BF16)
    kn_ref[0] = k
    vn_ref[0] = v


def _qkv_prompt(h, w_g, dil):
    B, S, D = h.shape
    L = S // dil
    tl = min(L, 512)
    assert S % dil == 0 and L % tl == 0
    hv = h.reshape(B, L, dil * D)
    ph_spec = pl.BlockSpec((1, tl, D_ATTN), lambda b, r, l: (b * dil + r, l, 0))
    nat_spec = pl.BlockSpec((1, tl, D_ATTN), lambda b, r, l: (b, l, r))
    q, k, v, kn, vn = pl.pallas_call(
        _qkv_kernel,
        grid=(B, dil, L // tl),
        in_specs=[pl.BlockSpec((1, tl, D), lambda b, r, l: (b, l, r)), _const_spec(w_g.shape)],
        out_specs=[ph_spec, ph_spec, ph_spec, nat_spec, nat_spec],
        out_shape=[jax.ShapeDtypeStruct((B * dil, L, D_ATTN), BF16)] * 3
        + [jax.ShapeDtypeStruct((B, L, dil * D_ATTN), F32)] * 2,
        compiler_params=_cparams(("arbitrary", "arbitrary", "arbitrary")),
        name="qkv_prompt",
    )(hv, w_g)
    return q, k, v, kn.reshape(B, S, D_ATTN), vn.reshape(B, S, D_ATTN)


def _attn_prompt_kernel(q_ref, kp_ref, kc_ref, vp_ref, vc_ref, bias_ref, o_ref, lse_ref, *, steps):
    n = pl.program_id(1)
    qb = QUERY_BLOCK
    qi = lax.broadcasted_iota(I32, (qb, 2 * qb), 0)
    kj = lax.broadcasted_iota(I32, (qb, 2 * qb), 1)
    dist = qi + qb - kj
    valid = (dist >= 0) & (dist <= steps) & ((n > 0) | (kj >= qb))
    low = lax.broadcasted_iota(I32, (1, 2 * HEAD_DIM), 1) < HEAD_DIM
    scale = HEAD_DIM ** -0.5
    for p in range(HEADS // 2):
        c0 = p * 2 * HEAD_DIM
        q2 = q_ref[0, :, c0:c0 + 2 * HEAD_DIM]
        k2 = jnp.concatenate([kp_ref[0, :, c0:c0 + 2 * HEAD_DIM], kc_ref[0, :, c0:c0 + 2 * HEAD_DIM]], axis=0)
        v2 = jnp.concatenate([vp_ref[0, :, c0:c0 + 2 * HEAD_DIM], vc_ref[0, :, c0:c0 + 2 * HEAD_DIM]], axis=0)
        outs, lses = [], []
        for half in range(2):
            keep = low if half == 0 else jnp.logical_not(low)
            qh = jnp.where(keep, q2, jnp.zeros_like(q2))
            s = lax.dot_general(qh, k2, (((1,), (1,)), ((), ())), preferred_element_type=F32)
            s = s * scale + bias_ref[2 * p + half]
            s = jnp.where(valid, s, NEG_INF)
            m = jnp.max(s, axis=-1, keepdims=True)
            e = jnp.exp(s - m)
            l = jnp.sum(e, axis=-1, keepdims=True)
            prob = e / l
            outs.append(jnp.dot(prob.astype(BF16), v2, preferred_element_type=F32))
            lses.append(m + jnp.log(l))
        o_ref[0, :, c0:c0 + 2 * HEAD_DIM] = jnp.where(low, outs[0], outs[1])
        lse_ref[0, :, c0:c0 + 2 * HEAD_DIM] = jnp.where(low, lses[0], lses[1])


def _attn_prompt(q, k, v, bias, B, dil, steps):
    BD, L, _ = q.shape
    qb = QUERY_BLOCK
    assert L % qb == 0
    nb = L // qb
    cur = pl.BlockSpec((1, qb, D_ATTN), lambda bd, n: (bd, n, 0))
    prev = pl.BlockSpec((1, qb, D_ATTN), lambda bd, n: (bd, jnp.maximum(n - 1, 0), 0))
    nat = pl.BlockSpec((1, qb, D_ATTN), lambda bd, n: (bd // dil, n, bd % dil))
    o, lse = pl.pallas_call(
        functools.partial(_attn_prompt_kernel, steps=steps),
        grid=(BD, nb),
        in_specs=[cur, prev, cur, prev, cur, _const_spec(bias.shape)],
        out_specs=[nat, nat],
        out_shape=[jax.ShapeDtypeStruct((B, L, dil * D_ATTN), F32)] * 2,
        compiler_params=_cparams(("arbitrary", "arbitrary")),
        name="attn_prompt",
    )(q, k, k, v, v, bias)
    return o.reshape(B, L * dil, D_ATTN), lse.reshape(B, L * dil, D_ATTN)


def _merge_kernel(o1, o2, o3, l1, l2, l3, h_ref, wo_ref, g_ref, b_ref, out_ref):
    a1, a2, a3 = l1[...], l2[...], l3[...]
    m = jnp.maximum(jnp.maximum(a1, a2), a3)
    e1, e2, e3 = jnp.exp(a1 - m), jnp.exp(a2 - m), jnp.exp(a3 - m)
    o = (e1 * o1[...] + e2 * o2[...] + e3 * o3[...]) / (e1 + e2 + e3)
    y = jnp.dot(o.astype(BF16), wo_ref[...], preferred_element_type=F32)
    out_ref[...] = _layer_norm(DEEPNORM_ALPHA * h_ref[...] + y, g_ref[...], b_ref[...])


def _merge(os_, ls_, h, wo, g, b):
    M, D = h.shape
    tm = min(MERGE_TILE, M)
    assert M % tm == 0
    a_spec = pl.BlockSpec((tm, D_ATTN), lambda i: (i, 0))
    return pl.pallas_call(
        _merge_kernel,
        grid=(M // tm,),
        in_specs=[a_spec] * 6 + [pl.BlockSpec((tm, D), lambda i: (i, 0)), _const_spec(wo.shape),
                                 _const_spec(g.shape), _const_spec(b.shape)],
        out_specs=pl.BlockSpec((tm, D), lambda i: (i, 0)),
        out_shape=jax.ShapeDtypeStruct((M, D), F32),
        compiler_params=_cparams(("arbitrary",)),
        name="attn_merge",
    )(*os_, *ls_, h, wo, g, b)


def _mm_kernel(x_ref, w_ref, o_ref):
    o_ref[...] = jnp.dot(x_ref[...].astype(BF16), w_ref[...], preferred_element_type=F32)


def _matmul(x, w, tn):
    M, K = x.shape
    _, N = w.shape
    assert N % tn == 0
    return pl.pallas_call(
        _mm_kernel,
        grid=(N // tn,),
        in_specs=[_const_spec((M, K)), pl.BlockSpec((K, tn), lambda j: (0, j))],
        out_specs=pl.BlockSpec((M, tn), lambda j: (0, j)),
        out_shape=jax.ShapeDtypeStruct((M, N), F32),
        compiler_params=_cparams(("arbitrary",)),
        name="matmul",
    )(x, w)


def _round_bf16(x):
    return x.astype(BF16).astype(F32)


def _split_dot(x, w):
    hi = x.astype(BF16)
    lo = (x - hi.astype(F32)).astype(BF16)
    return jnp.dot(hi, w, preferred_element_type=F32) + jnp.dot(lo, w, preferred_element_type=F32)


def _attn_sample_kernel(qkv_ref, k1, v1, k2, v2, k3, v3, bk_ref, b0_ref, seg_ref, segt_ref, h_ref, wo_ref,
                        g_ref, b_ref, out_ref):
    tb = qkv_ref.shape[0]
    W = QUERY_BLOCK
    seg = seg_ref[...]
    segt = segt_ref[...]
    scale = HEAD_DIM ** -0.5
    caches = ((k1, v1), (k2, v2), (k3, v3))
    outs, lses = [], []
    for g in range(N_GROUPS):
        q = _round_bf16(qkv_ref[:, g * D_ATTN:(g + 1) * D_ATTN])
        kn = _round_bf16(qkv_ref[:, (N_GROUPS + g) * D_ATTN:(N_GROUPS + g + 1) * D_ATTN])
        vn = _round_bf16(qkv_ref[:, (2 * N_GROUPS + g) * D_ATTN:(2 * N_GROUPS + g + 1) * D_ATTN])
        kc = _round_bf16(caches[g][0][...])
        vc = _round_bf16(caches[g][1][...])
        prod = (kc * q[:, None, :]).reshape(tb * W, D_ATTN)
        s = _split_dot(prod, seg).reshape(tb, W, LANES) * scale + bk_ref[g][None, :, :]
        s0 = _split_dot(q * kn, seg) * scale + b0_ref[g]
        m = jnp.maximum(jnp.max(s, axis=1), s0)
        e = jnp.exp(s - m[:, None, :])
        e0 = jnp.exp(s0 - m)
        l = jnp.sum(e, axis=1) + e0
        prob = (e / l[:, None, :]).reshape(tb * W, LANES)
        pe = jnp.dot(prob.astype(BF16), segt, preferred_element_type=F32).reshape(tb, W, D_ATTN)
        p0 = jnp.dot((e0 / l).astype(BF16), segt, preferred_element_type=F32)
        outs.append(jnp.sum(pe * vc, axis=1) + p0 * vn)
        lses.append(m + jnp.log(l))
    mx = jnp.maximum(jnp.maximum(lses[0], lses[1]), lses[2])
    es = [jnp.exp(x - mx) for x in lses]
    tot = es[0] + es[1] + es[2]
    o = sum(_split_dot(es[g] / tot, segt) * outs[g] for g in range(N_GROUPS))
    y = jnp.dot(o.astype(BF16), wo_ref[...], preferred_element_type=F32)
    out_ref[...] = _layer_norm(DEEPNORM_ALPHA * h_ref[...] + y, g_ref[...], b_ref[...])


def _attn_sample(qkv, caches, bias_keys, bias_self, h, wo, g, b, tb=SUBLANES):
    DB, D = h.shape
    W = QUERY_BLOCK
    assert DB % tb == 0
    lane_head = jnp.arange(D_ATTN)[:, None] // HEAD_DIM == jnp.arange(LANES)[None, :]
    seg = lane_head.astype(BF16)
    segt = lane_head.T.astype(BF16)
    cache_spec = pl.BlockSpec((tb, W, D_ATTN), lambda i: (i, 0, 0))
    flat = [c for kv in caches for c in kv]
    return pl.pallas_call(
        _attn_sample_kernel,
        grid=(DB // tb,),
        in_specs=[pl.BlockSpec((tb, qkv.shape[1]), lambda i: (i, 0))] + [cache_spec] * 6
        + [_const_spec(bias_keys.shape), _const_spec(bias_self.shape), _const_spec(seg.shape),
           _const_spec(segt.shape), pl.BlockSpec((tb, D), lambda i: (i, 0)), _const_spec(wo.shape),
           _const_spec(g.shape), _const_spec(b.shape)],
        out_specs=pl.BlockSpec((tb, D), lambda i: (i, 0)),
        out_shape=jax.ShapeDtypeStruct((DB, D), F32),
        compiler_params=_cparams(("arbitrary",)),
        name="attn_sample",
    )(qkv, *flat, bias_keys, bias_self, seg, segt, h, wo, g, b)


def _dilated_layer(hp, hs, caches, w_qkv, w_o, rel_bias, g1, b1):
    B, S, D = hp.shape
    DB = hs.shape[0]
    w_qkv_b = w_qkv.astype(BF16)
    w_o_b = w_o.astype(BF16)
    qb = QUERY_BLOCK
    qi = jnp.arange(qb)[:, None]
    kj = jnp.arange(2 * qb)[None, :]
    dist = qi + qb - kj
    os_, ls_, kp, vp = [], [], [], []
    bias_keys, bias_self, cache_views = [], [], []
    for g, (window, dil) in enumerate(DIL_CONFIGS):
        steps = window // dil
        assert steps == qb
        tab = rel_bias[:, g * HEADS:(g + 1) * HEADS]
        cols = [w_qkv_b[:, (j * N_GROUPS + g) * D_ATTN:(j * N_GROUPS + g + 1) * D_ATTN] for j in range(3)]
        q, k, v, kn, vn = _qkv_prompt(hp, jnp.concatenate(cols, axis=1), dil)
        bias = tab[_rel_bucket(jnp.maximum(dist, 0) * dil)].transpose(2, 0, 1)
        o, lse = _attn_prompt(q, k, v, bias, B, dil, steps)
        os_.append(o.reshape(B * S, D_ATTN))
        ls_.append(lse.reshape(B * S, D_ATTN))
        keep = min(window, S)
        kp.append(kn[:, S - keep:].reshape(B, keep, HEADS, HEAD_DIM))
        vp.append(vn[:, S - keep:].reshape(B, keep, HEADS, HEAD_DIM))
        sb = tab[_rel_bucket(jnp.arange(steps + 1, dtype=I32) * dil)]
        sb = jnp.pad(sb, ((0, 0), (0, LANES - HEADS)))
        bias_self.append(sb[0:1])
        bias_keys.append(sb[1:][::-1])
        ck, cv = caches[g]
        wb = ck.shape[1]
        assert wb == steps * dil
        cache_views.append((ck.reshape(DB, steps, dil * D_ATTN), cv.reshape(DB, steps, dil * D_ATTN)))
    h_p = _merge(os_, ls_, hp.reshape(B * S, D), w_o_b, g1, b1).reshape(B, S, D)

    qkv_s = _matmul(hs, w_qkv_b, 512)
    h_s = _attn_sample(qkv_s, cache_views, jnp.stack(bias_keys), jnp.stack(bias_self), hs, w_o_b, g1, b1)
    ks = [qkv_s[:, (N_GROUPS + g) * D_ATTN:(N_GROUPS + g + 1) * D_ATTN].reshape(DB, 1, HEADS, HEAD_DIM)
          for g in range(N_GROUPS)]
    vs = [qkv_s[:, (2 * N_GROUPS + g) * D_ATTN:(2 * N_GROUPS + g + 1) * D_ATTN].reshape(DB, 1, HEADS, HEAD_DIM)
          for g in range(N_GROUPS)]
    return h_p, h_s, kp, vp, ks, vs


def _pool_conv_layer(hp, hs, state_pool, state_conv, w_in, pool_w, pool_scale, conv_w, conv_b, ln_g, ln_b, w_out,
                     g1, b1):
    params = (w_in.astype(BF16), pool_w.astype(BF16), pool_scale.reshape(1, D_POOL), conv_w,
              conv_b.reshape(1, D_CONV), ln_g.reshape(1, D_CONV), ln_b.reshape(1, D_CONV), w_out.astype(BF16), g1, b1)
    h_p, pst, cst = _l0_prompt(hp, *params)
    h_s, u_s, glu_s = _l0_sample(hs, state_pool, state_conv, *params)
    pool_p = pst[:, POOL_HALO - POOL_STATE:]
    conv_p = cst[:, CONV_HALO - CONV_STATE:]
    pool_s = jnp.concatenate([state_pool[:, 1:], u_s[:, None, :]], axis=1)
    conv_s = jnp.concatenate([state_conv[:, 1:], glu_s[:, None, :]], axis=1)
    return h_p, h_s, pool_p, conv_p, pool_s, conv_s


def kernel(x_prompt, x_sample, state_pool, state_conv, cache_k1, cache_v1, cache_k2, cache_v2, cache_k3, cache_v3,
           w_in_ab, pool_w, pool_scale, conv_w, conv_b, conv_ln_g, conv_ln_b, w_out_ab, w_qkv, w_o, rel_bias,
           ln1_g, ln1_b, ln2_g, ln2_b, w_router, router_bias, we_gate, we_up, we_down, ws_gate, ws_up, ws_down):
    B, S, D = x_prompt.shape
    DB, T, _ = x_sample.shape
    assert T == 1 and D == D_MODEL
    hp = x_prompt
    hs = x_sample.reshape(DB, D)
    caches_k = (cache_k1, cache_k2, cache_k3)
    caches_v = (cache_v1, cache_v2, cache_v3)
    pool_p, conv_p, pool_s, conv_s = [], [], [], []
    kp = [[] for _ in range(N_GROUPS)]
    vp = [[] for _ in range(N_GROUPS)]
    ksm = [[] for _ in range(N_GROUPS)]
    vsm = [[] for _ in range(N_GROUPS)]
    for layer in range(DEPTH):
        j = layer // 2
        g1 = ln1_g[layer].reshape(1, D)
        b1 = ln1_b[layer].reshape(1, D)
        if layer % 2 == 0:
            hp, hs, pp, cp, ps, cs = _pool_conv_layer(
                hp, hs, state_pool[j], state_conv[j], w_in_ab[j], pool_w[j], pool_scale[j], conv_w[j], conv_b[j],
                conv_ln_g[j], conv_ln_b[j], w_out_ab[j], g1, b1)
            pool_p.append(pp)
            conv_p.append(cp)
            pool_s.append(ps)
            conv_s.append(cs)
        else:
            caches = [(caches_k[g][j], caches_v[g][j]) for g in range(N_GROUPS)]
            hp, hs, nkp, nvp, nks, nvs = _dilated_layer(hp, hs, caches, w_qkv[j], w_o[j], rel_bias, g1, b1)
            for g in range(N_GROUPS):
                kp[g].append(nkp[g])
                vp[g].append(nvp[g])
                ksm[g].append(nks[g])
                vsm[g].append(nvs[g])
        tok = jnp.concatenate([hp.reshape(B * S, D), hs], axis=0)
        tok = _moe_layer(tok, w_router[layer], router_bias[layer], we_gate[layer], we_up[layer], we_down[layer],
                         ws_gate[layer], ws_up[layer], ws_down[layer], ln2_g[layer], ln2_b[layer])
        hp = tok[:B * S].reshape(B, S, D)
        hs = tok[B * S:]
    return (hp, hs.reshape(DB, T, D),
            jnp.stack(pool_p), jnp.stack(conv_p),
            jnp.stack(kp[0]), jnp.stack(vp[0]), jnp.stack(kp[1]), jnp.stack(vp[1]), jnp.stack(kp[2]), jnp.stack(vp[2]),
            jnp.stack(pool_s), jnp.stack(conv_s),
            jnp.stack(ksm[0]), jnp.stack(vsm[0]), jnp.stack(ksm[1]), jnp.stack(vsm[1]),
            jnp.stack(ksm[2]), jnp.stack(vsm[2]))
```

```python
import functools
import math

import jax
import jax.numpy as jnp
from jax import lax
from jax.experimental import pallas as pl
from jax.experimental.pallas import tpu as pltpu

F32 = jnp.float32
BF16 = jnp.bfloat16
I32 = jnp.int32

D_MODEL = 1024
D_POOL = 512
D_CONV = 512
POOL_WINDOWS = (2, 4, 8, 16)
POOL_GROUP = 128
POOL_STATE = 15
CONV_WIDTH = 31
CONV_STATE = 30
DIL_CONFIGS = ((128, 1), (512, 4), (2048, 16))
N_GROUPS = 3
HEADS = 8
HEAD_DIM = 64
D_ATTN = HEADS * HEAD_DIM
QUERY_BLOCK = 128
N_BUCKETS = 32
MAX_DISTANCE = 2048
N_EXPERTS = 256
TOP_K = 8
N_EXPERT_GROUPS = 8
EXPERTS_PER_GROUP = N_EXPERTS // N_EXPERT_GROUPS
TOPK_GROUPS = 4
D_EXPERT = 256
ROUTED_SCALE = 2.5
DEPTH = 2
DEEPNORM_ALPHA = (2.0 * DEPTH) ** 0.25
LN_EPS = 1e-5
NEG_INF = -1e30

LANES = 128
SUBLANES = 8
VMEM_LIMIT_BYTES = 56 * 1024 * 1024

POOL_HALO = 16
CONV_HALO = 32
L0_TIME_TILE = 256
ROUTER_TILE = 384
EXPERT_BLOCK = 256
MERGE_TILE = 512


def _cparams(sem):
    return pltpu.CompilerParams(dimension_semantics=sem, vmem_limit_bytes=VMEM_LIMIT_BYTES)


def _layer_norm(x, g, b):
    mu = jnp.mean(x, axis=-1, keepdims=True)
    xc = x - mu
    var = jnp.mean(xc * xc, axis=-1, keepdims=True)
    return xc * lax.rsqrt(var + LN_EPS) * g + b


def _const_spec(shape):
    nd = len(shape)
    return pl.BlockSpec(shape, lambda *_: (0,) * nd)


def _l0_prompt_kernel(x_ref, win_ref, pw_ref, ps_ref, cw_ref, cb_ref, cg_ref, cbeta_ref, wout_ref,
                      g1_ref, b1_ref, h_ref, pstate_ref, cstate_ref, ue_ref, ge_ref):
    tt = x_ref.shape[1]
    t = pl.program_id(1)

    @pl.when(t == 0)
    def _():
        ue_ref[0:POOL_HALO, :] = jnp.zeros((POOL_HALO, D_POOL), F32)
        ge_ref[0:CONV_HALO, :] = jnp.zeros((CONV_HALO, D_CONV), F32)

    @pl.when(t > 0)
    def _():
        ue_ref[0:POOL_HALO, :] = ue_ref[tt:tt + POOL_HALO, :]
        ge_ref[0:CONV_HALO, :] = ge_ref[tt:tt + CONV_HALO, :]

    x = x_ref[0]
    proj = jnp.dot(x.astype(BF16), win_ref[...], preferred_element_type=F32)
    u = proj[:, :D_POOL]
    a = proj[:, D_POOL:D_POOL + D_CONV]
    gate = proj[:, D_POOL + D_CONV:]
    glu = a * jax.nn.sigmoid(gate)
    ue_ref[POOL_HALO:POOL_HALO + tt, :] = u
    ge_ref[CONV_HALO:CONV_HALO + tt, :] = glu

    tg = t * tt + lax.broadcasted_iota(I32, (tt, 1), 0)
    parts = []
    for g, w in enumerate(POOL_WINDOWS):
        c0 = g * POOL_GROUP
        ug = u[:, c0:c0 + POOL_GROUP]
        s = ug
        for j in range(1, w):
            s = s + ue_ref[POOL_HALO - j:POOL_HALO - j + tt, c0:c0 + POOL_GROUP]
        cnt = jnp.minimum(tg + 1, w).astype(F32)
        pooled = s / cnt - ug
        parts.append(jnp.dot(pooled.astype(BF16), pw_ref[g], preferred_element_type=F32))
    yp = jnp.concatenate(parts, axis=1) * ps_ref[...]

    acc = glu * cw_ref[CONV_STATE:CONV_STATE + 1, :]
    off = CONV_HALO - CONV_STATE
    for j in range(CONV_STATE):
        acc = acc + ge_ref[off + j:off + j + tt, :] * cw_ref[j:j + 1, :]
    yn = _layer_norm(acc + cb_ref[...], cg_ref[...], cbeta_ref[...])
    yc = yn * jax.nn.sigmoid(yn)

    cat = jnp.concatenate([yp, yc], axis=1).astype(BF16)
    m = jnp.dot(cat, wout_ref[...], preferred_element_type=F32)
    h_ref[0] = _layer_norm(DEEPNORM_ALPHA * x + m, g1_ref[...], b1_ref[...])
    pstate_ref[0] = ue_ref[tt:tt + POOL_HALO, :]
    cstate_ref[0] = ge_ref[tt:tt + CONV_HALO, :]


def _l0_prompt(x, win, pw, ps, cw, cb, cg, cbeta, wout, g1, b1):
    B, S, D = x.shape
    tt = min(L0_TIME_TILE, S)
    assert S % tt == 0 and tt >= CONV_HALO
    return pl.pallas_call(
        _l0_prompt_kernel,
        grid=(B, S // tt),
        in_specs=[
            pl.BlockSpec((1, tt, D), lambda b, t: (b, t, 0)),
            _const_spec(win.shape), _const_spec(pw.shape), _const_spec(ps.shape), _const_spec(cw.shape),
            _const_spec(cb.shape), _const_spec(cg.shape), _const_spec(cbeta.shape), _const_spec(wout.shape),
            _const_spec(g1.shape), _const_spec(b1.shape),
        ],
        out_specs=[
            pl.BlockSpec((1, tt, D), lambda b, t: (b, t, 0)),
            pl.BlockSpec((1, POOL_HALO, D_POOL), lambda b, t: (b, 0, 0)),
            pl.BlockSpec((1, CONV_HALO, D_CONV), lambda b, t: (b, 0, 0)),
        ],
        out_shape=[
            jax.ShapeDtypeStruct((B, S, D), F32),
            jax.ShapeDtypeStruct((B, POOL_HALO, D_POOL), F32),
            jax.ShapeDtypeStruct((B, CONV_HALO, D_CONV), F32),
        ],
        scratch_shapes=[pltpu.VMEM((tt + POOL_HALO, D_POOL), F32), pltpu.VMEM((tt + CONV_HALO, D_CONV), F32)],
        compiler_params=_cparams(("arbitrary", "arbitrary")),
        name="l0_prompt",
    )(x, win, pw, ps, cw, cb, cg, cbeta, wout, g1, b1)


def _l0_sample_kernel(x_ref, sp_ref, sc_ref, win_ref, pw_ref, ps_ref, cw_ref, cb_ref, cg_ref, cbeta_ref,
                      wout_ref, g1_ref, b1_ref, h_ref, u_ref, glu_ref):
    x = x_ref[...]
    proj = jnp.dot(x.astype(BF16), win_ref[...], preferred_element_type=F32)
    u = proj[:, :D_POOL]
    a = proj[:, D_POOL:D_POOL + D_CONV]
    gate = proj[:, D_POOL + D_CONV:]
    glu = a * jax.nn.sigmoid(gate)
    u_ref[...] = u
    glu_ref[...] = glu

    parts = []
    for g, w in enumerate(POOL_WINDOWS):
        c0 = g * POOL_GROUP
        ug = u[:, c0:c0 + POOL_GROUP]
        past = sp_ref[:, POOL_STATE - (w - 1):POOL_STATE, c0:c0 + POOL_GROUP]
        s = ug + jnp.sum(past, axis=1)
        pooled = s / float(w) - ug
        parts.append(jnp.dot(pooled.astype(BF16), pw_ref[g], preferred_element_type=F32))
    yp = jnp.concatenate(parts, axis=1) * ps_ref[...]

    acc = glu * cw_ref[CONV_STATE:CONV_STATE + 1, :]
    acc = acc + jnp.sum(sc_ref[...] * cw_ref[0:CONV_STATE, :][None, :, :], axis=1)
    yn = _layer_norm(acc + cb_ref[...], cg_ref[...], cbeta_ref[...])
    yc = yn * jax.nn.sigmoid(yn)

    cat = jnp.concatenate([yp, yc], axis=1).astype(BF16)
    m = jnp.dot(cat, wout_ref[...], preferred_element_type=F32)
    h_ref[...] = _layer_norm(DEEPNORM_ALPHA * x + m, g1_ref[...], b1_ref[...])


def _l0_sample(x, sp, sc, win, pw, ps, cw, cb, cg, cbeta, wout, g1, b1):
    DB, D = x.shape
    args = (x, sp, sc, win, pw, ps, cw, cb, cg, cbeta, wout, g1, b1)
    return pl.pallas_call(
        _l0_sample_kernel,
        grid=(1,),
        in_specs=[_const_spec(a.shape) for a in args],
        out_specs=[_const_spec((DB, D)), _const_spec((DB, D_POOL)), _const_spec((DB, D_CONV))],
        out_shape=[
            jax.ShapeDtypeStruct((DB, D), F32),
            jax.ShapeDtypeStruct((DB, D_POOL), F32),
            jax.ShapeDtypeStruct((DB, D_CONV), F32),
        ],
        compiler_params=_cparams(("arbitrary",)),
        name="l0_sample",
    )(*args)


def _first_index_of_max(x, iota, size):
    m = jnp.max(x, axis=0, keepdims=True)
    f = jnp.min(jnp.where(x == m, iota, size), axis=0, keepdims=True)
    return m, f


def _router_kernel(h_ref, wrt_ref, bias_ref, tri_ref, idx_ref, w_ref, rank_ref, cnt_ref, run_ref):
    tm = h_ref.shape[0]
    E = N_EXPERTS
    PG = EXPERTS_PER_GROUP

    @pl.when(pl.program_id(0) == 0)
    def _():
        run_ref[...] = jnp.zeros_like(run_ref)

    logits = lax.dot_general(wrt_ref[...], h_ref[...].astype(BF16), (((1,), (1,)), ((), ())),
                             preferred_element_type=F32)
    scores = jax.nn.sigmoid(logits)
    sel = scores + bias_ref[...]

    io_g = lax.broadcasted_iota(I32, (PG, tm), 0)
    rows = []
    for g in range(N_EXPERT_GROUPS):
        blk = sel[g * PG:(g + 1) * PG, :]
        m1, f1 = _first_index_of_max(blk, io_g, PG)
        m2 = jnp.max(jnp.where(io_g == f1, -jnp.inf, blk), axis=0, keepdims=True)
        rows.append(m1 + m2)
    gs = jnp.concatenate(rows, axis=0)

    io_n = lax.broadcasted_iota(I32, (N_EXPERT_GROUPS, tm), 0)
    gsel = jnp.zeros((N_EXPERT_GROUPS, tm), F32)
    cur = gs
    for _ in range(TOPK_GROUPS):
        _, f = _first_index_of_max(cur, io_n, N_EXPERT_GROUPS)
        hit = io_n == f
        gsel = jnp.where(hit, 1.0, gsel)
        cur = jnp.where(hit, -jnp.inf, cur)
    masked = jnp.concatenate(
        [jnp.where(gsel[g:g + 1, :] > 0.5, sel[g * PG:(g + 1) * PG, :], -jnp.inf) for g in range(N_EXPERT_GROUPS)],
        axis=0)

    io_e = lax.broadcasted_iota(I32, (E, tm), 0)
    onehot = jnp.zeros((E, tm), F32)
    idx_rows, sc_rows = [], []
    cur = masked
    for _ in range(TOP_K):
        _, f = _first_index_of_max(cur, io_e, E)
        hit = io_e == f
        idx_rows.append(f)
        sc_rows.append(jnp.sum(jnp.where(hit, scores, 0.0), axis=0, keepdims=True))
        onehot = jnp.where(hit, 1.0, onehot)
        cur = jnp.where(hit, -jnp.inf, cur)
    sc = jnp.concatenate(sc_rows, axis=0)
    idx_ref[0] = jnp.concatenate(idx_rows, axis=0)
    w_ref[...] = sc / jnp.sum(sc, axis=0, keepdims=True) * ROUTED_SCALE

    before = jnp.dot(onehot.astype(BF16), tri_ref[...], preferred_element_type=F32) + run_ref[...]
    rank_rows = [jnp.sum(jnp.where(io_e == f, before, 0.0), axis=0, keepdims=True) for f in idx_rows]
    rank_ref[0] = jnp.concatenate(rank_rows, axis=0).astype(I32)
    run_ref[...] = run_ref[...] + jnp.sum(onehot, axis=1, keepdims=True)
    cnt_ref[...] = run_ref[...]


def _router(tok, wrt, bias):
    N, D = tok.shape
    tm = ROUTER_TILE
    assert N % tm == 0
    tri = (jnp.arange(tm)[:, None] < jnp.arange(tm)[None, :]).astype(BF16)
    return pl.pallas_call(
        _router_kernel,
        grid=(N // tm,),
        in_specs=[
            pl.BlockSpec((tm, D), lambda i: (i, 0)),
            _const_spec(wrt.shape), _const_spec(bias.shape), _const_spec(tri.shape),
        ],
        out_specs=[
            pl.BlockSpec((1, TOP_K, tm), lambda i: (i, 0, 0)),
            pl.BlockSpec((TOP_K, tm), lambda i: (0, i)),
            pl.BlockSpec((1, TOP_K, tm), lambda i: (i, 0, 0)),
            _const_spec((N_EXPERTS, 1)),
        ],
        out_shape=[
            jax.ShapeDtypeStruct((N // tm, TOP_K, tm), I32),
            jax.ShapeDtypeStruct((TOP_K, N), F32),
            jax.ShapeDtypeStruct((N // tm, TOP_K, tm), I32),
            jax.ShapeDtypeStruct((N_EXPERTS, 1), F32),
        ],
        scratch_shapes=[pltpu.VMEM((N_EXPERTS, 1), F32)],
        compiler_params=_cparams(("arbitrary",)),
        name="moe_router",
    )(tok, wrt, bias, tri)


def _load_slot_tables(i, idx_hbm, rank_hbm, idx_s, rank_s, sem):
    copies = [pltpu.make_async_copy(idx_hbm.at[i], idx_s, sem), pltpu.make_async_copy(rank_hbm.at[i], rank_s, sem)]
    for c in copies:
        c.start()
    for c in copies:
        c.wait()


def _dispatch_kernel(idx_hbm, rank_hbm, pstart_ref, pend_ref, cnt_ref, tok_ref, xs_hbm, idx_s, rank_s, zero_ref,
                     sem_idx, sem_rows):
    i = pl.program_id(0)
    tm = tok_ref.shape[0]
    blk = zero_ref.shape[0]

    @pl.when(i == 0)
    def _():
        zero_ref[...] = jnp.zeros_like(zero_ref)

        def zero_copy(e):
            last_block = pl.multiple_of(pend_ref[e] - blk, blk)
            return pltpu.make_async_copy(zero_ref, xs_hbm.at[pl.ds(last_block, blk)], sem_rows)

        def start(e, c):
            @pl.when(cnt_ref[e] > 0)
            def _():
                zero_copy(e).start()
            return c

        def wait(e, c):
            @pl.when(cnt_ref[e] > 0)
            def _():
                zero_copy(e).wait()
            return c

        lax.fori_loop(0, N_EXPERTS, start, 0)
        lax.fori_loop(0, N_EXPERTS, wait, 0)

    _load_slot_tables(i, idx_hbm, rank_hbm, idx_s, rank_s, sem_idx)

    def row_copy(t, k):
        slot = pstart_ref[idx_s[k, t]] + rank_s[k, t]
        return pltpu.make_async_copy(tok_ref.at[t], xs_hbm.at[slot], sem_rows)

    def start(t, c):
        for k in range(TOP_K):
            row_copy(t, k).start()
        return c

    def wait(t, c):
        for k in range(TOP_K):
            pltpu.make_async_copy(tok_ref.at[0], xs_hbm.at[0], sem_rows).wait()
        return c

    lax.fori_loop(0, tm, start, 0)
    lax.fori_loop(0, tm, wait, 0)


def _dispatch(tok3, idx3, rank3, pstart, pend, counts, n_slots):
    N = tok3.shape[0]
    tm = idx3.shape[2]
    assert N % tm == 0 and idx3.shape == rank3.shape == (N // tm, TOP_K, tm)
    smem = pl.BlockSpec(memory_space=pltpu.SMEM)
    return pl.pallas_call(
        _dispatch_kernel,
        grid=(N // tm,),
        in_specs=[
            pl.BlockSpec(memory_space=pl.ANY), pl.BlockSpec(memory_space=pl.ANY), smem, smem, smem,
            pl.BlockSpec((tm, SUBLANES, LANES), lambda i: (i, 0, 0)),
        ],
        out_specs=pl.BlockSpec(memory_space=pl.ANY),
        scratch_shapes=[
            pltpu.SMEM((TOP_K, tm), I32),
            pltpu.SMEM((TOP_K, tm), I32),
            pltpu.VMEM((EXPERT_BLOCK, SUBLANES, LANES), F32),
            pltpu.SemaphoreType.DMA,
            pltpu.SemaphoreType.DMA,
        ],
        out_shape=jax.ShapeDtypeStruct((n_slots, SUBLANES, LANES), F32),
        compiler_params=_cparams(("arbitrary",)),
        name="moe_dispatch",
    )(idx3, rank3, pstart, pend, counts, tok3)


def _ffn_kernel(be_ref, nu_ref, xs_ref, wg_ref, wu_ref, wd_ref, ys_ref, wgu_s, wd_s):
    b = pl.program_id(0)
    prev = be_ref[jnp.maximum(b - 1, 0)]
    new_expert = jnp.logical_or(b == 0, be_ref[b] != prev)

    @pl.when(jnp.logical_and(new_expert, b < nu_ref[0]))
    def _():
        wgu_s[:, :D_EXPERT] = wg_ref[0, 0].astype(BF16)
        wgu_s[:, D_EXPERT:] = wu_ref[0, 0].astype(BF16)
        wd_s[...] = wd_ref[0, 0].astype(BF16)

    @pl.when(b < nu_ref[0])
    def _():
        blk = xs_ref.shape[0] // SUBLANES
        x = jnp.concatenate([xs_ref[pl.ds(c, blk, stride=SUBLANES), :] for c in range(SUBLANES)], axis=1)
        gu = jnp.dot(x.astype(BF16), wgu_s[...], preferred_element_type=F32)
        gt = gu[:, :D_EXPERT]
        hid = gt * jax.nn.sigmoid(gt) * gu[:, D_EXPERT:]
        y = jnp.dot(hid.astype(BF16), wd_s[...], preferred_element_type=F32)
        for c in range(SUBLANES):
            ys_ref[pl.ds(c, blk, stride=SUBLANES), :] = y[:, c * LANES:(c + 1) * LANES]


def _expert_ffn(xs, block_e, n_used, layer, wg, wu, wd):
    n_slots = xs.shape[0]
    D = SUBLANES * LANES
    blk = EXPERT_BLOCK
    n_blocks = n_slots // blk

    def row_map(b, be, nu):
        return (jnp.minimum(b, nu[0] - 1), 0)

    def w_map(b, be, nu):
        return (layer, be[b], 0, 0)

    return pl.pallas_call(
        _ffn_kernel,
        grid_spec=pltpu.PrefetchScalarGridSpec(
            num_scalar_prefetch=2,
            grid=(n_blocks,),
            in_specs=[
                pl.BlockSpec((blk * SUBLANES, LANES), row_map),
                pl.BlockSpec((1, 1, D, D_EXPERT), w_map),
                pl.BlockSpec((1, 1, D, D_EXPERT), w_map),
                pl.BlockSpec((1, 1, D_EXPERT, D), w_map),
            ],
            out_specs=pl.BlockSpec((blk * SUBLANES, LANES), row_map),
            scratch_shapes=[pltpu.VMEM((D, 2 * D_EXPERT), BF16), pltpu.VMEM((D_EXPERT, D), BF16)],
        ),
        out_shape=jax.ShapeDtypeStruct((n_slots * SUBLANES, LANES), F32),
        compiler_params=_cparams(("arbitrary",)),
        name="moe_ffn",
    )(block_e, n_used, xs.reshape(n_slots * SUBLANES, LANES), wg, wu, wd).reshape(n_slots, SUBLANES, LANES)


def _combine_kernel(idx_hbm, rank_hbm, pstart_ref, ys_hbm, tok_ref, w_ref, wsg_ref, wsu_ref, wsd_ref, g_ref, b_ref,
                    out_ref, idx_s, rank_s, buf_ref, sem_idx, sem_rows):
    i = pl.program_id(0)
    tm = tok_ref.shape[0]

    _load_slot_tables(i, idx_hbm, rank_hbm, idx_s, rank_s, sem_idx)

    def row_copy(t, k):
        slot = pstart_ref[idx_s[k, t]] + rank_s[k, t]
        row = pl.multiple_of(t * SUBLANES, SUBLANES)
        return pltpu.make_async_copy(ys_hbm.at[slot], buf_ref.at[k, pl.ds(row, SUBLANES)], sem_rows)

    def start(t, c):
        for k in range(TOP_K):
            row_copy(t, k).start()
        return c

    def wait(t, c):
        for k in range(TOP_K):
            pltpu.make_async_copy(ys_hbm.at[0], buf_ref.at[0, pl.ds(0, SUBLANES)], sem_rows).wait()
        return c

    lax.fori_loop(0, tm, start, 0)

    h = tok_ref[...]
    hb = h.astype(BF16)
    gt = jnp.dot(hb, wsg_ref[...], preferred_element_type=F32)
    up = jnp.dot(hb, wsu_ref[...], preferred_element_type=F32)
    hid = gt * jax.nn.sigmoid(gt) * up
    f = jnp.dot(hid.astype(BF16), wsd_ref[...], preferred_element_type=F32)

    lax.fori_loop(0, tm, wait, 0)
    w = w_ref[...]
    wb = [jnp.broadcast_to(w[:, k:k + 1], (tm, LANES)) for k in range(TOP_K)]
    chunks = []
    for c in range(SUBLANES):
        acc = f[:, c * LANES:(c + 1) * LANES]
        for k in range(TOP_K):
            acc = acc + buf_ref[k, pl.ds(c, tm, stride=SUBLANES), :] * wb[k]
        chunks.append(acc)
    f = jnp.concatenate(chunks, axis=1)
    out_ref[...] = _layer_norm(DEEPNORM_ALPHA * h + f, g_ref[...], b_ref[...])


def _combine(tok, ys, idx3, rank3, pstart, w_tok, wsg, wsu, wsd, g, b):
    N, D = tok.shape
    tm = idx3.shape[2]
    assert N % tm == 0 and idx3.shape == rank3.shape == (N // tm, TOP_K, tm)
    return pl.pallas_call(
        _combine_kernel,
        grid=(N // tm,),
        in_specs=[
            pl.BlockSpec(memory_space=pl.ANY), pl.BlockSpec(memory_space=pl.ANY),
            pl.BlockSpec(memory_space=pltpu.SMEM),
            pl.BlockSpec(memory_space=pl.ANY),
            pl.BlockSpec((tm, D), lambda i: (i, 0)),
            pl.BlockSpec((tm, TOP_K), lambda i: (i, 0)),
            _const_spec(wsg.shape), _const_spec(wsu.shape), _const_spec(wsd.shape),
            _const_spec(g.shape), _const_spec(b.shape),
        ],
        out_specs=pl.BlockSpec((tm, D), lambda i: (i, 0)),
        scratch_shapes=[
            pltpu.SMEM((TOP_K, tm), I32),
            pltpu.SMEM((TOP_K, tm), I32),
            pltpu.VMEM((TOP_K, tm * SUBLANES, LANES), F32),
            pltpu.SemaphoreType.DMA,
            pltpu.SemaphoreType.DMA,
        ],
        out_shape=jax.ShapeDtypeStruct((N, D), F32),
        compiler_params=_cparams(("arbitrary",)),
        name="moe_combine",
    )(idx3, rank3, pstart, ys, tok, w_tok, wsg, wsu, wsd, g, b)


def _moe_layer(tok, layer, w_router, router_bias, we_gate, we_up, we_down, ws_gate, ws_up, ws_down, ln_g, ln_b):
    N, D = tok.shape
    blk = EXPERT_BLOCK
    idx3, w_t, rank3, cnt = _router(tok, w_router.T.astype(BF16), router_bias.reshape(N_EXPERTS, 1))

    counts = cnt[:, 0].astype(I32)
    pcounts = (counts + blk - 1) // blk * blk
    pend = jnp.cumsum(pcounts).astype(I32)
    pstart = pend - pcounts
    n_blocks = N * TOP_K // blk + N_EXPERTS
    block_start = jnp.arange(n_blocks, dtype=I32) * blk
    block_e = jnp.minimum(jnp.sum(pend[None, :] <= block_start[:, None], axis=1), N_EXPERTS - 1).astype(I32)
    n_used = (pend[-1] // blk).reshape(1)

    xs = _dispatch(tok.reshape(N, SUBLANES, LANES), idx3, rank3, pstart, pend, counts, n_blocks * blk)
    ys = _expert_ffn(xs, block_e, n_used, layer, we_gate, we_up, we_down)
    return _combine(tok, ys, idx3, rank3, pstart, w_t.T,
                    ws_gate.astype(BF16), ws_up.astype(BF16), ws_down.astype(BF16),
                    ln_g.reshape(1, D), ln_b.reshape(1, D))


def _rel_bucket(dist):
    exact = N_BUCKETS // 2
    df = jnp.maximum(dist, 1).astype(F32)
    large = exact + (jnp.log(df / exact) / math.log(MAX_DISTANCE / exact) * (N_BUCKETS - exact)).astype(I32)
    large = jnp.minimum(large, N_BUCKETS - 1)
    return jnp.where(dist < exact, dist, large)


def _qkv_kernel(x_ref, w_ref, q_ref, k_ref, v_ref, kn_ref, vn_ref):
    y = jnp.dot(x_ref[0].astype(BF16), w_ref[...], preferred_element_type=F32)
    q_ref[0] = y[:, :D_ATTN].astype(BF16)
    k = y[:, D_ATTN:2 * D_ATTN]
    v = y[:, 2 * D_ATTN:]
    k_ref[0] = k.astype(BF16)
    v_ref[0] = v.astype(BF16)
    kn_ref[0] = k
    vn_ref[0] = v


def _qkv_prompt(h, w_g, dil):
    B, S, D = h.shape
    L = S // dil
    tl = min(L, 512)
    assert S % dil == 0 and L % tl == 0
    hv = h.reshape(B, L, dil * D)
    ph_spec = pl.BlockSpec((1, tl, D_ATTN), lambda b, r, l: (b * dil + r, l, 0))
    nat_spec = pl.BlockSpec((1, tl, D_ATTN), lambda b, r, l: (b, l, r))
    q, k, v, kn, vn = pl.pallas_call(
        _qkv_kernel,
        grid=(B, dil, L // tl),
        in_specs=[pl.BlockSpec((1, tl, D), lambda b, r, l: (b, l, r)), _const_spec(w_g.shape)],
        out_specs=[ph_spec, ph_spec, ph_spec, nat_spec, nat_spec],
        out_shape=[jax.ShapeDtypeStruct((B * dil, L, D_ATTN), BF16)] * 3
        + [jax.ShapeDtypeStruct((B, L, dil * D_ATTN), F32)] * 2,
        compiler_params=_cparams(("arbitrary", "arbitrary", "arbitrary")),
        name="qkv_prompt",
    )(hv, w_g)
    return q, k, v, kn.reshape(B, S, D_ATTN), vn.reshape(B, S, D_ATTN)


def _attn_prompt_kernel(q_ref, kp_ref, kc_ref, vp_ref, vc_ref, bias_ref, o_ref, lse_ref, *, steps):
    n = pl.program_id(1)
    qb = QUERY_BLOCK
    qi = lax.broadcasted_iota(I32, (qb, 2 * qb), 0)
    kj = lax.broadcasted_iota(I32, (qb, 2 * qb), 1)
    dist = qi + qb - kj
    valid = (dist >= 0) & (dist <= steps) & ((n > 0) | (kj >= qb))
    low = lax.broadcasted_iota(I32, (1, 2 * HEAD_DIM), 1) < HEAD_DIM
    scale = HEAD_DIM ** -0.5
    for p in range(HEADS // 2):
        c0 = p * 2 * HEAD_DIM
        q2 = q_ref[0, :, c0:c0 + 2 * HEAD_DIM]
        k2 = jnp.concatenate([kp_ref[0, :, c0:c0 + 2 * HEAD_DIM], kc_ref[0, :, c0:c0 + 2 * HEAD_DIM]], axis=0)
        v2 = jnp.concatenate([vp_ref[0, :, c0:c0 + 2 * HEAD_DIM], vc_ref[0, :, c0:c0 + 2 * HEAD_DIM]], axis=0)
        outs, lses = [], []
        for half in range(2):
            keep = low if half == 0 else jnp.logical_not(low)
            qh = jnp.where(keep, q2, jnp.zeros_like(q2))
            s = lax.dot_general(qh, k2, (((1,), (1,)), ((), ())), preferred_element_type=F32)
            s = s * scale + bias_ref[2 * p + half]
            s = jnp.where(valid, s, NEG_INF)
            m = jnp.max(s, axis=-1, keepdims=True)
            e = jnp.exp(s - m)
            l = jnp.sum(e, axis=-1, keepdims=True)
            prob = e / l
            outs.append(jnp.dot(prob.astype(BF16), v2, preferred_element_type=F32))
            lses.append(m + jnp.log(l))
        o_ref[0, :, c0:c0 + 2 * HEAD_DIM] = jnp.where(low, outs[0], outs[1])
        lse_ref[0, :, c0:c0 + 2 * HEAD_DIM] = jnp.where(low, lses[0], lses[1])


def _attn_prompt(q, k, v, bias, B, dil, steps):
    BD, L, _ = q.shape
    qb = QUERY_BLOCK
    assert L % qb == 0
    nb = L // qb
    cur = pl.BlockSpec((1, qb, D_ATTN), lambda bd, n: (bd, n, 0))
    prev = pl.BlockSpec((1, qb, D_ATTN), lambda bd, n: (bd, jnp.maximum(n - 1, 0), 0))
    nat = pl.BlockSpec((1, qb, D_ATTN), lambda bd, n: (bd // dil, n, bd % dil))
    o, lse = pl.pallas_call(
        functools.partial(_attn_prompt_kernel, steps=steps),
        grid=(BD, nb),
        in_specs=[cur, prev, cur, prev, cur, _const_spec(bias.shape)],
        out_specs=[nat, nat],
        out_shape=[jax.ShapeDtypeStruct((B, L, dil * D_ATTN), F32)] * 2,
        compiler_params=_cparams(("arbitrary", "arbitrary")),
        name="attn_prompt",
    )(q, k, k, v, v, bias)
    return o.reshape(B, L * dil, D_ATTN), lse.reshape(B, L * dil, D_ATTN)


def _merge_kernel(o1, o2, o3, l1, l2, l3, h_ref, wo_ref, g_ref, b_ref, out_ref):
    a1, a2, a3 = l1[...], l2[...], l3[...]
    m = jnp.maximum(jnp.maximum(a1, a2), a3)
    e1, e2, e3 = jnp.exp(a1 - m), jnp.exp(a2 - m), jnp.exp(a3 - m)
    o = (e1 * o1[...] + e2 * o2[...] + e3 * o3[...]) / (e1 + e2 + e3)
    y = jnp.dot(o.astype(BF16), wo_ref[...], preferred_element_type=F32)
    out_ref[...] = _layer_norm(DEEPNORM_ALPHA * h_ref[...] + y, g_ref[...], b_ref[...])


def _merge(os_, ls_, h, wo, g, b):
    M, D = h.shape
    tm = min(MERGE_TILE, M)
    assert M % tm == 0
    a_spec = pl.BlockSpec((tm, D_ATTN), lambda i: (i, 0))
    return pl.pallas_call(
        _merge_kernel,
        grid=(M // tm,),
        in_specs=[a_spec] * 6 + [pl.BlockSpec((tm, D), lambda i: (i, 0)), _const_spec(wo.shape),
                                 _const_spec(g.shape), _const_spec(b.shape)],
        out_specs=pl.BlockSpec((tm, D), lambda i: (i, 0)),
        out_shape=jax.ShapeDtypeStruct((M, D), F32),
        compiler_params=_cparams(("arbitrary",)),
        name="attn_merge",
    )(*os_, *ls_, h, wo, g, b)


def _mm_kernel(x_ref, w_ref, o_ref):
    o_ref[...] = jnp.dot(x_ref[...].astype(BF16), w_ref[...], preferred_element_type=F32)


def _matmul(x, w, tn):
    M, K = x.shape
    _, N = w.shape
    assert N % tn == 0
    return pl.pallas_call(
        _mm_kernel,
        grid=(N // tn,),
        in_specs=[_const_spec((M, K)), pl.BlockSpec((K, tn), lambda j: (0, j))],
        out_specs=pl.BlockSpec((M, tn), lambda j: (0, j)),
        out_shape=jax.ShapeDtypeStruct((M, N), F32),
        compiler_params=_cparams(("arbitrary",)),
        name="matmul",
    )(x, w)


def _round_bf16(x):
    return x.astype(BF16).astype(F32)


def _split_dot(x, w):
    hi = x.astype(BF16)
    lo = (x - hi.astype(F32)).astype(BF16)
    return jnp.dot(hi, w, preferred_element_type=F32) + jnp.dot(lo, w, preferred_element_type=F32)


def _attn_sample_kernel(qkv_ref, k1, v1, k2, v2, k3, v3, bk_ref, b0_ref, seg_ref, segt_ref, h_ref, wo_ref,
                        g_ref, b_ref, out_ref):
    tb = qkv_ref.shape[0]
    W = QUERY_BLOCK
    seg = seg_ref[...]
    segt = segt_ref[...]
    scale = HEAD_DIM ** -0.5
    caches = ((k1, v1), (k2, v2), (k3, v3))
    outs, lses = [], []
    for g in range(N_GROUPS):
        q = _round_bf16(qkv_ref[:, g * D_ATTN:(g + 1) * D_ATTN])
        kn = _round_bf16(qkv_ref[:, (N_GROUPS + g) * D_ATTN:(N_GROUPS + g + 1) * D_ATTN])
        vn = _round_bf16(qkv_ref[:, (2 * N_GROUPS + g) * D_ATTN:(2 * N_GROUPS + g + 1) * D_ATTN])
        kc = _round_bf16(caches[g][0][...])
        vc = _round_bf16(caches[g][1][...])
        prod = (kc * q[:, None, :]).reshape(tb * W, D_ATTN)
        s = _split_dot(prod, seg).reshape(tb, W, LANES) * scale + bk_ref[g][None, :, :]
        s0 = _split_dot(q * kn, seg) * scale + b0_ref[g]
        m = jnp.maximum(jnp.max(s, axis=1), s0)
        e = jnp.exp(s - m[:, None, :])
        e0 = jnp.exp(s0 - m)
        l = jnp.sum(e, axis=1) + e0
        prob = (e / l[:, None, :]).reshape(tb * W, LANES)
        pe = jnp.dot(prob.astype(BF16), segt, preferred_element_type=F32).reshape(tb, W, D_ATTN)
        p0 = jnp.dot((e0 / l).astype(BF16), segt, preferred_element_type=F32)
        outs.append(jnp.sum(pe * vc, axis=1) + p0 * vn)
        lses.append(m + jnp.log(l))
    mx = jnp.maximum(jnp.maximum(lses[0], lses[1]), lses[2])
    es = [jnp.exp(x - mx) for x in lses]
    tot = es[0] + es[1] + es[2]
    o = sum(_split_dot(es[g] / tot, segt) * outs[g] for g in range(N_GROUPS))
    y = jnp.dot(o.astype(BF16), wo_ref[...], preferred_element_type=F32)
    out_ref[...] = _layer_norm(DEEPNORM_ALPHA * h_ref[...] + y, g_ref[...], b_ref[...])


def _attn_sample(qkv, caches, j, bias_keys, bias_self, h, wo, g, b, tb=SUBLANES):
    DB, D = h.shape
    W = QUERY_BLOCK
    assert DB % tb == 0
    lane_head = jnp.arange(D_ATTN)[:, None] // HEAD_DIM == jnp.arange(LANES)[None, :]
    seg = lane_head.astype(BF16)
    segt = lane_head.T.astype(BF16)
    cache_spec = pl.BlockSpec((tb, W, D_ATTN), lambda i: (j * (DB // tb) + i, 0, 0))
    flat = [c for kv in caches for c in kv]
    return pl.pallas_call(
        _attn_sample_kernel,
        grid=(DB // tb,),
        in_specs=[pl.BlockSpec((tb, qkv.shape[1]), lambda i: (i, 0))] + [cache_spec] * 6
        + [_const_spec(bias_keys.shape), _const_spec(bias_self.shape), _const_spec(seg.shape),
           _const_spec(segt.shape), pl.BlockSpec((tb, D), lambda i: (i, 0)), _const_spec(wo.shape),
           _const_spec(g.shape), _const_spec(b.shape)],
        out_specs=pl.BlockSpec((tb, D), lambda i: (i, 0)),
        out_shape=jax.ShapeDtypeStruct((DB, D), F32),
        compiler_params=_cparams(("arbitrary",)),
        name="attn_sample",
    )(qkv, *flat, bias_keys, bias_self, seg, segt, h, wo, g, b)


def _bias_lookup(tab, buckets):
    onehot = (buckets[..., None] == jnp.arange(N_BUCKETS, dtype=I32)).astype(F32)
    return jnp.einsum('...n,nh->...h', onehot, tab, precision=lax.Precision.HIGHEST)


def _dilated_layer(hp, hs, caches, j, w_qkv, w_o, rel_bias, g1, b1):
    B, S, D = hp.shape
    DB = hs.shape[0]
    w_qkv_b = w_qkv.astype(BF16)
    w_o_b = w_o.astype(BF16)
    qb = QUERY_BLOCK
    qi = jnp.arange(qb, dtype=I32)[:, None]
    kj = jnp.arange(2 * qb, dtype=I32)[None, :]
    dist = qi + qb - kj
    os_, ls_, kp, vp = [], [], [], []
    bias_keys, bias_self, cache_views = [], [], []
    for g, (window, dil) in enumerate(DIL_CONFIGS):
        steps = window // dil
        assert steps == qb
        tab = rel_bias[:, g * HEADS:(g + 1) * HEADS]
        cols = [w_qkv_b[:, (j * N_GROUPS + g) * D_ATTN:(j * N_GROUPS + g + 1) * D_ATTN] for j in range(3)]
        q, k, v, kn, vn = _qkv_prompt(hp, jnp.concatenate(cols, axis=1), dil)
        bias = _bias_lookup(tab, _rel_bucket(jnp.maximum(dist, 0) * dil)).transpose(2, 0, 1)
        o, lse = _attn_prompt(q, k, v, bias, B, dil, steps)
        os_.append(o.reshape(B * S, D_ATTN))
        ls_.append(lse.reshape(B * S, D_ATTN))
        keep = min(window, S)
        kp.append(kn[:, S - keep:].reshape(B, keep, HEADS, HEAD_DIM))
        vp.append(vn[:, S - keep:].reshape(B, keep, HEADS, HEAD_DIM))
        sb = _bias_lookup(tab, _rel_bucket(jnp.arange(steps + 1, dtype=I32) * dil))
        sb = jnp.pad(sb, ((0, 0), (0, LANES - HEADS)))
        bias_self.append(sb[0:1])
        bias_keys.append(sb[1:][::-1])
        ck, cv = caches[g]
        n_layers, _, wb = ck.shape[:3]
        assert wb == steps * dil and ck.shape[1] == DB
        cache_views.append((ck.reshape(n_layers * DB, steps, dil * D_ATTN),
                            cv.reshape(n_layers * DB, steps, dil * D_ATTN)))
    h_p = _merge(os_, ls_, hp.reshape(B * S, D), w_o_b, g1, b1).reshape(B, S, D)

    qkv_s = _matmul(hs, w_qkv_b, 512)
    h_s = _attn_sample(qkv_s, cache_views, j, jnp.stack(bias_keys), jnp.stack(bias_self), hs, w_o_b, g1, b1)
    ks = [qkv_s[:, (N_GROUPS + g) * D_ATTN:(N_GROUPS + g + 1) * D_ATTN].reshape(DB, 1, HEADS, HEAD_DIM)
          for g in range(N_GROUPS)]
    vs = [qkv_s[:, (2 * N_GROUPS + g) * D_ATTN:(2 * N_GROUPS + g + 1) * D_ATTN].reshape(DB, 1, HEADS, HEAD_DIM)
          for g in range(N_GROUPS)]
    return h_p, h_s, kp, vp, ks, vs


def _pool_conv_layer(hp, hs, state_pool, state_conv, w_in, pool_w, pool_scale, conv_w, conv_b, ln_g, ln_b, w_out,
                     g1, b1):
    params = (w_in.astype(BF16), pool_w.astype(BF16), pool_scale.reshape(1, D_POOL), conv_w,
              conv_b.reshape(1, D_CONV), ln_g.reshape(1, D_CONV), ln_b.reshape(1, D_CONV), w_out.astype(BF16), g1, b1)
    h_p, pst, cst = _l0_prompt(hp, *params)
    h_s, u_s, glu_s = _l0_sample(hs, state_pool, state_conv, *params)
    pool_p = pst[:, POOL_HALO - POOL_STATE:]
    conv_p = cst[:, CONV_HALO - CONV_STATE:]
    pool_s = jnp.concatenate([state_pool[:, 1:], u_s[:, None, :]], axis=1)
    conv_s = jnp.concatenate([state_conv[:, 1:], glu_s[:, None, :]], axis=1)
    return h_p, h_s, pool_p, conv_p, pool_s, conv_s


def kernel(x_prompt, x_sample, state_pool, state_conv, cache_k1, cache_v1, cache_k2, cache_v2, cache_k3, cache_v3,
           w_in_ab, pool_w, pool_scale, conv_w, conv_b, conv_ln_g, conv_ln_b, w_out_ab, w_qkv, w_o, rel_bias,
           ln1_g, ln1_b, ln2_g, ln2_b, w_router, router_bias, we_gate, we_up, we_down, ws_gate, ws_up, ws_down):
    B, S, D = x_prompt.shape
    DB, T, _ = x_sample.shape
    assert T == 1 and D == D_MODEL
    hp = x_prompt
    hs = x_sample.reshape(DB, D)
    caches_k = (cache_k1, cache_k2, cache_k3)
    caches_v = (cache_v1, cache_v2, cache_v3)
    pool_p, conv_p, pool_s, conv_s = [], [], [], []
    kp = [[] for _ in range(N_GROUPS)]
    vp = [[] for _ in range(N_GROUPS)]
    ksm = [[] for _ in range(N_GROUPS)]
    vsm = [[] for _ in range(N_GROUPS)]
    for layer in range(DEPTH):
        j = layer // 2
        g1 = ln1_g[layer].reshape(1, D)
        b1 = ln1_b[layer].reshape(1, D)
        if layer % 2 == 0:
            hp, hs, pp, cp, ps, cs = _pool_conv_layer(
                hp, hs, state_pool[j], state_conv[j], w_in_ab[j], pool_w[j], pool_scale[j], conv_w[j], conv_b[j],
                conv_ln_g[j], conv_ln_b[j], w_out_ab[j], g1, b1)
            pool_p.append(pp)
            conv_p.append(cp)
            pool_s.append(ps)
            conv_s.append(cs)
        else:
            caches = [(caches_k[g], caches_v[g]) for g in range(N_GROUPS)]
            hp, hs, nkp, nvp, nks, nvs = _dilated_layer(hp, hs, caches, j, w_qkv[j], w_o[j], rel_bias, g1, b1)
            for g in range(N_GROUPS):
                kp[g].append(nkp[g])
                vp[g].append(nvp[g])
                ksm[g].append(nks[g])
                vsm[g].append(nvs[g])
        tok = jnp.concatenate([hp.reshape(B * S, D), hs], axis=0)
        tok = _moe_layer(tok, layer, w_router[layer], router_bias[layer], we_gate, we_up, we_down,
                         ws_gate[layer], ws_up[layer], ws_down[layer], ln2_g[layer], ln2_b[layer])
        hp = tok[:B * S].reshape(B, S, D)
        hs = tok[B * S:]
    return (hp, hs.reshape(DB, T, D),
            jnp.stack(pool_p), jnp.stack(conv_p),
            jnp.stack(kp[0]), jnp.stack(vp[0]), jnp.stack(kp[1]), jnp.stack(vp[1]), jnp.stack(kp[2]), jnp.stack(vp[2]),
            jnp.stack(pool_s), jnp.stack(conv_s),
            jnp.stack(ksm[0]), jnp.stack(vsm[0]), jnp.stack(ksm[1]), jnp.stack(vsm[1]),
            jnp.stack(ksm[2]), jnp.stack(vsm[2]))
```

```python
import functools
import math

import jax
import jax.numpy as jnp
from jax import lax
from jax.experimental import pallas as pl
from jax.experimental.pallas import tpu as pltpu

F32 = jnp.float32
BF16 = jnp.bfloat16
I32 = jnp.int32

D_MODEL = 1024
D_POOL = 512
D_CONV = 512
POOL_WINDOWS = (2, 4, 8, 16)
POOL_GROUP = 128
POOL_STATE = 15
CONV_WIDTH = 31
CONV_STATE = 30
DIL_CONFIGS = ((128, 1), (512, 4), (2048, 16))
N_GROUPS = 3
HEADS = 8
HEAD_DIM = 64
D_ATTN = HEADS * HEAD_DIM
QUERY_BLOCK = 128
N_BUCKETS = 32
MAX_DISTANCE = 2048
N_EXPERTS = 256
TOP_K = 8
N_EXPERT_GROUPS = 8
EXPERTS_PER_GROUP = N_EXPERTS // N_EXPERT_GROUPS
TOPK_GROUPS = 4
D_EXPERT = 256
ROUTED_SCALE = 2.5
DEPTH = 2
DEEPNORM_ALPHA = (2.0 * DEPTH) ** 0.25
LN_EPS = 1e-5
NEG_INF = -1e30

LANES = 128
SUBLANES = 8
VMEM_LIMIT_BYTES = 56 * 1024 * 1024

POOL_HALO = 16
CONV_HALO = 32
L0_TIME_TILE = 256
ROUTER_TILE = 384
EXPERT_BLOCK = 256
MERGE_TILE = 512


def _cparams(sem):
    return pltpu.CompilerParams(dimension_semantics=sem, vmem_limit_bytes=VMEM_LIMIT_BYTES)


def _layer_norm(x, g, b):
    mu = jnp.mean(x, axis=-1, keepdims=True)
    xc = x - mu
    var = jnp.mean(xc * xc, axis=-1, keepdims=True)
    return xc * lax.rsqrt(var + LN_EPS) * g + b


def _const_spec(shape):
    nd = len(shape)
    return pl.BlockSpec(shape, lambda *_: (0,) * nd)


def _l0_prompt_kernel(x_ref, win_ref, pw_ref, ps_ref, cw_ref, cb_ref, cg_ref, cbeta_ref, wout_ref,
                      g1_ref, b1_ref, h_ref, pstate_ref, cstate_ref, ue_ref, ge_ref):
    tt = x_ref.shape[1]
    t = pl.program_id(1)

    @pl.when(t == 0)
    def _():
        ue_ref[0:POOL_HALO, :] = jnp.zeros((POOL_HALO, D_POOL), F32)
        ge_ref[0:CONV_HALO, :] = jnp.zeros((CONV_HALO, D_CONV), F32)

    @pl.when(t > 0)
    def _():
        ue_ref[0:POOL_HALO, :] = ue_ref[tt:tt + POOL_HALO, :]
        ge_ref[0:CONV_HALO, :] = ge_ref[tt:tt + CONV_HALO, :]

    x = x_ref[0]
    proj = jnp.dot(x.astype(BF16), win_ref[...], preferred_element_type=F32)
    u = proj[:, :D_POOL]
    a = proj[:, D_POOL:D_POOL + D_CONV]
    gate = proj[:, D_POOL + D_CONV:]
    glu = a * jax.nn.sigmoid(gate)
    ue_ref[POOL_HALO:POOL_HALO + tt, :] = u
    ge_ref[CONV_HALO:CONV_HALO + tt, :] = glu

    tg = t * tt + lax.broadcasted_iota(I32, (tt, 1), 0)
    parts = []
    for g, w in enumerate(POOL_WINDOWS):
        c0 = g * POOL_GROUP
        ug = u[:, c0:c0 + POOL_GROUP]
        s = ug
        for j in range(1, w):
            s = s + ue_ref[POOL_HALO - j:POOL_HALO - j + tt, c0:c0 + POOL_GROUP]
        cnt = jnp.minimum(tg + 1, w).astype(F32)
        pooled = s / cnt - ug
        parts.append(jnp.dot(pooled.astype(BF16), pw_ref[g], preferred_element_type=F32))
    yp = jnp.concatenate(parts, axis=1) * ps_ref[...]

    acc = glu * cw_ref[CONV_STATE:CONV_STATE + 1, :]
    off = CONV_HALO - CONV_STATE
    for j in range(CONV_STATE):
        acc = acc + ge_ref[off + j:off + j + tt, :] * cw_ref[j:j + 1, :]
    yn = _layer_norm(acc + cb_ref[...], cg_ref[...], cbeta_ref[...])
    yc = yn * jax.nn.sigmoid(yn)

    cat = jnp.concatenate([yp, yc], axis=1).astype(BF16)
    m = jnp.dot(cat, wout_ref[...], preferred_element_type=F32)
    h_ref[0] = _layer_norm(DEEPNORM_ALPHA * x + m, g1_ref[...], b1_ref[...])
    pstate_ref[0] = ue_ref[tt:tt + POOL_HALO, :]
    cstate_ref[0] = ge_ref[tt:tt + CONV_HALO, :]


def _l0_prompt(x, win, pw, ps, cw, cb, cg, cbeta, wout, g1, b1):
    B, S, D = x.shape
    tt = min(L0_TIME_TILE, S)
    assert S % tt == 0 and tt >= CONV_HALO
    return pl.pallas_call(
        _l0_prompt_kernel,
        grid=(B, S // tt),
        in_specs=[
            pl.BlockSpec((1, tt, D), lambda b, t: (b, t, 0)),
            _const_spec(win.shape), _const_spec(pw.shape), _const_spec(ps.shape), _const_spec(cw.shape),
            _const_spec(cb.shape), _const_spec(cg.shape), _const_spec(cbeta.shape), _const_spec(wout.shape),
            _const_spec(g1.shape), _const_spec(b1.shape),
        ],
        out_specs=[
            pl.BlockSpec((1, tt, D), lambda b, t: (b, t, 0)),
            pl.BlockSpec((1, POOL_HALO, D_POOL), lambda b, t: (b, 0, 0)),
            pl.BlockSpec((1, CONV_HALO, D_CONV), lambda b, t: (b, 0, 0)),
        ],
        out_shape=[
            jax.ShapeDtypeStruct((B, S, D), F32),
            jax.ShapeDtypeStruct((B, POOL_HALO, D_POOL), F32),
            jax.ShapeDtypeStruct((B, CONV_HALO, D_CONV), F32),
        ],
        scratch_shapes=[pltpu.VMEM((tt + POOL_HALO, D_POOL), F32), pltpu.VMEM((tt + CONV_HALO, D_CONV), F32)],
        compiler_params=_cparams(("arbitrary", "arbitrary")),
        name="l0_prompt",
    )(x, win, pw, ps, cw, cb, cg, cbeta, wout, g1, b1)


def _l0_sample_kernel(x_ref, sp_ref, sc_ref, win_ref, pw_ref, ps_ref, cw_ref, cb_ref, cg_ref, cbeta_ref,
                      wout_ref, g1_ref, b1_ref, h_ref, u_ref, glu_ref):
    x = x_ref[...]
    proj = jnp.dot(x.astype(BF16), win_ref[...], preferred_element_type=F32)
    u = proj[:, :D_POOL]
    a = proj[:, D_POOL:D_POOL + D_CONV]
    gate = proj[:, D_POOL + D_CONV:]
    glu = a * jax.nn.sigmoid(gate)
    u_ref[...] = u
    glu_ref[...] = glu

    parts = []
    for g, w in enumerate(POOL_WINDOWS):
        c0 = g * POOL_GROUP
        ug = u[:, c0:c0 + POOL_GROUP]
        past = sp_ref[:, POOL_STATE - (w - 1):POOL_STATE, c0:c0 + POOL_GROUP]
        s = ug + jnp.sum(past, axis=1)
        pooled = s / float(w) - ug
        parts.append(jnp.dot(pooled.astype(BF16), pw_ref[g], preferred_element_type=F32))
    yp = jnp.concatenate(parts, axis=1) * ps_ref[...]

    acc = glu * cw_ref[CONV_STATE:CONV_STATE + 1, :]
    acc = acc + jnp.sum(sc_ref[...] * cw_ref[0:CONV_STATE, :][None, :, :], axis=1)
    yn = _layer_norm(acc + cb_ref[...], cg_ref[...], cbeta_ref[...])
    yc = yn * jax.nn.sigmoid(yn)

    cat = jnp.concatenate([yp, yc], axis=1).astype(BF16)
    m = jnp.dot(cat, wout_ref[...], preferred_element_type=F32)
    h_ref[...] = _layer_norm(DEEPNORM_ALPHA * x + m, g1_ref[...], b1_ref[...])


def _l0_sample(x, sp, sc, win, pw, ps, cw, cb, cg, cbeta, wout, g1, b1):
    DB, D = x.shape
    args = (x, sp, sc, win, pw, ps, cw, cb, cg, cbeta, wout, g1, b1)
    return pl.pallas_call(
        _l0_sample_kernel,
        grid=(1,),
        in_specs=[_const_spec(a.shape) for a in args],
        out_specs=[_const_spec((DB, D)), _const_spec((DB, D_POOL)), _const_spec((DB, D_CONV))],
        out_shape=[
            jax.ShapeDtypeStruct((DB, D), F32),
            jax.ShapeDtypeStruct((DB, D_POOL), F32),
            jax.ShapeDtypeStruct((DB, D_CONV), F32),
        ],
        compiler_params=_cparams(("arbitrary",)),
        name="l0_sample",
    )(*args)


def _first_index_of_max(x, iota, size):
    m = jnp.max(x, axis=0, keepdims=True)
    f = jnp.min(jnp.where(x == m, iota, size), axis=0, keepdims=True)
    return m, f


def _router_kernel(h_ref, wrt_ref, bias_ref, tri_ref, idx_ref, w_ref, rank_ref, cnt_ref, run_ref):
    tm = h_ref.shape[0]
    E = N_EXPERTS
    PG = EXPERTS_PER_GROUP

    @pl.when(pl.program_id(0) == 0)
    def _():
        run_ref[...] = jnp.zeros_like(run_ref)

    logits = lax.dot_general(wrt_ref[...], h_ref[...].astype(BF16), (((1,), (1,)), ((), ())),
                             preferred_element_type=F32)
    scores = jax.nn.sigmoid(logits)
    sel = scores + bias_ref[...]

    io_g = lax.broadcasted_iota(I32, (PG, tm), 0)
    rows = []
    for g in range(N_EXPERT_GROUPS):
        blk = sel[g * PG:(g + 1) * PG, :]
        m1, f1 = _first_index_of_max(blk, io_g, PG)
        m2 = jnp.max(jnp.where(io_g == f1, -jnp.inf, blk), axis=0, keepdims=True)
        rows.append(m1 + m2)
    gs = jnp.concatenate(rows, axis=0)

    io_n = lax.broadcasted_iota(I32, (N_EXPERT_GROUPS, tm), 0)
    gsel = jnp.zeros((N_EXPERT_GROUPS, tm), F32)
    cur = gs
    for _ in range(TOPK_GROUPS):
        _, f = _first_index_of_max(cur, io_n, N_EXPERT_GROUPS)
        hit = io_n == f
        gsel = jnp.where(hit, 1.0, gsel)
        cur = jnp.where(hit, -jnp.inf, cur)
    masked = jnp.concatenate(
        [jnp.where(gsel[g:g + 1, :] > 0.5, sel[g * PG:(g + 1) * PG, :], -jnp.inf) for g in range(N_EXPERT_GROUPS)],
        axis=0)

    io_e = lax.broadcasted_iota(I32, (E, tm), 0)
    onehot = jnp.zeros((E, tm), F32)
    idx_rows, sc_rows = [], []
    cur = masked
    for _ in range(TOP_K):
        _, f = _first_index_of_max(cur, io_e, E)
        hit = io_e == f
        idx_rows.append(f)
        sc_rows.append(jnp.sum(jnp.where(hit, scores, 0.0), axis=0, keepdims=True))
        onehot = jnp.where(hit, 1.0, onehot)
        cur = jnp.where(hit, -jnp.inf, cur)
    sc = jnp.concatenate(sc_rows, axis=0)
    idx_ref[0] = jnp.concatenate(idx_rows, axis=0)
    w_ref[...] = sc / jnp.sum(sc, axis=0, keepdims=True) * ROUTED_SCALE

    before = jnp.dot(onehot.astype(BF16), tri_ref[...], preferred_element_type=F32) + run_ref[...]
    rank_rows = [jnp.sum(jnp.where(io_e == f, before, 0.0), axis=0, keepdims=True) for f in idx_rows]
    rank_ref[0] = jnp.concatenate(rank_rows, axis=0).astype(I32)
    run_ref[...] = run_ref[...] + jnp.sum(onehot, axis=1, keepdims=True)
    cnt_ref[...] = run_ref[...]


def _router(tok, wrt, bias):
    N, D = tok.shape
    tm = ROUTER_TILE
    assert N % tm == 0
    tri = (jnp.arange(tm)[:, None] < jnp.arange(tm)[None, :]).astype(BF16)
    return pl.pallas_call(
        _router_kernel,
        grid=(N // tm,),
        in_specs=[
            pl.BlockSpec((tm, D), lambda i: (i, 0)),
            _const_spec(wrt.shape), _const_spec(bias.shape), _const_spec(tri.shape),
        ],
        out_specs=[
            pl.BlockSpec((1, TOP_K, tm), lambda i: (i, 0, 0)),
            pl.BlockSpec((TOP_K, tm), lambda i: (0, i)),
            pl.BlockSpec((1, TOP_K, tm), lambda i: (i, 0, 0)),
            _const_spec((N_EXPERTS, 1)),
        ],
        out_shape=[
            jax.ShapeDtypeStruct((N // tm, TOP_K, tm), I32),
            jax.ShapeDtypeStruct((TOP_K, N), F32),
            jax.ShapeDtypeStruct((N // tm, TOP_K, tm), I32),
            jax.ShapeDtypeStruct((N_EXPERTS, 1), F32),
        ],
        scratch_shapes=[pltpu.VMEM((N_EXPERTS, 1), F32)],
        compiler_params=_cparams(("arbitrary",)),
        name="moe_router",
    )(tok, wrt, bias, tri)


def _load_slot_tables(i, idx_hbm, rank_hbm, idx_s, rank_s, sem):
    copies = [pltpu.make_async_copy(idx_hbm.at[i], idx_s, sem), pltpu.make_async_copy(rank_hbm.at[i], rank_s, sem)]
    for c in copies:
        c.start()
    for c in copies:
        c.wait()


def _dispatch_kernel(idx_hbm, rank_hbm, pstart_ref, pend_ref, cnt_ref, tok_ref, xs_hbm, idx_s, rank_s, zero_ref,
                     sem_idx, sem_rows):
    i = pl.program_id(0)
    tm = tok_ref.shape[0]
    blk = zero_ref.shape[0]

    @pl.when(i == 0)
    def _():
        zero_ref[...] = jnp.zeros_like(zero_ref)

        def zero_copy(e):
            last_block = pl.multiple_of(pend_ref[e] - blk, blk)
            return pltpu.make_async_copy(zero_ref, xs_hbm.at[pl.ds(last_block, blk)], sem_rows)

        def start(e, c):
            @pl.when(cnt_ref[e] > 0)
            def _():
                zero_copy(e).start()
            return c

        def wait(e, c):
            @pl.when(cnt_ref[e] > 0)
            def _():
                zero_copy(e).wait()
            return c

        lax.fori_loop(0, N_EXPERTS, start, 0)
        lax.fori_loop(0, N_EXPERTS, wait, 0)

    _load_slot_tables(i, idx_hbm, rank_hbm, idx_s, rank_s, sem_idx)

    def row_copy(t, k):
        slot = pstart_ref[idx_s[k, t]] + rank_s[k, t]
        return pltpu.make_async_copy(tok_ref.at[t], xs_hbm.at[slot], sem_rows)

    def start(t, c):
        for k in range(TOP_K):
            row_copy(t, k).start()
        return c

    def wait(t, c):
        for k in range(TOP_K):
            pltpu.make_async_copy(tok_ref.at[0], xs_hbm.at[0], sem_rows).wait()
        return c

    lax.fori_loop(0, tm, start, 0)
    lax.fori_loop(0, tm, wait, 0)


def _dispatch(tok3, idx3, rank3, pstart, pend, counts, n_slots):
    N = tok3.shape[0]
    tm = idx3.shape[2]
    assert N % tm == 0 and idx3.shape == rank3.shape == (N // tm, TOP_K, tm)
    smem = pl.BlockSpec(memory_space=pltpu.SMEM)
    return pl.pallas_call(
        _dispatch_kernel,
        grid=(N // tm,),
        in_specs=[
            pl.BlockSpec(memory_space=pl.ANY), pl.BlockSpec(memory_space=pl.ANY), smem, smem, smem,
            pl.BlockSpec((tm, SUBLANES, LANES), lambda i: (i, 0, 0)),
        ],
        out_specs=pl.BlockSpec(memory_space=pl.ANY),
        scratch_shapes=[
            pltpu.SMEM((TOP_K, tm), I32),
            pltpu.SMEM((TOP_K, tm), I32),
            pltpu.VMEM((EXPERT_BLOCK, SUBLANES, LANES), F32),
            pltpu.SemaphoreType.DMA,
            pltpu.SemaphoreType.DMA,
        ],
        out_shape=jax.ShapeDtypeStruct((n_slots, SUBLANES, LANES), F32),
        compiler_params=_cparams(("arbitrary",)),
        name="moe_dispatch",
    )(idx3, rank3, pstart, pend, counts, tok3)


def _ffn_kernel(be_ref, nu_ref, xs_ref, wg_ref, wu_ref, wd_ref, ys_ref, wgu_s, wd_s):
    b = pl.program_id(0)
    prev = be_ref[jnp.maximum(b - 1, 0)]
    new_expert = jnp.logical_or(b == 0, be_ref[b] != prev)

    @pl.when(jnp.logical_and(new_expert, b < nu_ref[0]))
    def _():
        wgu_s[:, :D_EXPERT] = wg_ref[0, 0].astype(BF16)
        wgu_s[:, D_EXPERT:] = wu_ref[0, 0].astype(BF16)
        wd_s[...] = wd_ref[0, 0].astype(BF16)

    @pl.when(b < nu_ref[0])
    def _():
        blk = xs_ref.shape[0] // SUBLANES
        x = jnp.concatenate([xs_ref[pl.ds(c, blk, stride=SUBLANES), :] for c in range(SUBLANES)], axis=1)
        gu = jnp.dot(x.astype(BF16), wgu_s[...], preferred_element_type=F32)
        gt = gu[:, :D_EXPERT]
        hid = gt * jax.nn.sigmoid(gt) * gu[:, D_EXPERT:]
        y = jnp.dot(hid.astype(BF16), wd_s[...], preferred_element_type=F32)
        for c in range(SUBLANES):
            ys_ref[pl.ds(c, blk, stride=SUBLANES), :] = y[:, c * LANES:(c + 1) * LANES]


def _expert_ffn(xs, block_e, n_used, layer, wg, wu, wd):
    n_slots = xs.shape[0]
    D = SUBLANES * LANES
    blk = EXPERT_BLOCK
    n_blocks = n_slots // blk

    def row_map(b, be, nu):
        return (jnp.minimum(b, nu[0] - 1), 0)

    def w_map(b, be, nu):
        return (layer, be[b], 0, 0)

    return pl.pallas_call(
        _ffn_kernel,
        grid_spec=pltpu.PrefetchScalarGridSpec(
            num_scalar_prefetch=2,
            grid=(n_blocks,),
            in_specs=[
                pl.BlockSpec((blk * SUBLANES, LANES), row_map),
                pl.BlockSpec((1, 1, D, D_EXPERT), w_map),
                pl.BlockSpec((1, 1, D, D_EXPERT), w_map),
                pl.BlockSpec((1, 1, D_EXPERT, D), w_map),
            ],
            out_specs=pl.BlockSpec((blk * SUBLANES, LANES), row_map),
            scratch_shapes=[pltpu.VMEM((D, 2 * D_EXPERT), BF16), pltpu.VMEM((D_EXPERT, D), BF16)],
        ),
        out_shape=jax.ShapeDtypeStruct((n_slots * SUBLANES, LANES), F32),
        compiler_params=_cparams(("arbitrary",)),
        name="moe_ffn",
    )(block_e, n_used, xs.reshape(n_slots * SUBLANES, LANES), wg, wu, wd).reshape(n_slots, SUBLANES, LANES)


def _combine_kernel(idx_hbm, rank_hbm, pstart_ref, ys_hbm, tok_ref, w_ref, wsg_ref, wsu_ref, wsd_ref, g_ref, b_ref,
                    out_ref, idx_s, rank_s, buf_ref, sem_idx, sem_rows):
    i = pl.program_id(0)
    tm = tok_ref.shape[0]

    _load_slot_tables(i, idx_hbm, rank_hbm, idx_s, rank_s, sem_idx)

    def row_copy(t, k):
        slot = pstart_ref[idx_s[k, t]] + rank_s[k, t]
        row = pl.multiple_of(t * SUBLANES, SUBLANES)
        return pltpu.make_async_copy(ys_hbm.at[slot], buf_ref.at[k, pl.ds(row, SUBLANES)], sem_rows)

    def start(t, c):
        for k in range(TOP_K):
            row_copy(t, k).start()
        return c

    def wait(t, c):
        for k in range(TOP_K):
            pltpu.make_async_copy(ys_hbm.at[0], buf_ref.at[0, pl.ds(0, SUBLANES)], sem_rows).wait()
        return c

    lax.fori_loop(0, tm, start, 0)

    h = tok_ref[...]
    hb = h.astype(BF16)
    gt = jnp.dot(hb, wsg_ref[...], preferred_element_type=F32)
    up = jnp.dot(hb, wsu_ref[...], preferred_element_type=F32)
    hid = gt * jax.nn.sigmoid(gt) * up
    f = jnp.dot(hid.astype(BF16), wsd_ref[...], preferred_element_type=F32)

    lax.fori_loop(0, tm, wait, 0)
    w = w_ref[...]
    wb = [jnp.broadcast_to(w[:, k:k + 1], (tm, LANES)) for k in range(TOP_K)]
    chunks = []
    for c in range(SUBLANES):
        acc = f[:, c * LANES:(c + 1) * LANES]
        for k in range(TOP_K):
            acc = acc + buf_ref[k, pl.ds(c, tm, stride=SUBLANES), :] * wb[k]
        chunks.append(acc)
    f = jnp.concatenate(chunks, axis=1)
    out_ref[...] = _layer_norm(DEEPNORM_ALPHA * h + f, g_ref[...], b_ref[...])


def _combine(tok, ys, idx3, rank3, pstart, w_tok, wsg, wsu, wsd, g, b):
    N, D = tok.shape
    tm = idx3.shape[2]
    assert N % tm == 0 and idx3.shape == rank3.shape == (N // tm, TOP_K, tm)
    return pl.pallas_call(
        _combine_kernel,
        grid=(N // tm,),
        in_specs=[
            pl.BlockSpec(memory_space=pl.ANY), pl.BlockSpec(memory_space=pl.ANY),
            pl.BlockSpec(memory_space=pltpu.SMEM),
            pl.BlockSpec(memory_space=pl.ANY),
            pl.BlockSpec((tm, D), lambda i: (i, 0)),
            pl.BlockSpec((tm, TOP_K), lambda i: (i, 0)),
            _const_spec(wsg.shape), _const_spec(wsu.shape), _const_spec(wsd.shape),
            _const_spec(g.shape), _const_spec(b.shape),
        ],
        out_specs=pl.BlockSpec((tm, D), lambda i: (i, 0)),
        scratch_shapes=[
            pltpu.SMEM((TOP_K, tm), I32),
            pltpu.SMEM((TOP_K, tm), I32),
            pltpu.VMEM((TOP_K, tm * SUBLANES, LANES), F32),
            pltpu.SemaphoreType.DMA,
            pltpu.SemaphoreType.DMA,
        ],
        out_shape=jax.ShapeDtypeStruct((N, D), F32),
        compiler_params=_cparams(("arbitrary",)),
        name="moe_combine",
    )(idx3, rank3, pstart, ys, tok, w_tok, wsg, wsu, wsd, g, b)


def _moe_layer(tok, layer, w_router, router_bias, we_gate, we_up, we_down, ws_gate, ws_up, ws_down, ln_g, ln_b):
    N, D = tok.shape
    blk = EXPERT_BLOCK
    idx3, w_t, rank3, cnt = _router(tok, w_router.T.astype(BF16), router_bias.reshape(N_EXPERTS, 1))

    counts = cnt[:, 0].astype(I32)
    pcounts = (counts + blk - 1) // blk * blk
    pend = jnp.cumsum(pcounts).astype(I32)
    pstart = pend - pcounts
    n_blocks = N * TOP_K // blk + N_EXPERTS
    block_start = jnp.arange(n_blocks, dtype=I32) * blk
    block_e = jnp.minimum(jnp.sum(pend[None, :] <= block_start[:, None], axis=1), N_EXPERTS - 1).astype(I32)
    n_used = (pend[-1] // blk).reshape(1)

    xs = _dispatch(tok.reshape(N, SUBLANES, LANES), idx3, rank3, pstart, pend, counts, n_blocks * blk)
    ys = _expert_ffn(xs, block_e, n_used, layer, we_gate, we_up, we_down)
    return _combine(tok, ys, idx3, rank3, pstart, w_t.T,
                    ws_gate.astype(BF16), ws_up.astype(BF16), ws_down.astype(BF16),
                    ln_g.reshape(1, D), ln_b.reshape(1, D))


def _rel_bucket(dist):
    exact = N_BUCKETS // 2
    df = jnp.maximum(dist, 1).astype(F32)
    large = exact + (jnp.log(df / exact) / math.log(MAX_DISTANCE / exact) * (N_BUCKETS - exact)).astype(I32)
    large = jnp.minimum(large, N_BUCKETS - 1)
    return jnp.where(dist < exact, dist, large)


def _qkv_kernel(x_ref, w_ref, q_ref, k_ref, v_ref, kn_ref, vn_ref):
    tl = x_ref.shape[0]
    y = jnp.dot(x_ref[...].astype(BF16), w_ref[...], preferred_element_type=F32)
    q_ref[...] = y[:, :D_ATTN].astype(BF16)
    k = y[:, D_ATTN:2 * D_ATTN]
    v = y[:, 2 * D_ATTN:]
    k_ref[...] = k.astype(BF16)
    v_ref[...] = v.astype(BF16)
    for h in range(HEADS):
        kn_ref[pl.ds(h, tl, stride=HEADS), :] = k[:, h * HEAD_DIM:(h + 1) * HEAD_DIM]
        vn_ref[pl.ds(h, tl, stride=HEADS), :] = v[:, h * HEAD_DIM:(h + 1) * HEAD_DIM]


def _qkv_prompt(h, w_g):
    M, D = h.shape
    tl = min(M, 512)
    assert M % tl == 0
    row_spec = pl.BlockSpec((tl, D_ATTN), lambda i: (i, 0))
    head_spec = pl.BlockSpec((tl * HEADS, HEAD_DIM), lambda i: (i, 0))
    return pl.pallas_call(
        _qkv_kernel,
        grid=(M // tl,),
        in_specs=[pl.BlockSpec((tl, D), lambda i: (i, 0)), _const_spec(w_g.shape)],
        out_specs=[row_spec, row_spec, row_spec, head_spec, head_spec],
        out_shape=[jax.ShapeDtypeStruct((M, D_ATTN), BF16)] * 3
        + [jax.ShapeDtypeStruct((M * HEADS, HEAD_DIM), F32)] * 2,
        compiler_params=_cparams(("arbitrary",)),
        name="qkv_prompt",
    )(h, w_g)


def _attn_prompt_kernel(q_ref, kp_ref, kc_ref, vp_ref, vc_ref, bias_ref, o_ref, lse_ref, *, steps):
    n = pl.program_id(1)
    qb = QUERY_BLOCK
    qi = lax.broadcasted_iota(I32, (qb, 2 * qb), 0)
    kj = lax.broadcasted_iota(I32, (qb, 2 * qb), 1)
    dist = qi + qb - kj
    valid = (dist >= 0) & (dist <= steps) & ((n > 0) | (kj >= qb))
    low = lax.broadcasted_iota(I32, (1, 2 * HEAD_DIM), 1) < HEAD_DIM
    scale = HEAD_DIM ** -0.5
    for p in range(HEADS // 2):
        c0 = p * 2 * HEAD_DIM
        q2 = q_ref[0, :, c0:c0 + 2 * HEAD_DIM]
        k2 = jnp.concatenate([kp_ref[0, :, c0:c0 + 2 * HEAD_DIM], kc_ref[0, :, c0:c0 + 2 * HEAD_DIM]], axis=0)
        v2 = jnp.concatenate([vp_ref[0, :, c0:c0 + 2 * HEAD_DIM], vc_ref[0, :, c0:c0 + 2 * HEAD_DIM]], axis=0)
        outs, lses = [], []
        for half in range(2):
            keep = low if half == 0 else jnp.logical_not(low)
            qh = jnp.where(keep, q2, jnp.zeros_like(q2))
            s = lax.dot_general(qh, k2, (((1,), (1,)), ((), ())), preferred_element_type=F32)
            s = s * scale + bias_ref[2 * p + half]
            s = jnp.where(valid, s, NEG_INF)
            m = jnp.max(s, axis=-1, keepdims=True)
            e = jnp.exp(s - m)
            l = jnp.sum(e, axis=-1, keepdims=True)
            prob = e / l
            outs.append(jnp.dot(prob.astype(BF16), v2, preferred_element_type=F32))
            lses.append(m + jnp.log(l))
        o_ref[0, :, c0:c0 + 2 * HEAD_DIM] = jnp.where(low, outs[0], outs[1])
        lse_ref[0, :, c0:c0 + 2 * HEAD_DIM] = jnp.where(low, lses[0], lses[1])


def _attn_prompt(q, k, v, bias, B, dil, steps):
    M = q.shape[0]
    S = M // B
    L = S // dil
    qb = QUERY_BLOCK
    assert S % dil == 0 and L % qb == 0
    cur = pl.BlockSpec((1, qb, D_ATTN), lambda bd, n: (bd // dil, n, bd % dil))
    prev = pl.BlockSpec((1, qb, D_ATTN), lambda bd, n: (bd // dil, jnp.maximum(n - 1, 0), bd % dil))
    qv, kv, vv = (a.reshape(B, L, dil * D_ATTN) for a in (q, k, v))
    o, lse = pl.pallas_call(
        functools.partial(_attn_prompt_kernel, steps=steps),
        grid=(B * dil, L // qb),
        in_specs=[cur, prev, cur, prev, cur, _const_spec(bias.shape)],
        out_specs=[cur, cur],
        out_shape=[jax.ShapeDtypeStruct((B, L, dil * D_ATTN), F32)] * 2,
        compiler_params=_cparams(("arbitrary", "arbitrary")),
        name="attn_prompt",
    )(qv, kv, kv, vv, vv, bias)
    return o.reshape(M, D_ATTN), lse.reshape(M, D_ATTN)


def _merge_kernel(o1, o2, o3, l1, l2, l3, h_ref, wo_ref, g_ref, b_ref, out_ref):
    a1, a2, a3 = l1[...], l2[...], l3[...]
    m = jnp.maximum(jnp.maximum(a1, a2), a3)
    e1, e2, e3 = jnp.exp(a1 - m), jnp.exp(a2 - m), jnp.exp(a3 - m)
    o = (e1 * o1[...] + e2 * o2[...] + e3 * o3[...]) / (e1 + e2 + e3)
    y = jnp.dot(o.astype(BF16), wo_ref[...], preferred_element_type=F32)
    out_ref[...] = _layer_norm(DEEPNORM_ALPHA * h_ref[...] + y, g_ref[...], b_ref[...])


def _merge(os_, ls_, h, wo, g, b):
    M, D = h.shape
    tm = min(MERGE_TILE, M)
    assert M % tm == 0
    a_spec = pl.BlockSpec((tm, D_ATTN), lambda i: (i, 0))
    return pl.pallas_call(
        _merge_kernel,
        grid=(M // tm,),
        in_specs=[a_spec] * 6 + [pl.BlockSpec((tm, D), lambda i: (i, 0)), _const_spec(wo.shape),
                                 _const_spec(g.shape), _const_spec(b.shape)],
        out_specs=pl.BlockSpec((tm, D), lambda i: (i, 0)),
        out_shape=jax.ShapeDtypeStruct((M, D), F32),
        compiler_params=_cparams(("arbitrary",)),
        name="attn_merge",
    )(*os_, *ls_, h, wo, g, b)


def _mm_kernel(x_ref, w_ref, o_ref):
    o_ref[...] = jnp.dot(x_ref[...].astype(BF16), w_ref[...], preferred_element_type=F32)


def _matmul(x, w, tn):
    M, K = x.shape
    _, N = w.shape
    assert N % tn == 0
    return pl.pallas_call(
        _mm_kernel,
        grid=(N // tn,),
        in_specs=[_const_spec((M, K)), pl.BlockSpec((K, tn), lambda j: (0, j))],
        out_specs=pl.BlockSpec((M, tn), lambda j: (0, j)),
        out_shape=jax.ShapeDtypeStruct((M, N), F32),
        compiler_params=_cparams(("arbitrary",)),
        name="matmul",
    )(x, w)


def _round_bf16(x):
    return x.astype(BF16).astype(F32)


def _attn_sample_kernel(q_ref, kn_ref, vn_ref, kc_ref, vc_ref, bk_ref, b0_ref, o_ref, lse_ref):
    scale = HEAD_DIM ** -0.5
    q = _round_bf16(q_ref[...])
    kn = _round_bf16(kn_ref[...])
    vn = _round_bf16(vn_ref[...])
    kc = _round_bf16(kc_ref[...])
    vc = _round_bf16(vc_ref[...])
    s = jnp.sum(kc * q[:, None], axis=-1, keepdims=True) * scale + bk_ref[...][None]
    s0 = jnp.sum(q * kn, axis=-1, keepdims=True) * scale + b0_ref[...][None]
    m = jnp.maximum(jnp.max(s, axis=1), s0)
    e = jnp.exp(s - m[:, None])
    e0 = jnp.exp(s0 - m)
    l = jnp.sum(e, axis=1) + e0
    p = _round_bf16(e / l[:, None])
    p0 = _round_bf16(e0 / l)
    o_ref[...] = jnp.sum(p * vc, axis=1) + p0 * vn
    lse_ref[...] = m + jnp.log(l)


def _attn_sample(q, kn, vn, ck, cv, j, bias_keys, bias_self, tb=4):
    DB = q.shape[0]
    W = ck.shape[1]
    assert DB % tb == 0
    new_spec = pl.BlockSpec((tb, HEADS, HEAD_DIM), lambda i: (i, 0, 0))
    cache_spec = pl.BlockSpec((tb, W, None, HEADS, HEAD_DIM), lambda i: (j * (DB // tb) + i, 0, 0, 0, 0))
    return pl.pallas_call(
        _attn_sample_kernel,
        grid=(DB // tb,),
        in_specs=[new_spec, new_spec, new_spec, cache_spec, cache_spec,
                  _const_spec(bias_keys.shape), _const_spec(bias_self.shape)],
        out_specs=[new_spec, pl.BlockSpec((tb, HEADS, 1), lambda i: (i, 0, 0))],
        out_shape=[jax.ShapeDtypeStruct((DB, HEADS, HEAD_DIM), F32), jax.ShapeDtypeStruct((DB, HEADS, 1), F32)],
        compiler_params=_cparams(("arbitrary",)),
        name="attn_sample",
    )(q, kn, vn, ck, cv, bias_keys, bias_self)


def _bias_lookup(tab, buckets):
    onehot = (buckets[..., None] == jnp.arange(N_BUCKETS, dtype=I32)).astype(F32)
    return jnp.einsum('...n,nh->...h', onehot, tab, precision=lax.Precision.HIGHEST)


def _dilated_layer(hp, hs, caches, j, w_qkv, w_o, rel_bias, g1, b1):
    B, S, D = hp.shape
    DB = hs.shape[0]
    w_qkv_b = w_qkv.astype(BF16)
    w_o_b = w_o.astype(BF16)
    qb = QUERY_BLOCK
    qi = jnp.arange(qb, dtype=I32)[:, None]
    kj = jnp.arange(2 * qb, dtype=I32)[None, :]
    dist = qi + qb - kj
    qkv_s = _matmul(hs, w_qkv_b, 512)

    def sample_cols(part, g):
        c0 = (part * N_GROUPS + g) * D_ATTN
        return qkv_s[:, c0:c0 + D_ATTN].reshape(DB, HEADS, HEAD_DIM)

    os_, ls_, kp, vp = [], [], [], []
    os_s, ls_s, ks, vs = [], [], [], []
    for g, (window, dil) in enumerate(DIL_CONFIGS):
        steps = window // dil
        assert steps == qb
        tab = rel_bias[:, g * HEADS:(g + 1) * HEADS]
        cols = [w_qkv_b[:, (part * N_GROUPS + g) * D_ATTN:(part * N_GROUPS + g + 1) * D_ATTN] for part in range(3)]
        q, k, v, kn, vn = _qkv_prompt(hp.reshape(B * S, D), jnp.concatenate(cols, axis=1))
        bias = _bias_lookup(tab, _rel_bucket(jnp.maximum(dist, 0) * dil)).transpose(2, 0, 1)
        o, lse = _attn_prompt(q, k, v, bias, B, dil, steps)
        os_.append(o)
        ls_.append(lse)
        keep = min(window, S)
        kp.append(kn.reshape(B, S, HEADS, HEAD_DIM)[:, S - keep:])
        vp.append(vn.reshape(B, S, HEADS, HEAD_DIM)[:, S - keep:])
        sb = _bias_lookup(tab, _rel_bucket(jnp.arange(steps + 1, dtype=I32) * dil))
        ck, cv = caches[g]
        n_layers, _, wb = ck.shape[:3]
        assert wb == steps * dil and ck.shape[1] == DB
        view = (n_layers * DB, steps, dil, HEADS, HEAD_DIM)
        q_s, k_s, v_s = sample_cols(0, g), sample_cols(1, g), sample_cols(2, g)
        o_s, lse_s = _attn_sample(q_s, k_s, v_s, ck.reshape(view), cv.reshape(view), j,
                                  sb[1:][::-1][:, :, None], sb[0][:, None])
        os_s.append(o_s.reshape(DB, D_ATTN))
        ls_s.append(jnp.broadcast_to(lse_s, (DB, HEADS, HEAD_DIM)).reshape(DB, D_ATTN))
        ks.append(k_s[:, None])
        vs.append(v_s[:, None])
    h_p = _merge(os_, ls_, hp.reshape(B * S, D), w_o_b, g1, b1).reshape(B, S, D)
    h_s = _merge(os_s, ls_s, hs, w_o_b, g1, b1)
    return h_p, h_s, kp, vp, ks, vs


def _pool_conv_layer(hp, hs, state_pool, state_conv, w_in, pool_w, pool_scale, conv_w, conv_b, ln_g, ln_b, w_out,
                     g1, b1):
    params = (w_in.astype(BF16), pool_w.astype(BF16), pool_scale.reshape(1, D_POOL), conv_w,
              conv_b.reshape(1, D_CONV), ln_g.reshape(1, D_CONV), ln_b.reshape(1, D_CONV), w_out.astype(BF16), g1, b1)
    h_p, pst, cst = _l0_prompt(hp, *params)
    h_s, u_s, glu_s = _l0_sample(hs, state_pool, state_conv, *params)
    pool_p = pst[:, POOL_HALO - POOL_STATE:]
    conv_p = cst[:, CONV_HALO - CONV_STATE:]
    pool_s = jnp.concatenate([state_pool[:, 1:], u_s[:, None, :]], axis=1)
    conv_s = jnp.concatenate([state_conv[:, 1:], glu_s[:, None, :]], axis=1)
    return h_p, h_s, pool_p, conv_p, pool_s, conv_s


def kernel(x_prompt, x_sample, state_pool, state_conv, cache_k1, cache_v1, cache_k2, cache_v2, cache_k3, cache_v3,
           w_in_ab, pool_w, pool_scale, conv_w, conv_b, conv_ln_g, conv_ln_b, w_out_ab, w_qkv, w_o, rel_bias,
           ln1_g, ln1_b, ln2_g, ln2_b, w_router, router_bias, we_gate, we_up, we_down, ws_gate, ws_up, ws_down):
    B, S, D = x_prompt.shape
    DB, T, _ = x_sample.shape
    assert T == 1 and D == D_MODEL
    hp = x_prompt
    hs = x_sample.reshape(DB, D)
    caches_k = (cache_k1, cache_k2, cache_k3)
    caches_v = (cache_v1, cache_v2, cache_v3)
    pool_p, conv_p, pool_s, conv_s = [], [], [], []
    kp = [[] for _ in range(N_GROUPS)]
    vp = [[] for _ in range(N_GROUPS)]
    ksm = [[] for _ in range(N_GROUPS)]
    vsm = [[] for _ in range(N_GROUPS)]
    for layer in range(DEPTH):
        j = layer // 2
        g1 = ln1_g[layer].reshape(1, D)
        b1 = ln1_b[layer].reshape(1, D)
        if layer % 2 == 0:
            hp, hs, pp, cp, ps, cs = _pool_conv_layer(
                hp, hs, state_pool[j], state_conv[j], w_in_ab[j], pool_w[j], pool_scale[j], conv_w[j], conv_b[j],
                conv_ln_g[j], conv_ln_b[j], w_out_ab[j], g1, b1)
            pool_p.append(pp)
            conv_p.append(cp)
            pool_s.append(ps)
            conv_s.append(cs)
        else:
            caches = [(caches_k[g], caches_v[g]) for g in range(N_GROUPS)]
            hp, hs, nkp, nvp, nks, nvs = _dilated_layer(hp, hs, caches, j, w_qkv[j], w_o[j], rel_bias, g1, b1)
            for g in range(N_GROUPS):
                kp[g].append(nkp[g])
                vp[g].append(nvp[g])
                ksm[g].append(nks[g])
                vsm[g].append(nvs[g])
        tok = jnp.concatenate([hp.reshape(B * S, D), hs], axis=0)
        tok = _moe_layer(tok, layer, w_router[layer], router_bias[layer], we_gate, we_up, we_down,
                         ws_gate[layer], ws_up[layer], ws_down[layer], ln2_g[layer], ln2_b[layer])
        hp = tok[:B * S].reshape(B, S, D)
        hs = tok[B * S:]
    return (hp, hs.reshape(DB, T, D),
            jnp.stack(pool_p), jnp.stack(conv_p),
            jnp.stack(kp[0]), jnp.stack(vp[0]), jnp.stack(kp[1]), jnp.stack(vp[1]), jnp.stack(kp[2]), jnp.stack(vp[2]),
            jnp.stack(pool_s), jnp.stack(conv_s),
            jnp.stack(ksm[0]), jnp.stack(vsm[0]), jnp.stack(ksm[1]), jnp.stack(vsm[1]),
            jnp.stack(ksm[2]), jnp.stack(vsm[2]))
```

```python
import functools
import math

import jax
import jax.numpy as jnp
from jax import lax
from jax.experimental import pallas as pl
from jax.experimental.pallas import tpu as pltpu

F32 = jnp.float32
BF16 = jnp.bfloat16
I32 = jnp.int32

D_MODEL = 1024
D_POOL = 512
D_CONV = 512
POOL_WINDOWS = (2, 4, 8, 16)
POOL_GROUP = 128
POOL_STATE = 15
CONV_WIDTH = 31
CONV_STATE = 30
DIL_CONFIGS = ((128, 1), (512, 4), (2048, 16))
N_GROUPS = 3
HEADS = 8
HEAD_DIM = 64
D_ATTN = HEADS * HEAD_DIM
QUERY_BLOCK = 128
N_BUCKETS = 32
MAX_DISTANCE = 2048
N_EXPERTS = 256
TOP_K = 8
N_EXPERT_GROUPS = 8
EXPERTS_PER_GROUP = N_EXPERTS // N_EXPERT_GROUPS
TOPK_GROUPS = 4
D_EXPERT = 256
ROUTED_SCALE = 2.5
DEPTH = 2
DEEPNORM_ALPHA = (2.0 * DEPTH) ** 0.25
LN_EPS = 1e-5
NEG_INF = -1e30

LANES = 128
SUBLANES = 8
VMEM_LIMIT_BYTES = 56 * 1024 * 1024

POOL_HALO = 16
CONV_HALO = 32
L0_TIME_TILE = 256
ROUTER_TILE = 384
EXPERT_BLOCK = 256
MERGE_TILE = 512


def _cparams(sem):
    return pltpu.CompilerParams(dimension_semantics=sem, vmem_limit_bytes=VMEM_LIMIT_BYTES)


def _layer_norm(x, g, b):
    mu = jnp.mean(x, axis=-1, keepdims=True)
    xc = x - mu
    var = jnp.mean(xc * xc, axis=-1, keepdims=True)
    return xc * lax.rsqrt(var + LN_EPS) * g + b


def _const_spec(shape):
    nd = len(shape)
    return pl.BlockSpec(shape, lambda *_: (0,) * nd)


def _l0_prompt_kernel(x_ref, win_ref, pw_ref, ps_ref, cw_ref, cb_ref, cg_ref, cbeta_ref, wout_ref,
                      g1_ref, b1_ref, h_ref, pstate_ref, cstate_ref, ue_ref, ge_ref):
    tt = x_ref.shape[1]
    t = pl.program_id(1)

    @pl.when(t == 0)
    def _():
        ue_ref[0:POOL_HALO, :] = jnp.zeros((POOL_HALO, D_POOL), F32)
        ge_ref[0:CONV_HALO, :] = jnp.zeros((CONV_HALO, D_CONV), F32)

    @pl.when(t > 0)
    def _():
        ue_ref[0:POOL_HALO, :] = ue_ref[tt:tt + POOL_HALO, :]
        ge_ref[0:CONV_HALO, :] = ge_ref[tt:tt + CONV_HALO, :]

    x = x_ref[0]
    proj = jnp.dot(x.astype(BF16), win_ref[...], preferred_element_type=F32)
    u = proj[:, :D_POOL]
    a = proj[:, D_POOL:D_POOL + D_CONV]
    gate = proj[:, D_POOL + D_CONV:]
    glu = a * jax.nn.sigmoid(gate)
    ue_ref[POOL_HALO:POOL_HALO + tt, :] = u
    ge_ref[CONV_HALO:CONV_HALO + tt, :] = glu

    tg = t * tt + lax.broadcasted_iota(I32, (tt, 1), 0)
    parts = []
    for g, w in enumerate(POOL_WINDOWS):
        c0 = g * POOL_GROUP
        ug = u[:, c0:c0 + POOL_GROUP]
        s = ug
        for j in range(1, w):
            s = s + ue_ref[POOL_HALO - j:POOL_HALO - j + tt, c0:c0 + POOL_GROUP]
        cnt = jnp.minimum(tg + 1, w).astype(F32)
        pooled = s / cnt - ug
        parts.append(jnp.dot(pooled.astype(BF16), pw_ref[g], preferred_element_type=F32))
    yp = jnp.concatenate(parts, axis=1) * ps_ref[...]

    acc = glu * cw_ref[CONV_STATE:CONV_STATE + 1, :]
    off = CONV_HALO - CONV_STATE
    for j in range(CONV_STATE):
        acc = acc + ge_ref[off + j:off + j + tt, :] * cw_ref[j:j + 1, :]
    yn = _layer_norm(acc + cb_ref[...], cg_ref[...], cbeta_ref[...])
    yc = yn * jax.nn.sigmoid(yn)

    cat = jnp.concatenate([yp, yc], axis=1).astype(BF16)
    m = jnp.dot(cat, wout_ref[...], preferred_element_type=F32)
    h_ref[0] = _layer_norm(DEEPNORM_ALPHA * x + m, g1_ref[...], b1_ref[...])
    pstate_ref[0] = ue_ref[tt:tt + POOL_HALO, :]
    cstate_ref[0] = ge_ref[tt:tt + CONV_HALO, :]


def _l0_prompt(x, win, pw, ps, cw, cb, cg, cbeta, wout, g1, b1):
    B, S, D = x.shape
    tt = min(L0_TIME_TILE, S)
    assert S % tt == 0 and tt >= CONV_HALO
    return pl.pallas_call(
        _l0_prompt_kernel,
        grid=(B, S // tt),
        in_specs=[
            pl.BlockSpec((1, tt, D), lambda b, t: (b, t, 0)),
            _const_spec(win.shape), _const_spec(pw.shape), _const_spec(ps.shape), _const_spec(cw.shape),
            _const_spec(cb.shape), _const_spec(cg.shape), _const_spec(cbeta.shape), _const_spec(wout.shape),
            _const_spec(g1.shape), _const_spec(b1.shape),
        ],
        out_specs=[
            pl.BlockSpec((1, tt, D), lambda b, t: (b, t, 0)),
            pl.BlockSpec((1, POOL_HALO, D_POOL), lambda b, t: (b, 0, 0)),
            pl.BlockSpec((1, CONV_HALO, D_CONV), lambda b, t: (b, 0, 0)),
        ],
        out_shape=[
            jax.ShapeDtypeStruct((B, S, D), F32),
            jax.ShapeDtypeStruct((B, POOL_HALO, D_POOL), F32),
            jax.ShapeDtypeStruct((B, CONV_HALO, D_CONV), F32),
        ],
        scratch_shapes=[pltpu.VMEM((tt + POOL_HALO, D_POOL), F32), pltpu.VMEM((tt + CONV_HALO, D_CONV), F32)],
        compiler_params=_cparams(("arbitrary", "arbitrary")),
        name="l0_prompt",
    )(x, win, pw, ps, cw, cb, cg, cbeta, wout, g1, b1)


def _l0_sample_kernel(x_ref, sp_ref, sc_ref, win_ref, pw_ref, ps_ref, cw_ref, cb_ref, cg_ref, cbeta_ref,
                      wout_ref, g1_ref, b1_ref, h_ref, u_ref, glu_ref):
    x = x_ref[...]
    proj = jnp.dot(x.astype(BF16), win_ref[...], preferred_element_type=F32)
    u = proj[:, :D_POOL]
    a = proj[:, D_POOL:D_POOL + D_CONV]
    gate = proj[:, D_POOL + D_CONV:]
    glu = a * jax.nn.sigmoid(gate)
    u_ref[...] = u
    glu_ref[...] = glu

    parts = []
    for g, w in enumerate(POOL_WINDOWS):
        c0 = g * POOL_GROUP
        ug = u[:, c0:c0 + POOL_GROUP]
        past = sp_ref[:, POOL_STATE - (w - 1):POOL_STATE, c0:c0 + POOL_GROUP]
        s = ug + jnp.sum(past, axis=1)
        pooled = s / float(w) - ug
        parts.append(jnp.dot(pooled.astype(BF16), pw_ref[g], preferred_element_type=F32))
    yp = jnp.concatenate(parts, axis=1) * ps_ref[...]

    acc = glu * cw_ref[CONV_STATE:CONV_STATE + 1, :]
    acc = acc + jnp.sum(sc_ref[...] * cw_ref[0:CONV_STATE, :][None, :, :], axis=1)
    yn = _layer_norm(acc + cb_ref[...], cg_ref[...], cbeta_ref[...])
    yc = yn * jax.nn.sigmoid(yn)

    cat = jnp.concatenate([yp, yc], axis=1).astype(BF16)
    m = jnp.dot(cat, wout_ref[...], preferred_element_type=F32)
    h_ref[...] = _layer_norm(DEEPNORM_ALPHA * x + m, g1_ref[...], b1_ref[...])


def _l0_sample(x, sp, sc, win, pw, ps, cw, cb, cg, cbeta, wout, g1, b1):
    DB, D = x.shape
    args = (x, sp, sc, win, pw, ps, cw, cb, cg, cbeta, wout, g1, b1)
    return pl.pallas_call(
        _l0_sample_kernel,
        grid=(1,),
        in_specs=[_const_spec(a.shape) for a in args],
        out_specs=[_const_spec((DB, D)), _const_spec((DB, D_POOL)), _const_spec((DB, D_CONV))],
        out_shape=[
            jax.ShapeDtypeStruct((DB, D), F32),
            jax.ShapeDtypeStruct((DB, D_POOL), F32),
            jax.ShapeDtypeStruct((DB, D_CONV), F32),
        ],
        compiler_params=_cparams(("arbitrary",)),
        name="l0_sample",
    )(*args)


def _first_index_of_max(x, iota, size):
    m = jnp.max(x, axis=0, keepdims=True)
    f = jnp.min(jnp.where(x == m, iota, size), axis=0, keepdims=True)
    return m, f


def _router_kernel(h_ref, wrt_ref, bias_ref, tri_ref, idx_ref, w_ref, rank_ref, cnt_ref, run_ref):
    tm = h_ref.shape[0]
    E = N_EXPERTS
    PG = EXPERTS_PER_GROUP

    @pl.when(pl.program_id(0) == 0)
    def _():
        run_ref[...] = jnp.zeros_like(run_ref)

    logits = lax.dot_general(wrt_ref[...], h_ref[...].astype(BF16), (((1,), (1,)), ((), ())),
                             preferred_element_type=F32)
    scores = jax.nn.sigmoid(logits)
    sel = scores + bias_ref[...]

    io_g = lax.broadcasted_iota(I32, (PG, tm), 0)
    rows = []
    for g in range(N_EXPERT_GROUPS):
        blk = sel[g * PG:(g + 1) * PG, :]
        m1, f1 = _first_index_of_max(blk, io_g, PG)
        m2 = jnp.max(jnp.where(io_g == f1, -jnp.inf, blk), axis=0, keepdims=True)
        rows.append(m1 + m2)
    gs = jnp.concatenate(rows, axis=0)

    io_n = lax.broadcasted_iota(I32, (N_EXPERT_GROUPS, tm), 0)
    gsel = jnp.zeros((N_EXPERT_GROUPS, tm), F32)
    cur = gs
    for _ in range(TOPK_GROUPS):
        _, f = _first_index_of_max(cur, io_n, N_EXPERT_GROUPS)
        hit = io_n == f
        gsel = jnp.where(hit, 1.0, gsel)
        cur = jnp.where(hit, -jnp.inf, cur)
    masked = jnp.concatenate(
        [jnp.where(gsel[g:g + 1, :] > 0.5, sel[g * PG:(g + 1) * PG, :], -jnp.inf) for g in range(N_EXPERT_GROUPS)],
        axis=0)

    io_e = lax.broadcasted_iota(I32, (E, tm), 0)
    onehot = jnp.zeros((E, tm), F32)
    idx_rows, sc_rows = [], []
    cur = masked
    for _ in range(TOP_K):
        _, f = _first_index_of_max(cur, io_e, E)
        hit = io_e == f
        idx_rows.append(f)
        sc_rows.append(jnp.sum(jnp.where(hit, scores, 0.0), axis=0, keepdims=True))
        onehot = jnp.where(hit, 1.0, onehot)
        cur = jnp.where(hit, -jnp.inf, cur)
    sc = jnp.concatenate(sc_rows, axis=0)
    idx_ref[0] = jnp.concatenate(idx_rows, axis=0)
    w_ref[...] = sc / jnp.sum(sc, axis=0, keepdims=True) * ROUTED_SCALE

    before = jnp.dot(onehot.astype(BF16), tri_ref[...], preferred_element_type=F32) + run_ref[...]
    rank_rows = [jnp.sum(jnp.where(io_e == f, before, 0.0), axis=0, keepdims=True) for f in idx_rows]
    rank_ref[0] = jnp.concatenate(rank_rows, axis=0).astype(I32)
    run_ref[...] = run_ref[...] + jnp.sum(onehot, axis=1, keepdims=True)
    cnt_ref[...] = run_ref[...]


def _router(tok, wrt, bias):
    N, D = tok.shape
    tm = ROUTER_TILE
    assert N % tm == 0
    tri = (jnp.arange(tm)[:, None] < jnp.arange(tm)[None, :]).astype(BF16)
    return pl.pallas_call(
        _router_kernel,
        grid=(N // tm,),
        in_specs=[
            pl.BlockSpec((tm, D), lambda i: (i, 0)),
            _const_spec(wrt.shape), _const_spec(bias.shape), _const_spec(tri.shape),
        ],
        out_specs=[
            pl.BlockSpec((1, TOP_K, tm), lambda i: (i, 0, 0)),
            pl.BlockSpec((TOP_K, tm), lambda i: (0, i)),
            pl.BlockSpec((1, TOP_K, tm), lambda i: (i, 0, 0)),
            _const_spec((N_EXPERTS, 1)),
        ],
        out_shape=[
            jax.ShapeDtypeStruct((N // tm, TOP_K, tm), I32),
            jax.ShapeDtypeStruct((TOP_K, N), F32),
            jax.ShapeDtypeStruct((N // tm, TOP_K, tm), I32),
            jax.ShapeDtypeStruct((N_EXPERTS, 1), F32),
        ],
        scratch_shapes=[pltpu.VMEM((N_EXPERTS, 1), F32)],
        compiler_params=_cparams(("arbitrary",)),
        name="moe_router",
    )(tok, wrt, bias, tri)


def _load_slot_tables(i, idx_hbm, rank_hbm, idx_s, rank_s, sem):
    copies = [pltpu.make_async_copy(idx_hbm.at[i], idx_s, sem), pltpu.make_async_copy(rank_hbm.at[i], rank_s, sem)]
    for c in copies:
        c.start()
    for c in copies:
        c.wait()


def _dispatch_kernel(idx_hbm, rank_hbm, pstart_ref, pend_ref, cnt_ref, tok_ref, xs_hbm, idx_s, rank_s, zero_ref,
                     sem_idx, sem_rows):
    i = pl.program_id(0)
    tm = tok_ref.shape[0]
    blk = zero_ref.shape[0]

    @pl.when(i == 0)
    def _():
        zero_ref[...] = jnp.zeros_like(zero_ref)

        def zero_copy(e):
            last_block = pl.multiple_of(pend_ref[e] - blk, blk)
            return pltpu.make_async_copy(zero_ref, xs_hbm.at[pl.ds(last_block, blk)], sem_rows)

        def start(e, c):
            @pl.when(cnt_ref[e] > 0)
            def _():
                zero_copy(e).start()
            return c

        def wait(e, c):
            @pl.when(cnt_ref[e] > 0)
            def _():
                zero_copy(e).wait()
            return c

        lax.fori_loop(0, N_EXPERTS, start, 0)
        lax.fori_loop(0, N_EXPERTS, wait, 0)

    _load_slot_tables(i, idx_hbm, rank_hbm, idx_s, rank_s, sem_idx)

    def row_copy(t, k):
        slot = pstart_ref[idx_s[k, t]] + rank_s[k, t]
        return pltpu.make_async_copy(tok_ref.at[t], xs_hbm.at[slot], sem_rows)

    def start(t, c):
        for k in range(TOP_K):
            row_copy(t, k).start()
        return c

    def wait(t, c):
        for k in range(TOP_K):
            pltpu.make_async_copy(tok_ref.at[0], xs_hbm.at[0], sem_rows).wait()
        return c

    lax.fori_loop(0, tm, start, 0)
    lax.fori_loop(0, tm, wait, 0)


def _dispatch(tok3, idx3, rank3, pstart, pend, counts, n_slots):
    N = tok3.shape[0]
    tm = idx3.shape[2]
    assert N % tm == 0 and idx3.shape == rank3.shape == (N // tm, TOP_K, tm)
    smem = pl.BlockSpec(memory_space=pltpu.SMEM)
    return pl.pallas_call(
        _dispatch_kernel,
        grid=(N // tm,),
        in_specs=[
            pl.BlockSpec(memory_space=pl.ANY), pl.BlockSpec(memory_space=pl.ANY), smem, smem, smem,
            pl.BlockSpec((tm, SUBLANES, LANES), lambda i: (i, 0, 0)),
        ],
        out_specs=pl.BlockSpec(memory_space=pl.ANY),
        scratch_shapes=[
            pltpu.SMEM((TOP_K, tm), I32),
            pltpu.SMEM((TOP_K, tm), I32),
            pltpu.VMEM((EXPERT_BLOCK, SUBLANES, LANES), F32),
            pltpu.SemaphoreType.DMA,
            pltpu.SemaphoreType.DMA,
        ],
        out_shape=jax.ShapeDtypeStruct((n_slots, SUBLANES, LANES), F32),
        compiler_params=_cparams(("arbitrary",)),
        name="moe_dispatch",
    )(idx3, rank3, pstart, pend, counts, tok3)


def _ffn_kernel(be_ref, nu_ref, xs_ref, wg_ref, wu_ref, wd_ref, ys_ref, wgu_s, wd_s):
    b = pl.program_id(0)
    prev = be_ref[jnp.maximum(b - 1, 0)]
    new_expert = jnp.logical_or(b == 0, be_ref[b] != prev)

    @pl.when(jnp.logical_and(new_expert, b < nu_ref[0]))
    def _():
        wgu_s[:, :D_EXPERT] = wg_ref[0, 0].astype(BF16)
        wgu_s[:, D_EXPERT:] = wu_ref[0, 0].astype(BF16)
        wd_s[...] = wd_ref[0, 0].astype(BF16)

    @pl.when(b < nu_ref[0])
    def _():
        blk = xs_ref.shape[0] // SUBLANES
        x = jnp.concatenate([xs_ref[pl.ds(c, blk, stride=SUBLANES), :] for c in range(SUBLANES)], axis=1)
        gu = jnp.dot(x.astype(BF16), wgu_s[...], preferred_element_type=F32)
        gt = gu[:, :D_EXPERT]
        hid = gt * jax.nn.sigmoid(gt) * gu[:, D_EXPERT:]
        y = jnp.dot(hid.astype(BF16), wd_s[...], preferred_element_type=F32)
        for c in range(SUBLANES):
            ys_ref[pl.ds(c, blk, stride=SUBLANES), :] = y[:, c * LANES:(c + 1) * LANES]


def _expert_ffn(xs, block_e, n_used, layer, wg, wu, wd):
    n_slots = xs.shape[0]
    D = SUBLANES * LANES
    blk = EXPERT_BLOCK
    n_blocks = n_slots // blk

    def row_map(b, be, nu):
        return (jnp.minimum(b, nu[0] - 1), 0)

    def w_map(b, be, nu):
        return (layer, be[b], 0, 0)

    return pl.pallas_call(
        _ffn_kernel,
        grid_spec=pltpu.PrefetchScalarGridSpec(
            num_scalar_prefetch=2,
            grid=(n_blocks,),
            in_specs=[
                pl.BlockSpec((blk * SUBLANES, LANES), row_map),
                pl.BlockSpec((1, 1, D, D_EXPERT), w_map),
                pl.BlockSpec((1, 1, D, D_EXPERT), w_map),
                pl.BlockSpec((1, 1, D_EXPERT, D), w_map),
            ],
            out_specs=pl.BlockSpec((blk * SUBLANES, LANES), row_map),
            scratch_shapes=[pltpu.VMEM((D, 2 * D_EXPERT), BF16), pltpu.VMEM((D_EXPERT, D), BF16)],
        ),
        out_shape=jax.ShapeDtypeStruct((n_slots * SUBLANES, LANES), F32),
        compiler_params=_cparams(("arbitrary",)),
        name="moe_ffn",
    )(block_e, n_used, xs.reshape(n_slots * SUBLANES, LANES), wg, wu, wd).reshape(n_slots, SUBLANES, LANES)


def _combine_kernel(idx_hbm, rank_hbm, pstart_ref, ys_hbm, tok_ref, w_ref, wsg_ref, wsu_ref, wsd_ref, g_ref, b_ref,
                    out_ref, idx_s, rank_s, buf_ref, sem_idx, sem_rows):
    i = pl.program_id(0)
    tm = tok_ref.shape[0]

    _load_slot_tables(i, idx_hbm, rank_hbm, idx_s, rank_s, sem_idx)

    def row_copy(t, k):
        slot = pstart_ref[idx_s[k, t]] + rank_s[k, t]
        row = pl.multiple_of(t * SUBLANES, SUBLANES)
        return pltpu.make_async_copy(ys_hbm.at[slot], buf_ref.at[k, pl.ds(row, SUBLANES)], sem_rows)

    def start(t, c):
        for k in range(TOP_K):
            row_copy(t, k).start()
        return c

    def wait(t, c):
        for k in range(TOP_K):
            pltpu.make_async_copy(ys_hbm.at[0], buf_ref.at[0, pl.ds(0, SUBLANES)], sem_rows).wait()
        return c

    lax.fori_loop(0, tm, start, 0)

    h = tok_ref[...]
    hb = h.astype(BF16)
    gt = jnp.dot(hb, wsg_ref[...], preferred_element_type=F32)
    up = jnp.dot(hb, wsu_ref[...], preferred_element_type=F32)
    hid = gt * jax.nn.sigmoid(gt) * up
    f = jnp.dot(hid.astype(BF16), wsd_ref[...], preferred_element_type=F32)

    lax.fori_loop(0, tm, wait, 0)
    w = w_ref[...]
    wb = [jnp.broadcast_to(w[:, k:k + 1], (tm, LANES)) for k in range(TOP_K)]
    chunks = []
    for c in range(SUBLANES):
        acc = f[:, c * LANES:(c + 1) * LANES]
        for k in range(TOP_K):
            acc = acc + buf_ref[k, pl.ds(c, tm, stride=SUBLANES), :] * wb[k]
        chunks.append(acc)
    f = jnp.concatenate(chunks, axis=1)
    out_ref[...] = _layer_norm(DEEPNORM_ALPHA * h + f, g_ref[...], b_ref[...])


def _combine(tok, ys, idx3, rank3, pstart, w_tok, wsg, wsu, wsd, g, b):
    N, D = tok.shape
    tm = idx3.shape[2]
    assert N % tm == 0 and idx3.shape == rank3.shape == (N // tm, TOP_K, tm)
    return pl.pallas_call(
        _combine_kernel,
        grid=(N // tm,),
        in_specs=[
            pl.BlockSpec(memory_space=pl.ANY), pl.BlockSpec(memory_space=pl.ANY),
            pl.BlockSpec(memory_space=pltpu.SMEM),
            pl.BlockSpec(memory_space=pl.ANY),
            pl.BlockSpec((tm, D), lambda i: (i, 0)),
            pl.BlockSpec((tm, TOP_K), lambda i: (i, 0)),
            _const_spec(wsg.shape), _const_spec(wsu.shape), _const_spec(wsd.shape),
            _const_spec(g.shape), _const_spec(b.shape),
        ],
        out_specs=pl.BlockSpec((tm, D), lambda i: (i, 0)),
        scratch_shapes=[
            pltpu.SMEM((TOP_K, tm), I32),
            pltpu.SMEM((TOP_K, tm), I32),
            pltpu.VMEM((TOP_K, tm * SUBLANES, LANES), F32),
            pltpu.SemaphoreType.DMA,
            pltpu.SemaphoreType.DMA,
        ],
        out_shape=jax.ShapeDtypeStruct((N, D), F32),
        compiler_params=_cparams(("arbitrary",)),
        name="moe_combine",
    )(idx3, rank3, pstart, ys, tok, w_tok, wsg, wsu, wsd, g, b)


def _moe_layer(tok, layer, w_router, router_bias, we_gate, we_up, we_down, ws_gate, ws_up, ws_down, ln_g, ln_b):
    N, D = tok.shape
    blk = EXPERT_BLOCK
    idx3, w_t, rank3, cnt = _router(tok, w_router.T.astype(BF16), router_bias.reshape(N_EXPERTS, 1))

    counts = cnt[:, 0].astype(I32)
    pcounts = (counts + blk - 1) // blk * blk
    pend = jnp.cumsum(pcounts).astype(I32)
    pstart = pend - pcounts
    n_blocks = N * TOP_K // blk + N_EXPERTS
    block_start = jnp.arange(n_blocks, dtype=I32) * blk
    block_e = jnp.minimum(jnp.sum(pend[None, :] <= block_start[:, None], axis=1), N_EXPERTS - 1).astype(I32)
    n_used = (pend[-1] // blk).reshape(1)

    xs = _dispatch(tok.reshape(N, SUBLANES, LANES), idx3, rank3, pstart, pend, counts, n_blocks * blk)
    ys = _expert_ffn(xs, block_e, n_used, layer, we_gate, we_up, we_down)
    return _combine(tok, ys, idx3, rank3, pstart, w_t.T,
                    ws_gate.astype(BF16), ws_up.astype(BF16), ws_down.astype(BF16),
                    ln_g.reshape(1, D), ln_b.reshape(1, D))


def _rel_bucket(dist):
    exact = N_BUCKETS // 2
    df = jnp.maximum(dist, 1).astype(F32)
    large = exact + (jnp.log(df / exact) / math.log(MAX_DISTANCE / exact) * (N_BUCKETS - exact)).astype(I32)
    large = jnp.minimum(large, N_BUCKETS - 1)
    return jnp.where(dist < exact, dist, large)


def _qkv_kernel(x_ref, w_ref, wkvt_ref, q_ref, k_ref, v_ref, kt_ref, vt_ref, *, first_kept_tile, kept_cols):
    tl = x_ref.shape[1]
    x = x_ref[0].astype(BF16)
    y = jnp.dot(x, w_ref[...], preferred_element_type=F32)
    q_ref[0] = y[:, :D_ATTN].astype(BF16)
    k_ref[0] = y[:, D_ATTN:2 * D_ATTN].astype(BF16)
    v_ref[0] = y[:, 2 * D_ATTN:].astype(BF16)

    @pl.when(pl.program_id(1) >= first_kept_tile)
    def _():
        kvt = lax.dot_general(wkvt_ref[...], x[tl - kept_cols:, :], (((1,), (1,)), ((), ())),
                              preferred_element_type=F32)
        kt_ref[0] = kvt[:D_ATTN]
        vt_ref[0] = kvt[D_ATTN:]


def _qkv_prompt(h, w_g, keep):
    B, S, D = h.shape
    tl = min(S, 512)
    kept_cols = min(tl, keep)
    assert S % tl == 0 and keep % kept_cols == 0 and (S - keep) % kept_cols == 0
    first_kept_tile = (S - keep) // tl
    row_spec = pl.BlockSpec((1, tl, D_ATTN), lambda b, t: (b, t, 0))
    state_spec = pl.BlockSpec((1, D_ATTN, kept_cols), lambda b, t: (b, 0, jnp.maximum(t - first_kept_tile, 0)))
    wkvt = w_g[:, D_ATTN:].T
    return pl.pallas_call(
        functools.partial(_qkv_kernel, first_kept_tile=first_kept_tile, kept_cols=kept_cols),
        grid=(B, S // tl),
        in_specs=[pl.BlockSpec((1, tl, D), lambda b, t: (b, t, 0)), _const_spec(w_g.shape), _const_spec(wkvt.shape)],
        out_specs=[row_spec, row_spec, row_spec, state_spec, state_spec],
        out_shape=[jax.ShapeDtypeStruct((B, S, D_ATTN), BF16)] * 3
        + [jax.ShapeDtypeStruct((B, D_ATTN, keep), F32)] * 2,
        compiler_params=_cparams(("arbitrary", "arbitrary")),
        name="qkv_prompt",
    )(h, w_g, wkvt)


def _attn_prompt_kernel(q_ref, kp_ref, kc_ref, vp_ref, vc_ref, bias_ref, o_ref, lse_ref, *, steps):
    n = pl.program_id(1)
    qb = QUERY_BLOCK
    qi = lax.broadcasted_iota(I32, (qb, 2 * qb), 0)
    kj = lax.broadcasted_iota(I32, (qb, 2 * qb), 1)
    dist = qi + qb - kj
    valid = (dist >= 0) & (dist <= steps) & ((n > 0) | (kj >= qb))
    low = lax.broadcasted_iota(I32, (1, 2 * HEAD_DIM), 1) < HEAD_DIM
    scale = HEAD_DIM ** -0.5
    for p in range(HEADS // 2):
        c0 = p * 2 * HEAD_DIM
        q2 = q_ref[0, :, c0:c0 + 2 * HEAD_DIM]
        k2 = jnp.concatenate([kp_ref[0, :, c0:c0 + 2 * HEAD_DIM], kc_ref[0, :, c0:c0 + 2 * HEAD_DIM]], axis=0)
        v2 = jnp.concatenate([vp_ref[0, :, c0:c0 + 2 * HEAD_DIM], vc_ref[0, :, c0:c0 + 2 * HEAD_DIM]], axis=0)
        outs, lses = [], []
        for half in range(2):
            keep = low if half == 0 else jnp.logical_not(low)
            qh = jnp.where(keep, q2, jnp.zeros_like(q2))
            s = lax.dot_general(qh, k2, (((1,), (1,)), ((), ())), preferred_element_type=F32)
            s = s * scale + bias_ref[2 * p + half]
            s = jnp.where(valid, s, NEG_INF)
            m = jnp.max(s, axis=-1, keepdims=True)
            e = jnp.exp(s - m)
            l = jnp.sum(e, axis=-1, keepdims=True)
            prob = e / l
            outs.append(jnp.dot(prob.astype(BF16), v2, preferred_element_type=F32))
            lses.append(m + jnp.log(l))
        o_ref[0, :, c0:c0 + 2 * HEAD_DIM] = jnp.where(low, outs[0], outs[1])
        lse_ref[0, :, c0:c0 + 2 * HEAD_DIM] = jnp.where(low, lses[0], lses[1])


def _attn_prompt(q, k, v, bias, B, dil, steps):
    S = q.shape[1]
    M = B * S
    L = S // dil
    qb = QUERY_BLOCK
    assert S % dil == 0 and L % qb == 0
    cur = pl.BlockSpec((1, qb, D_ATTN), lambda bd, n: (bd // dil, n, bd % dil))
    prev = pl.BlockSpec((1, qb, D_ATTN), lambda bd, n: (bd // dil, jnp.maximum(n - 1, 0), bd % dil))
    qv, kv, vv = (a.reshape(B, L, dil * D_ATTN) for a in (q, k, v))
    o, lse = pl.pallas_call(
        functools.partial(_attn_prompt_kernel, steps=steps),
        grid=(B * dil, L // qb),
        in_specs=[cur, prev, cur, prev, cur, _const_spec(bias.shape)],
        out_specs=[cur, cur],
        out_shape=[jax.ShapeDtypeStruct((B, L, dil * D_ATTN), F32)] * 2,
        compiler_params=_cparams(("arbitrary", "arbitrary")),
        name="attn_prompt",
    )(qv, kv, kv, vv, vv, bias)
    return o.reshape(M, D_ATTN), lse.reshape(M, D_ATTN)


def _merge_kernel(o1, o2, o3, l1, l2, l3, h_ref, wo_ref, g_ref, b_ref, out_ref):
    a1, a2, a3 = l1[...], l2[...], l3[...]
    m = jnp.maximum(jnp.maximum(a1, a2), a3)
    e1, e2, e3 = jnp.exp(a1 - m), jnp.exp(a2 - m), jnp.exp(a3 - m)
    o = (e1 * o1[...] + e2 * o2[...] + e3 * o3[...]) / (e1 + e2 + e3)
    y = jnp.dot(o.astype(BF16), wo_ref[...], preferred_element_type=F32)
    out_ref[...] = _layer_norm(DEEPNORM_ALPHA * h_ref[...] + y, g_ref[...], b_ref[...])


def _merge(os_, ls_, h, wo, g, b):
    M, D = h.shape
    tm = min(MERGE_TILE, M)
    assert M % tm == 0
    a_spec = pl.BlockSpec((tm, D_ATTN), lambda i: (i, 0))
    return pl.pallas_call(
        _merge_kernel,
        grid=(M // tm,),
        in_specs=[a_spec] * 6 + [pl.BlockSpec((tm, D), lambda i: (i, 0)), _const_spec(wo.shape),
                                 _const_spec(g.shape), _const_spec(b.shape)],
        out_specs=pl.BlockSpec((tm, D), lambda i: (i, 0)),
        out_shape=jax.ShapeDtypeStruct((M, D), F32),
        compiler_params=_cparams(("arbitrary",)),
        name="attn_merge",
    )(*os_, *ls_, h, wo, g, b)


def _mm_kernel(x_ref, w_ref, o_ref):
    o_ref[...] = jnp.dot(x_ref[...].astype(BF16), w_ref[...], preferred_element_type=F32)


def _matmul(x, w, tn):
    M, K = x.shape
    _, N = w.shape
    assert N % tn == 0
    return pl.pallas_call(
        _mm_kernel,
        grid=(N // tn,),
        in_specs=[_const_spec((M, K)), pl.BlockSpec((K, tn), lambda j: (0, j))],
        out_specs=pl.BlockSpec((M, tn), lambda j: (0, j)),
        out_shape=jax.ShapeDtypeStruct((M, N), F32),
        compiler_params=_cparams(("arbitrary",)),
        name="matmul",
    )(x, w)


def _round_bf16(x):
    return x.astype(BF16).astype(F32)


def _attn_sample_kernel(q_ref, kn_ref, vn_ref, kc_ref, vc_ref, bias_ref, b0_ref, o_ref, lse_ref):
    scale = HEAD_DIM ** -0.5
    q = _round_bf16(q_ref[...])
    kn = _round_bf16(kn_ref[...])
    vn = _round_bf16(vn_ref[...])
    kc = _round_bf16(kc_ref[...])
    s = jnp.sum(kc * q, axis=2, keepdims=True) * scale + bias_ref[...][None]
    s0 = jnp.sum(q * kn, axis=2, keepdims=True) * scale + b0_ref[...][None]
    m = jnp.maximum(jnp.max(s, axis=3, keepdims=True), s0)
    e = jnp.exp(s - m)
    e0 = jnp.exp(s0 - m)
    l = jnp.sum(e, axis=3, keepdims=True) + e0
    p = _round_bf16(e / l)
    p0 = _round_bf16(e0 / l)
    vc = _round_bf16(vc_ref[...])
    o_ref[...] = jnp.sum(vc * p, axis=3, keepdims=True) + p0 * vn
    lse_ref[...] = m + jnp.log(l)


def _attn_sample(q, kn, vn, ckt, cvt, j, bias, bias_self, tb=1):
    DB = q.shape[0]
    Wb = ckt.shape[3]
    assert DB % tb == 0
    new_spec = pl.BlockSpec((tb, HEADS, HEAD_DIM, 1), lambda i: (i, 0, 0, 0))
    cache_spec = pl.BlockSpec((tb, HEADS, HEAD_DIM, Wb), lambda i: (j * (DB // tb) + i, 0, 0, 0))
    return pl.pallas_call(
        _attn_sample_kernel,
        grid=(DB // tb,),
        in_specs=[new_spec, new_spec, new_spec, cache_spec, cache_spec,
                  _const_spec(bias.shape), _const_spec(bias_self.shape)],
        out_specs=[new_spec, pl.BlockSpec((tb, HEADS, 1, 1), lambda i: (i, 0, 0, 0))],
        out_shape=[jax.ShapeDtypeStruct((DB, HEADS, HEAD_DIM, 1), F32), jax.ShapeDtypeStruct((DB, HEADS, 1, 1), F32)],
        compiler_params=_cparams(("arbitrary",)),
        name="attn_sample",
    )(q, kn, vn, ckt, cvt, bias, bias_self)


def _bias_lookup(tab, buckets):
    onehot = (buckets[..., None] == jnp.arange(N_BUCKETS, dtype=I32)).astype(F32)
    return jnp.einsum('...n,nh->...h', onehot, tab, precision=lax.Precision.HIGHEST)


def _dilated_layer(hp, hs, caches, j, w_qkv, w_o, rel_bias, g1, b1):
    B, S, D = hp.shape
    DB = hs.shape[0]
    w_qkv_b = w_qkv.astype(BF16)
    w_o_b = w_o.astype(BF16)
    qb = QUERY_BLOCK
    qi = jnp.arange(qb, dtype=I32)[:, None]
    kj = jnp.arange(2 * qb, dtype=I32)[None, :]
    dist = qi + qb - kj
    qkv_s = _matmul(hs, w_qkv_b, 512)

    def sample_cols(part, g):
        c0 = (part * N_GROUPS + g) * D_ATTN
        return qkv_s[:, c0:c0 + D_ATTN].reshape(DB, HEADS, HEAD_DIM)

    def position_minor(c):
        return jnp.transpose(c, (0, 1, 3, 4, 2)).reshape(c.shape[0] * DB, HEADS, HEAD_DIM, c.shape[2])

    os_, ls_, kp, vp = [], [], [], []
    os_s, ls_s, ks, vs = [], [], [], []
    for g, (window, dil) in enumerate(DIL_CONFIGS):
        steps = window // dil
        assert steps == qb
        tab = rel_bias[:, g * HEADS:(g + 1) * HEADS]
        cols = [w_qkv_b[:, (part * N_GROUPS + g) * D_ATTN:(part * N_GROUPS + g + 1) * D_ATTN] for part in range(3)]
        keep = min(window, S)
        q, k, v, kt, vt = _qkv_prompt(hp, jnp.concatenate(cols, axis=1), keep)
        bias = _bias_lookup(tab, _rel_bucket(jnp.maximum(dist, 0) * dil)).transpose(2, 0, 1)
        o, lse = _attn_prompt(q, k, v, bias, B, dil, steps)
        os_.append(o)
        ls_.append(lse)
        kp.append(jnp.transpose(kt.reshape(B, HEADS, HEAD_DIM, keep), (0, 3, 1, 2)))
        vp.append(jnp.transpose(vt.reshape(B, HEADS, HEAD_DIM, keep), (0, 3, 1, 2)))
        ck, cv = caches[g]
        wb = ck.shape[2]
        assert wb == steps * dil and ck.shape[1] == DB
        pos = jnp.arange(wb, dtype=I32)
        cache_bias = jnp.where((pos % dil == 0)[:, None], _bias_lookup(tab, _rel_bucket(wb - pos)), NEG_INF)
        self_bias = _bias_lookup(tab, _rel_bucket(jnp.zeros((1,), I32)))
        q_s, k_s, v_s = sample_cols(0, g), sample_cols(1, g), sample_cols(2, g)
        o_s, lse_s = _attn_sample(q_s[..., None], k_s[..., None], v_s[..., None], position_minor(ck),
                                  position_minor(cv), j, cache_bias.T[:, None, :], self_bias.T[:, :, None],
                                  tb=max(1, 1024 // wb))
        os_s.append(o_s.reshape(DB, D_ATTN))
        ls_s.append(jnp.broadcast_to(lse_s.reshape(DB, HEADS, 1), (DB, HEADS, HEAD_DIM)).reshape(DB, D_ATTN))
        ks.append(k_s[:, None])
        vs.append(v_s[:, None])
    h_p = _merge(os_, ls_, hp.reshape(B * S, D), w_o_b, g1, b1).reshape(B, S, D)
    h_s = _merge(os_s, ls_s, hs, w_o_b, g1, b1)
    return h_p, h_s, kp, vp, ks, vs


def _pool_conv_layer(hp, hs, state_pool, state_conv, w_in, pool_w, pool_scale, conv_w, conv_b, ln_g, ln_b, w_out,
                     g1, b1):
    params = (w_in.astype(BF16), pool_w.astype(BF16), pool_scale.reshape(1, D_POOL), conv_w,
              conv_b.reshape(1, D_CONV), ln_g.reshape(1, D_CONV), ln_b.reshape(1, D_CONV), w_out.astype(BF16), g1, b1)
    h_p, pst, cst = _l0_prompt(hp, *params)
    h_s, u_s, glu_s = _l0_sample(hs, state_pool, state_conv, *params)
    pool_p = pst[:, POOL_HALO - POOL_STATE:]
    conv_p = cst[:, CONV_HALO - CONV_STATE:]
    pool_s = jnp.concatenate([state_pool[:, 1:], u_s[:, None, :]], axis=1)
    conv_s = jnp.concatenate([state_conv[:, 1:], glu_s[:, None, :]], axis=1)
    return h_p, h_s, pool_p, conv_p, pool_s, conv_s


def kernel(x_prompt, x_sample, state_pool, state_conv, cache_k1, cache_v1, cache_k2, cache_v2, cache_k3, cache_v3,
           w_in_ab, pool_w, pool_scale, conv_w, conv_b, conv_ln_g, conv_ln_b, w_out_ab, w_qkv, w_o, rel_bias,
           ln1_g, ln1_b, ln2_g, ln2_b, w_router, router_bias, we_gate, we_up, we_down, ws_gate, ws_up, ws_down):
    B, S, D = x_prompt.shape
    DB, T, _ = x_sample.shape
    assert T == 1 and D == D_MODEL
    hp = x_prompt
    hs = x_sample.reshape(DB, D)
    caches_k = (cache_k1, cache_k2, cache_k3)
    caches_v = (cache_v1, cache_v2, cache_v3)
    pool_p, conv_p, pool_s, conv_s = [], [], [], []
    kp = [[] for _ in range(N_GROUPS)]
    vp = [[] for _ in range(N_GROUPS)]
    ksm = [[] for _ in range(N_GROUPS)]
    vsm = [[] for _ in range(N_GROUPS)]
    for layer in range(DEPTH):
        j = layer // 2
        g1 = ln1_g[layer].reshape(1, D)
        b1 = ln1_b[layer].reshape(1, D)
        if layer % 2 == 0:
            hp, hs, pp, cp, ps, cs = _pool_conv_layer(
                hp, hs, state_pool[j], state_conv[j], w_in_ab[j], pool_w[j], pool_scale[j], conv_w[j], conv_b[j],
                conv_ln_g[j], conv_ln_b[j], w_out_ab[j], g1, b1)
            pool_p.append(pp)
            conv_p.append(cp)
            pool_s.append(ps)
            conv_s.append(cs)
        else:
            caches = [(caches_k[g], caches_v[g]) for g in range(N_GROUPS)]
            hp, hs, nkp, nvp, nks, nvs = _dilated_layer(hp, hs, caches, j, w_qkv[j], w_o[j], rel_bias, g1, b1)
            for g in range(N_GROUPS):
                kp[g].append(nkp[g])
                vp[g].append(nvp[g])
                ksm[g].append(nks[g])
                vsm[g].append(nvs[g])
        tok = jnp.concatenate([hp.reshape(B * S, D), hs], axis=0)
        tok = _moe_layer(tok, layer, w_router[layer], router_bias[layer], we_gate, we_up, we_down,
                         ws_gate[layer], ws_up[layer], ws_down[layer], ln2_g[layer], ln2_b[layer])
        hp = tok[:B * S].reshape(B, S, D)
        hs = tok[B * S:]
    return (hp, hs.reshape(DB, T, D),
            jnp.stack(pool_p), jnp.stack(conv_p),
            jnp.stack(kp[0]), jnp.stack(vp[0]), jnp.stack(kp[1]), jnp.stack(vp[1]), jnp.stack(kp[2]), jnp.stack(vp[2]),
            jnp.stack(pool_s), jnp.stack(conv_s),
            jnp.stack(ksm[0]), jnp.stack(vsm[0]), jnp.stack(ksm[1]), jnp.stack(vsm[1]),
            jnp.stack(ksm[2]), jnp.stack(vsm[2]))
```

```python
import functools
import math

import jax
import jax.numpy as jnp
from jax import lax
from jax.experimental import pallas as pl
from jax.experimental.pallas import tpu as pltpu

F32 = jnp.float32
BF16 = jnp.bfloat16
I32 = jnp.int32

D_MODEL = 1024
D_POOL = 512
D_CONV = 512
POOL_WINDOWS = (2, 4, 8, 16)
POOL_GROUP = 128
POOL_STATE = 15
CONV_WIDTH = 31
CONV_STATE = 30
DIL_CONFIGS = ((128, 1), (512, 4), (2048, 16))
N_GROUPS = 3
HEADS = 8
HEAD_DIM = 64
D_ATTN = HEADS * HEAD_DIM
QUERY_BLOCK = 128
N_BUCKETS = 32
MAX_DISTANCE = 2048
N_EXPERTS = 256
TOP_K = 8
N_EXPERT_GROUPS = 8
EXPERTS_PER_GROUP = N_EXPERTS // N_EXPERT_GROUPS
TOPK_GROUPS = 4
D_EXPERT = 256
ROUTED_SCALE = 2.5
DEPTH = 2
DEEPNORM_ALPHA = (2.0 * DEPTH) ** 0.25
LN_EPS = 1e-5
NEG_INF = -1e30

LANES = 128
SUBLANES = 8
VMEM_LIMIT_BYTES = 56 * 1024 * 1024

POOL_HALO = 16
CONV_HALO = 32
L0_TIME_TILE = 256
ROUTER_TILE = 384
EXPERT_BLOCK = 256
MERGE_TILE = 512


def _cparams(sem):
    return pltpu.CompilerParams(dimension_semantics=sem, vmem_limit_bytes=VMEM_LIMIT_BYTES)


def _layer_norm(x, g, b):
    mu = jnp.mean(x, axis=-1, keepdims=True)
    xc = x - mu
    var = jnp.mean(xc * xc, axis=-1, keepdims=True)
    return xc * lax.rsqrt(var + LN_EPS) * g + b


def _const_spec(shape):
    nd = len(shape)
    return pl.BlockSpec(shape, lambda *_: (0,) * nd)


def _l0_prompt_kernel(x_ref, win_ref, pw_ref, ps_ref, cw_ref, cb_ref, cg_ref, cbeta_ref, wout_ref,
                      g1_ref, b1_ref, h_ref, pstate_ref, cstate_ref, ue_ref, ge_ref):
    tt = x_ref.shape[1]
    t = pl.program_id(1)

    @pl.when(t == 0)
    def _():
        ue_ref[0:POOL_HALO, :] = jnp.zeros((POOL_HALO, D_POOL), F32)
        ge_ref[0:CONV_HALO, :] = jnp.zeros((CONV_HALO, D_CONV), F32)

    @pl.when(t > 0)
    def _():
        ue_ref[0:POOL_HALO, :] = ue_ref[tt:tt + POOL_HALO, :]
        ge_ref[0:CONV_HALO, :] = ge_ref[tt:tt + CONV_HALO, :]

    x = x_ref[0]
    proj = jnp.dot(x.astype(BF16), win_ref[...], preferred_element_type=F32)
    u = proj[:, :D_POOL]
    a = proj[:, D_POOL:D_POOL + D_CONV]
    gate = proj[:, D_POOL + D_CONV:]
    glu = a * jax.nn.sigmoid(gate)
    ue_ref[POOL_HALO:POOL_HALO + tt, :] = u
    ge_ref[CONV_HALO:CONV_HALO + tt, :] = glu

    tg = t * tt + lax.broadcasted_iota(I32, (tt, 1), 0)
    parts = []
    for g, w in enumerate(POOL_WINDOWS):
        c0 = g * POOL_GROUP
        ug = u[:, c0:c0 + POOL_GROUP]
        s = ug
        for j in range(1, w):
            s = s + ue_ref[POOL_HALO - j:POOL_HALO - j + tt, c0:c0 + POOL_GROUP]
        cnt = jnp.minimum(tg + 1, w).astype(F32)
        pooled = s / cnt - ug
        parts.append(jnp.dot(pooled.astype(BF16), pw_ref[g], preferred_element_type=F32))
    yp = jnp.concatenate(parts, axis=1) * ps_ref[...]

    acc = glu * cw_ref[CONV_STATE:CONV_STATE + 1, :]
    off = CONV_HALO - CONV_STATE
    for j in range(CONV_STATE):
        acc = acc + ge_ref[off + j:off + j + tt, :] * cw_ref[j:j + 1, :]
    yn = _layer_norm(acc + cb_ref[...], cg_ref[...], cbeta_ref[...])
    yc = yn * jax.nn.sigmoid(yn)

    cat = jnp.concatenate([yp, yc], axis=1).astype(BF16)
    m = jnp.dot(cat, wout_ref[...], preferred_element_type=F32)
    h_ref[0] = _layer_norm(DEEPNORM_ALPHA * x + m, g1_ref[...], b1_ref[...])
    pstate_ref[0] = ue_ref[tt:tt + POOL_HALO, :]
    cstate_ref[0] = ge_ref[tt:tt + CONV_HALO, :]


def _l0_prompt(x, win, pw, ps, cw, cb, cg, cbeta, wout, g1, b1):
    B, S, D = x.shape
    tt = min(L0_TIME_TILE, S)
    assert S % tt == 0 and tt >= CONV_HALO
    return pl.pallas_call(
        _l0_prompt_kernel,
        grid=(B, S // tt),
        in_specs=[
            pl.BlockSpec((1, tt, D), lambda b, t: (b, t, 0)),
            _const_spec(win.shape), _const_spec(pw.shape), _const_spec(ps.shape), _const_spec(cw.shape),
            _const_spec(cb.shape), _const_spec(cg.shape), _const_spec(cbeta.shape), _const_spec(wout.shape),
            _const_spec(g1.shape), _const_spec(b1.shape),
        ],
        out_specs=[
            pl.BlockSpec((1, tt, D), lambda b, t: (b, t, 0)),
            pl.BlockSpec((1, POOL_HALO, D_POOL), lambda b, t: (b, 0, 0)),
            pl.BlockSpec((1, CONV_HALO, D_CONV), lambda b, t: (b, 0, 0)),
        ],
        out_shape=[
            jax.ShapeDtypeStruct((B, S, D), F32),
            jax.ShapeDtypeStruct((B, POOL_HALO, D_POOL), F32),
            jax.ShapeDtypeStruct((B, CONV_HALO, D_CONV), F32),
        ],
        scratch_shapes=[pltpu.VMEM((tt + POOL_HALO, D_POOL), F32), pltpu.VMEM((tt + CONV_HALO, D_CONV), F32)],
        compiler_params=_cparams(("arbitrary", "arbitrary")),
        name="l0_prompt",
    )(x, win, pw, ps, cw, cb, cg, cbeta, wout, g1, b1)


def _l0_sample_kernel(x_ref, sp_ref, sc_ref, win_ref, pw_ref, ps_ref, cw_ref, cb_ref, cg_ref, cbeta_ref,
                      wout_ref, g1_ref, b1_ref, h_ref, u_ref, glu_ref):
    x = x_ref[...]
    proj = jnp.dot(x.astype(BF16), win_ref[...], preferred_element_type=F32)
    u = proj[:, :D_POOL]
    a = proj[:, D_POOL:D_POOL + D_CONV]
    gate = proj[:, D_POOL + D_CONV:]
    glu = a * jax.nn.sigmoid(gate)
    u_ref[...] = u
    glu_ref[...] = glu

    parts = []
    for g, w in enumerate(POOL_WINDOWS):
        c0 = g * POOL_GROUP
        ug = u[:, c0:c0 + POOL_GROUP]
        past = sp_ref[:, POOL_STATE - (w - 1):POOL_STATE, c0:c0 + POOL_GROUP]
        s = ug + jnp.sum(past, axis=1)
        pooled = s / float(w) - ug
        parts.append(jnp.dot(pooled.astype(BF16), pw_ref[g], preferred_element_type=F32))
    yp = jnp.concatenate(parts, axis=1) * ps_ref[...]

    acc = glu * cw_ref[CONV_STATE:CONV_STATE + 1, :]
    acc = acc + jnp.sum(sc_ref[...] * cw_ref[0:CONV_STATE, :][None, :, :], axis=1)
    yn = _layer_norm(acc + cb_ref[...], cg_ref[...], cbeta_ref[...])
    yc = yn * jax.nn.sigmoid(yn)

    cat = jnp.concatenate([yp, yc], axis=1).astype(BF16)
    m = jnp.dot(cat, wout_ref[...], preferred_element_type=F32)
    h_ref[...] = _layer_norm(DEEPNORM_ALPHA * x + m, g1_ref[...], b1_ref[...])


def _l0_sample(x, sp, sc, win, pw, ps, cw, cb, cg, cbeta, wout, g1, b1):
    DB, D = x.shape
    args = (x, sp, sc, win, pw, ps, cw, cb, cg, cbeta, wout, g1, b1)
    return pl.pallas_call(
        _l0_sample_kernel,
        grid=(1,),
        in_specs=[_const_spec(a.shape) for a in args],
        out_specs=[_const_spec((DB, D)), _const_spec((DB, D_POOL)), _const_spec((DB, D_CONV))],
        out_shape=[
            jax.ShapeDtypeStruct((DB, D), F32),
            jax.ShapeDtypeStruct((DB, D_POOL), F32),
            jax.ShapeDtypeStruct((DB, D_CONV), F32),
        ],
        compiler_params=_cparams(("arbitrary",)),
        name="l0_sample",
    )(*args)


def _first_index_of_max(x, iota, size):
    m = jnp.max(x, axis=0, keepdims=True)
    f = jnp.min(jnp.where(x == m, iota, size), axis=0, keepdims=True)
    return m, f


def _router_kernel(h_ref, wrt_ref, bias_ref, tri_ref, idx_ref, w_ref, rank_ref, cnt_ref, run_ref):
    tm = h_ref.shape[0]
    E = N_EXPERTS
    PG = EXPERTS_PER_GROUP

    @pl.when(pl.program_id(0) == 0)
    def _():
        run_ref[...] = jnp.zeros_like(run_ref)

    logits = lax.dot_general(wrt_ref[...], h_ref[...].astype(BF16), (((1,), (1,)), ((), ())),
                             preferred_element_type=F32)
    scores = jax.nn.sigmoid(logits)
    sel = scores + bias_ref[...]

    io_g = lax.broadcasted_iota(I32, (PG, tm), 0)
    rows = []
    for g in range(N_EXPERT_GROUPS):
        blk = sel[g * PG:(g + 1) * PG, :]
        m1, f1 = _first_index_of_max(blk, io_g, PG)
        m2 = jnp.max(jnp.where(io_g == f1, -jnp.inf, blk), axis=0, keepdims=True)
        rows.append(m1 + m2)
    gs = jnp.concatenate(rows, axis=0)

    io_n = lax.broadcasted_iota(I32, (N_EXPERT_GROUPS, tm), 0)
    gsel = jnp.zeros((N_EXPERT_GROUPS, tm), F32)
    cur = gs
    for _ in range(TOPK_GROUPS):
        _, f = _first_index_of_max(cur, io_n, N_EXPERT_GROUPS)
        hit = io_n == f
        gsel = jnp.where(hit, 1.0, gsel)
        cur = jnp.where(hit, -jnp.inf, cur)
    masked = jnp.concatenate(
        [jnp.where(gsel[g:g + 1, :] > 0.5, sel[g * PG:(g + 1) * PG, :], -jnp.inf) for g in range(N_EXPERT_GROUPS)],
        axis=0)

    io_e = lax.broadcasted_iota(I32, (E, tm), 0)
    onehot = jnp.zeros((E, tm), F32)
    idx_rows, sc_rows = [], []
    cur = masked
    for _ in range(TOP_K):
        _, f = _first_index_of_max(cur, io_e, E)
        hit = io_e == f
        idx_rows.append(f)
        sc_rows.append(jnp.sum(jnp.where(hit, scores, 0.0), axis=0, keepdims=True))
        onehot = jnp.where(hit, 1.0, onehot)
        cur = jnp.where(hit, -jnp.inf, cur)
    sc = jnp.concatenate(sc_rows, axis=0)
    idx_ref[0] = jnp.concatenate(idx_rows, axis=0)
    w_ref[...] = sc / jnp.sum(sc, axis=0, keepdims=True) * ROUTED_SCALE

    before = jnp.dot(onehot.astype(BF16), tri_ref[...], preferred_element_type=F32) + run_ref[...]
    rank_rows = [jnp.sum(jnp.where(io_e == f, before, 0.0), axis=0, keepdims=True) for f in idx_rows]
    rank_ref[0] = jnp.concatenate(rank_rows, axis=0).astype(I32)
    run_ref[...] = run_ref[...] + jnp.sum(onehot, axis=1, keepdims=True)
    cnt_ref[...] = run_ref[...]


def _router(tok, wrt, bias):
    N, D = tok.shape
    tm = ROUTER_TILE
    assert N % tm == 0
    tri = (jnp.arange(tm)[:, None] < jnp.arange(tm)[None, :]).astype(BF16)
    return pl.pallas_call(
        _router_kernel,
        grid=(N // tm,),
        in_specs=[
            pl.BlockSpec((tm, D), lambda i: (i, 0)),
            _const_spec(wrt.shape), _const_spec(bias.shape), _const_spec(tri.shape),
        ],
        out_specs=[
            pl.BlockSpec((1, TOP_K, tm), lambda i: (i, 0, 0)),
            pl.BlockSpec((TOP_K, tm), lambda i: (0, i)),
            pl.BlockSpec((1, TOP_K, tm), lambda i: (i, 0, 0)),
            _const_spec((N_EXPERTS, 1)),
        ],
        out_shape=[
            jax.ShapeDtypeStruct((N // tm, TOP_K, tm), I32),
            jax.ShapeDtypeStruct((TOP_K, N), F32),
            jax.ShapeDtypeStruct((N // tm, TOP_K, tm), I32),
            jax.ShapeDtypeStruct((N_EXPERTS, 1), F32),
        ],
        scratch_shapes=[pltpu.VMEM((N_EXPERTS, 1), F32)],
        compiler_params=_cparams(("arbitrary",)),
        name="moe_router",
    )(tok, wrt, bias, tri)


def _slots_kernel(idx_ref, rank_ref, pstart_ref, dest_ref):
    idx = idx_ref[0]
    tm = idx.shape[1]
    io_e = lax.broadcasted_iota(I32, (N_EXPERTS, tm), 0)
    start = pstart_ref[...]
    rows = [jnp.sum(jnp.where(io_e == idx[k:k + 1, :], start, 0.0), axis=0, keepdims=True) for k in range(TOP_K)]
    dest_ref[0] = jnp.concatenate(rows, axis=0).astype(I32) + rank_ref[0]


def _slots(idx3, rank3, pstart):
    nt, _, tm = idx3.shape
    spec = pl.BlockSpec((1, TOP_K, tm), lambda i: (i, 0, 0))
    return pl.pallas_call(
        _slots_kernel,
        grid=(nt,),
        in_specs=[spec, spec, _const_spec((N_EXPERTS, 1))],
        out_specs=spec,
        out_shape=jax.ShapeDtypeStruct(idx3.shape, I32),
        compiler_params=_cparams(("arbitrary",)),
        name="moe_slots",
    )(idx3, rank3, pstart.astype(F32).reshape(N_EXPERTS, 1))


def _load_slot_table(i, dest_hbm, dest_s, sem):
    copy = pltpu.make_async_copy(dest_hbm.at[i], dest_s, sem)
    copy.start()
    copy.wait()


def _dispatch_kernel(dest_hbm, pend_ref, cnt_ref, tok_ref, xs_hbm, dest_s, zero_ref, sem_idx, sem_rows):
    i = pl.program_id(0)
    tm = tok_ref.shape[0]
    blk = zero_ref.shape[0]

    @pl.when(i == 0)
    def _():
        zero_ref[...] = jnp.zeros_like(zero_ref)

        def zero_copy(e):
            last_block = pl.multiple_of(pend_ref[e] - blk, blk)
            return pltpu.make_async_copy(zero_ref, xs_hbm.at[pl.ds(last_block, blk)], sem_rows)

        def start(e, c):
            @pl.when(cnt_ref[e] > 0)
            def _():
                zero_copy(e).start()
            return c

        def wait(e, c):
            @pl.when(cnt_ref[e] > 0)
            def _():
                zero_copy(e).wait()
            return c

        lax.fori_loop(0, N_EXPERTS, start, 0)
        lax.fori_loop(0, N_EXPERTS, wait, 0)

    _load_slot_table(i, dest_hbm, dest_s, sem_idx)

    def start(t, c):
        for k in range(TOP_K):
            pltpu.make_async_copy(tok_ref.at[t], xs_hbm.at[dest_s[k, t]], sem_rows).start()
        return c

    def wait(t, c):
        for k in range(TOP_K):
            pltpu.make_async_copy(tok_ref.at[0], xs_hbm.at[0], sem_rows).wait()
        return c

    lax.fori_loop(0, tm, start, 0)
    lax.fori_loop(0, tm, wait, 0)


def _dispatch(tok3, dest3, pend, counts, n_slots):
    N = tok3.shape[0]
    tm = dest3.shape[2]
    assert N % tm == 0 and dest3.shape == (N // tm, TOP_K, tm)
    smem = pl.BlockSpec(memory_space=pltpu.SMEM)
    return pl.pallas_call(
        _dispatch_kernel,
        grid=(N // tm,),
        in_specs=[
            pl.BlockSpec(memory_space=pl.ANY), smem, smem,
            pl.BlockSpec((tm, SUBLANES, LANES), lambda i: (i, 0, 0)),
        ],
        out_specs=pl.BlockSpec(memory_space=pl.ANY),
        scratch_shapes=[
            pltpu.SMEM((TOP_K, tm), I32),
            pltpu.VMEM((EXPERT_BLOCK, SUBLANES, LANES), F32),
            pltpu.SemaphoreType.DMA,
            pltpu.SemaphoreType.DMA,
        ],
        out_shape=jax.ShapeDtypeStruct((n_slots, SUBLANES, LANES), F32),
        compiler_params=_cparams(("arbitrary",)),
        name="moe_dispatch",
    )(dest3, pend, counts, tok3)


def _ffn_kernel(be_ref, nu_ref, buf_ref, nxt_ref, xs_ref, wg_hbm, wu_hbm, wd_hbm, ys_ref,
                wg_buf, wu_buf, wd_buf, wgu_s, wd_s, sems, *, layer):
    b = pl.program_id(0)
    used = b < nu_ref[0]
    prev = be_ref[jnp.maximum(b - 1, 0)]
    new_expert = jnp.logical_or(b == 0, be_ref[b] != prev)

    def weight_copies(e, s):
        return [pltpu.make_async_copy(wg_hbm.at[layer, e], wg_buf.at[s], sems.at[s]),
                pltpu.make_async_copy(wu_hbm.at[layer, e], wu_buf.at[s], sems.at[s]),
                pltpu.make_async_copy(wd_hbm.at[layer, e], wd_buf.at[s], sems.at[s])]

    @pl.when(jnp.logical_and(used, b == 0))
    def _():
        for c in weight_copies(be_ref[0], buf_ref[0]):
            c.start()

    @pl.when(jnp.logical_and(used, new_expert))
    def _():
        s = buf_ref[b]

        @pl.when(nxt_ref[b] >= 0)
        def _():
            for c in weight_copies(nxt_ref[b], 1 - s):
                c.start()

        for c in weight_copies(be_ref[b], s):
            c.wait()
        wgu_s[:, :D_EXPERT] = wg_buf[s].astype(BF16)
        wgu_s[:, D_EXPERT:] = wu_buf[s].astype(BF16)
        wd_s[...] = wd_buf[s].astype(BF16)

    @pl.when(used)
    def _():
        blk = xs_ref.shape[0] // SUBLANES
        x = jnp.concatenate([xs_ref[pl.ds(c, blk, stride=SUBLANES), :] for c in range(SUBLANES)], axis=1)
        gu = jnp.dot(x.astype(BF16), wgu_s[...], preferred_element_type=F32)
        gt = gu[:, :D_EXPERT]
        hid = gt * jax.nn.sigmoid(gt) * gu[:, D_EXPERT:]
        y = jnp.dot(hid.astype(BF16), wd_s[...], preferred_element_type=F32)
        for c in range(SUBLANES):
            ys_ref[pl.ds(c, blk, stride=SUBLANES), :] = y[:, c * LANES:(c + 1) * LANES]


def _expert_ffn(xs, block_e, n_used, layer, wg, wu, wd):
    n_slots = xs.shape[0]
    D = SUBLANES * LANES
    blk = EXPERT_BLOCK
    n_blocks = n_slots // blk

    def row_map(b, be, nu, buf, nxt):
        return (jnp.minimum(b, nu[0] - 1), 0)

    first = jnp.concatenate([jnp.ones((1,), bool), block_e[1:] != block_e[:-1]])
    buf = ((jnp.cumsum(first.astype(I32)) - 1) % 2).astype(I32)
    nxt_block = jnp.sum(block_e[None, :] <= block_e[:, None], axis=1).astype(I32)
    nxt = jnp.where(nxt_block < n_used[0], block_e[jnp.minimum(nxt_block, n_blocks - 1)], -1).astype(I32)

    hbm = pl.BlockSpec(memory_space=pl.ANY)
    return pl.pallas_call(
        functools.partial(_ffn_kernel, layer=layer),
        grid_spec=pltpu.PrefetchScalarGridSpec(
            num_scalar_prefetch=4,
            grid=(n_blocks,),
            in_specs=[pl.BlockSpec((blk * SUBLANES, LANES), row_map), hbm, hbm, hbm],
            out_specs=pl.BlockSpec((blk * SUBLANES, LANES), row_map),
            scratch_shapes=[
                pltpu.VMEM((2, D, D_EXPERT), F32), pltpu.VMEM((2, D, D_EXPERT), F32), pltpu.VMEM((2, D_EXPERT, D), F32),
                pltpu.VMEM((D, 2 * D_EXPERT), BF16), pltpu.VMEM((D_EXPERT, D), BF16),
                pltpu.SemaphoreType.DMA((2,)),
            ],
        ),
        out_shape=jax.ShapeDtypeStruct((n_slots * SUBLANES, LANES), F32),
        compiler_params=_cparams(("arbitrary",)),
        name="moe_ffn",
    )(block_e, n_used, buf, nxt, xs.reshape(n_slots * SUBLANES, LANES), wg, wu, wd).reshape(n_slots, SUBLANES, LANES)


def _combine_kernel(dest_hbm, ys_hbm, tok_ref, w_ref, wsg_ref, wsu_ref, wsd_ref, g_ref, b_ref,
                    out_ref, dest_s, buf_ref, sem_idx, sem_rows):
    i = pl.program_id(0)
    tm = tok_ref.shape[0]

    _load_slot_table(i, dest_hbm, dest_s, sem_idx)

    def start(t, c):
        row = pl.multiple_of(t * SUBLANES, SUBLANES)
        for k in range(TOP_K):
            pltpu.make_async_copy(ys_hbm.at[dest_s[k, t]], buf_ref.at[k, pl.ds(row, SUBLANES)], sem_rows).start()
        return c

    def wait(t, c):
        for k in range(TOP_K):
            pltpu.make_async_copy(ys_hbm.at[0], buf_ref.at[0, pl.ds(0, SUBLANES)], sem_rows).wait()
        return c

    lax.fori_loop(0, tm, start, 0)

    h = tok_ref[...]
    hb = h.astype(BF16)
    gt = jnp.dot(hb, wsg_ref[...], preferred_element_type=F32)
    up = jnp.dot(hb, wsu_ref[...], preferred_element_type=F32)
    hid = gt * jax.nn.sigmoid(gt) * up
    f = jnp.dot(hid.astype(BF16), wsd_ref[...], preferred_element_type=F32)

    lax.fori_loop(0, tm, wait, 0)
    w = w_ref[...]
    wb = [jnp.broadcast_to(w[:, k:k + 1], (tm, LANES)) for k in range(TOP_K)]
    chunks = []
    for c in range(SUBLANES):
        acc = f[:, c * LANES:(c + 1) * LANES]
        for k in range(TOP_K):
            acc = acc + buf_ref[k, pl.ds(c, tm, stride=SUBLANES), :] * wb[k]
        chunks.append(acc)
    f = jnp.concatenate(chunks, axis=1)
    out_ref[...] = _layer_norm(DEEPNORM_ALPHA * h + f, g_ref[...], b_ref[...])


def _combine(tok, ys, dest3, w_tok, wsg, wsu, wsd, g, b):
    N, D = tok.shape
    tm = dest3.shape[2]
    assert N % tm == 0 and dest3.shape == (N // tm, TOP_K, tm)
    return pl.pallas_call(
        _combine_kernel,
        grid=(N // tm,),
        in_specs=[
            pl.BlockSpec(memory_space=pl.ANY),
            pl.BlockSpec(memory_space=pl.ANY),
            pl.BlockSpec((tm, D), lambda i: (i, 0)),
            pl.BlockSpec((tm, TOP_K), lambda i: (i, 0)),
            _const_spec(wsg.shape), _const_spec(wsu.shape), _const_spec(wsd.shape),
            _const_spec(g.shape), _const_spec(b.shape),
        ],
        out_specs=pl.BlockSpec((tm, D), lambda i: (i, 0)),
        scratch_shapes=[
            pltpu.SMEM((TOP_K, tm), I32),
            pltpu.VMEM((TOP_K, tm * SUBLANES, LANES), F32),
            pltpu.SemaphoreType.DMA,
            pltpu.SemaphoreType.DMA,
        ],
        out_shape=jax.ShapeDtypeStruct((N, D), F32),
        compiler_params=_cparams(("arbitrary",)),
        name="moe_combine",
    )(dest3, ys, tok, w_tok, wsg, wsu, wsd, g, b)


def _moe_layer(tok, layer, w_router, router_bias, we_gate, we_up, we_down, ws_gate, ws_up, ws_down, ln_g, ln_b):
    N, D = tok.shape
    blk = EXPERT_BLOCK
    idx3, w_t, rank3, cnt = _router(tok, w_router.T.astype(BF16), router_bias.reshape(N_EXPERTS, 1))

    counts = cnt[:, 0].astype(I32)
    pcounts = (counts + blk - 1) // blk * blk
    pend = jnp.cumsum(pcounts).astype(I32)
    pstart = pend - pcounts
    n_blocks = N * TOP_K // blk + N_EXPERTS
    block_start = jnp.arange(n_blocks, dtype=I32) * blk
    block_e = jnp.minimum(jnp.sum(pend[None, :] <= block_start[:, None], axis=1), N_EXPERTS - 1).astype(I32)
    n_used = (pend[-1] // blk).reshape(1)

    dest3 = _slots(idx3, rank3, pstart)
    xs = _dispatch(tok.reshape(N, SUBLANES, LANES), dest3, pend, counts, n_blocks * blk)
    ys = _expert_ffn(xs, block_e, n_used, layer, we_gate, we_up, we_down)
    return _combine(tok, ys, dest3, w_t.T,
                    ws_gate.astype(BF16), ws_up.astype(BF16), ws_down.astype(BF16),
                    ln_g.reshape(1, D), ln_b.reshape(1, D))


def _rel_bucket(dist):
    exact = N_BUCKETS // 2
    df = jnp.maximum(dist, 1).astype(F32)
    large = exact + (jnp.log(df / exact) / math.log(MAX_DISTANCE / exact) * (N_BUCKETS - exact)).astype(I32)
    large = jnp.minimum(large, N_BUCKETS - 1)
    return jnp.where(dist < exact, dist, large)


def _qkv_kernel(x_ref, perm_ref, w_ref, wkvt_ref, q_ref, k_ref, v_ref, kt_ref, vt_ref, *, dil, first_kept_tile,
                kept_cols):
    tl = x_ref.shape[0]
    chunk = tl // dil
    x = x_ref[...].astype(BF16)
    xp = x if dil == 1 else jnp.dot(perm_ref[...], x, preferred_element_type=F32).astype(BF16)
    y = jnp.dot(xp, w_ref[...], preferred_element_type=F32)
    for r in range(dil):
        rows = slice(r * chunk, (r + 1) * chunk)
        q_ref[r] = y[rows, :D_ATTN].astype(BF16)
        k_ref[r] = y[rows, D_ATTN:2 * D_ATTN].astype(BF16)
        v_ref[r] = y[rows, 2 * D_ATTN:].astype(BF16)

    @pl.when(pl.program_id(1) >= first_kept_tile)
    def _():
        kvt = lax.dot_general(wkvt_ref[...], x[tl - kept_cols:, :], (((1,), (1,)), ((), ())),
                              preferred_element_type=F32)
        kt_ref[...] = kvt[:D_ATTN]
        vt_ref[...] = kvt[D_ATTN:]


def _qkv_prompt(h, w_g, dil, keep):
    B, S, D = h.shape
    tl = min(S, 512)
    kept_cols = min(tl, keep)
    assert S % tl == 0 and tl % dil == 0 and keep % kept_cols == 0 and (S - keep) % kept_cols == 0
    first_kept_tile = (S - keep) // tl
    phase_spec = pl.BlockSpec((None, dil, tl // dil, D_ATTN), lambda b, t: (b, 0, t, 0))
    state_spec = pl.BlockSpec((None, D_ATTN, kept_cols), lambda b, t: (b, 0, jnp.maximum(t - first_kept_tile, 0)))
    wkvt = w_g[:, D_ATTN:].T
    dst = jnp.arange(tl, dtype=I32)
    src = (dst % (tl // dil)) * dil + dst // (tl // dil)
    perm = (src[:, None] == jnp.arange(tl, dtype=I32)[None, :]).astype(BF16)
    return pl.pallas_call(
        functools.partial(_qkv_kernel, dil=dil, first_kept_tile=first_kept_tile, kept_cols=kept_cols),
        grid=(B, S // tl),
        in_specs=[pl.BlockSpec((None, tl, D), lambda b, t: (b, t, 0)), _const_spec(perm.shape),
                  _const_spec(w_g.shape), _const_spec(wkvt.shape)],
        out_specs=[phase_spec, phase_spec, phase_spec, state_spec, state_spec],
        out_shape=[jax.ShapeDtypeStruct((B, dil, S // dil, D_ATTN), BF16)] * 3
        + [jax.ShapeDtypeStruct((B, D_ATTN, keep), F32)] * 2,
        compiler_params=_cparams(("arbitrary", "arbitrary")),
        name="qkv_prompt",
    )(h, perm, w_g, wkvt)


def _attn_prompt_kernel(q_ref, kp_ref, kc_ref, vp_ref, vc_ref, bias_ref, o_ref, lse_ref, *, steps):
    n = pl.program_id(1)
    qb = QUERY_BLOCK
    qi = lax.broadcasted_iota(I32, (qb, 2 * qb), 0)
    kj = lax.broadcasted_iota(I32, (qb, 2 * qb), 1)
    dist = qi + qb - kj
    valid = (dist >= 0) & (dist <= steps) & ((n > 0) | (kj >= qb))
    low = lax.broadcasted_iota(I32, (1, 2 * HEAD_DIM), 1) < HEAD_DIM
    scale = HEAD_DIM ** -0.5
    for p in range(HEADS // 2):
        c0 = p * 2 * HEAD_DIM
        q2 = q_ref[0, :, c0:c0 + 2 * HEAD_DIM]
        k2 = jnp.concatenate([kp_ref[0, :, c0:c0 + 2 * HEAD_DIM], kc_ref[0, :, c0:c0 + 2 * HEAD_DIM]], axis=0)
        v2 = jnp.concatenate([vp_ref[0, :, c0:c0 + 2 * HEAD_DIM], vc_ref[0, :, c0:c0 + 2 * HEAD_DIM]], axis=0)
        outs, lses = [], []
        for half in range(2):
            keep = low if half == 0 else jnp.logical_not(low)
            qh = jnp.where(keep, q2, jnp.zeros_like(q2))
            s = lax.dot_general(qh, k2, (((1,), (1,)), ((), ())), preferred_element_type=F32)
            s = s * scale + bias_ref[2 * p + half]
            s = jnp.where(valid, s, NEG_INF)
            m = jnp.max(s, axis=-1, keepdims=True)
            e = jnp.exp(s - m)
            l = jnp.sum(e, axis=-1, keepdims=True)
            prob = e / l
            outs.append(jnp.dot(prob.astype(BF16), v2, preferred_element_type=F32))
            lses.append(m + jnp.log(l))
        o_ref[0, :, c0:c0 + 2 * HEAD_DIM] = jnp.where(low, outs[0], outs[1])
        lse_ref[0, :, c0:c0 + 2 * HEAD_DIM] = jnp.where(low, lses[0], lses[1])


def _attn_prompt(q, k, v, bias, steps):
    B, dil, L, _ = q.shape
    qb = QUERY_BLOCK
    assert L % qb == 0
    cur = pl.BlockSpec((1, qb, D_ATTN), lambda bd, n: (bd, n, 0))
    prev = pl.BlockSpec((1, qb, D_ATTN), lambda bd, n: (bd, jnp.maximum(n - 1, 0), 0))
    out = pl.BlockSpec((1, qb, D_ATTN), lambda bd, n: (bd // dil, n, bd % dil))
    qv, kv, vv = (a.reshape(B * dil, L, D_ATTN) for a in (q, k, v))
    o, lse = pl.pallas_call(
        functools.partial(_attn_prompt_kernel, steps=steps),
        grid=(B * dil, L // qb),
        in_specs=[cur, prev, cur, prev, cur, _const_spec(bias.shape)],
        out_specs=[out, out],
        out_shape=[jax.ShapeDtypeStruct((B, L, dil * D_ATTN), F32)] * 2,
        compiler_params=_cparams(("arbitrary", "arbitrary")),
        name="attn_prompt",
    )(qv, kv, kv, vv, vv, bias)
    return o.reshape(B * L * dil, D_ATTN), lse.reshape(B * L * dil, D_ATTN)


def _merge_kernel(o1, o2, o3, l1, l2, l3, h_ref, wo_ref, g_ref, b_ref, out_ref):
    a1, a2, a3 = l1[...], l2[...], l3[...]
    m = jnp.maximum(jnp.maximum(a1, a2), a3)
    e1, e2, e3 = jnp.exp(a1 - m), jnp.exp(a2 - m), jnp.exp(a3 - m)
    o = (e1 * o1[...] + e2 * o2[...] + e3 * o3[...]) / (e1 + e2 + e3)
    y = jnp.dot(o.astype(BF16), wo_ref[...], preferred_element_type=F32)
    out_ref[...] = _layer_norm(DEEPNORM_ALPHA * h_ref[...] + y, g_ref[...], b_ref[...])


def _merge(os_, ls_, h, wo, g, b):
    M, D = h.shape
    tm = min(MERGE_TILE, M)
    assert M % tm == 0
    a_spec = pl.BlockSpec((tm, D_ATTN), lambda i: (i, 0))
    return pl.pallas_call(
        _merge_kernel,
        grid=(M // tm,),
        in_specs=[a_spec] * 6 + [pl.BlockSpec((tm, D), lambda i: (i, 0)), _const_spec(wo.shape),
                                 _const_spec(g.shape), _const_spec(b.shape)],
        out_specs=pl.BlockSpec((tm, D), lambda i: (i, 0)),
        out_shape=jax.ShapeDtypeStruct((M, D), F32),
        compiler_params=_cparams(("arbitrary",)),
        name="attn_merge",
    )(*os_, *ls_, h, wo, g, b)


def _mm_kernel(x_ref, w_ref, o_ref):
    o_ref[...] = jnp.dot(x_ref[...].astype(BF16), w_ref[...], preferred_element_type=F32)


def _matmul(x, w, tn):
    M, K = x.shape
    _, N = w.shape
    assert N % tn == 0
    return pl.pallas_call(
        _mm_kernel,
        grid=(N // tn,),
        in_specs=[_const_spec((M, K)), pl.BlockSpec((K, tn), lambda j: (0, j))],
        out_specs=pl.BlockSpec((M, tn), lambda j: (0, j)),
        out_shape=jax.ShapeDtypeStruct((M, N), F32),
        compiler_params=_cparams(("arbitrary",)),
        name="matmul",
    )(x, w)


def _round_bf16(x):
    return x.astype(BF16).astype(F32)


def _attn_sample_kernel(q_ref, kn_ref, vn_ref, kc_ref, vc_ref, bias_ref, b0_ref, o_ref, lse_ref):
    scale = HEAD_DIM ** -0.5
    q = _round_bf16(q_ref[...])
    kn = _round_bf16(kn_ref[...])
    vn = _round_bf16(vn_ref[...])
    kc = _round_bf16(kc_ref[...])
    s = jnp.sum(kc * q, axis=2, keepdims=True) * scale + bias_ref[...][None]
    s0 = jnp.sum(q * kn, axis=2, keepdims=True) * scale + b0_ref[...][None]
    m = jnp.maximum(jnp.max(s, axis=3, keepdims=True), s0)
    e = jnp.exp(s - m)
    e0 = jnp.exp(s0 - m)
    l = jnp.sum(e, axis=3, keepdims=True) + e0
    p = _round_bf16(e / l)
    p0 = _round_bf16(e0 / l)
    vc = _round_bf16(vc_ref[...])
    o_ref[...] = jnp.sum(vc * p, axis=3, keepdims=True) + p0 * vn
    lse_ref[...] = m + jnp.log(l)


def _attn_sample(q, kn, vn, ckt, cvt, j, bias, bias_self, tb=1):
    DB = q.shape[0]
    Wb = ckt.shape[3]
    assert DB % tb == 0
    new_spec = pl.BlockSpec((tb, HEADS, HEAD_DIM, 1), lambda i: (i, 0, 0, 0))
    cache_spec = pl.BlockSpec((tb, HEADS, HEAD_DIM, Wb), lambda i: (j * (DB // tb) + i, 0, 0, 0))
    return pl.pallas_call(
        _attn_sample_kernel,
        grid=(DB // tb,),
        in_specs=[new_spec, new_spec, new_spec, cache_spec, cache_spec,
                  _const_spec(bias.shape), _const_spec(bias_self.shape)],
        out_specs=[new_spec, pl.BlockSpec((tb, HEADS, 1, 1), lambda i: (i, 0, 0, 0))],
        out_shape=[jax.ShapeDtypeStruct((DB, HEADS, HEAD_DIM, 1), F32), jax.ShapeDtypeStruct((DB, HEADS, 1, 1), F32)],
        compiler_params=_cparams(("arbitrary",)),
        name="attn_sample",
    )(q, kn, vn, ckt, cvt, bias, bias_self)


def _bias_lookup(tab, buckets):
    onehot = (buckets[..., None] == jnp.arange(N_BUCKETS, dtype=I32)).astype(F32)
    return jnp.einsum('...n,nh->...h', onehot, tab, precision=lax.Precision.HIGHEST)


def _dilated_layer(hp, hs, caches, j, w_qkv, w_o, rel_bias, g1, b1):
    B, S, D = hp.shape
    DB = hs.shape[0]
    w_qkv_b = w_qkv.astype(BF16)
    w_o_b = w_o.astype(BF16)
    qb = QUERY_BLOCK
    qi = jnp.arange(qb, dtype=I32)[:, None]
    kj = jnp.arange(2 * qb, dtype=I32)[None, :]
    dist = qi + qb - kj
    qkv_s = _matmul(hs, w_qkv_b, 512)

    def sample_cols(part, g):
        c0 = (part * N_GROUPS + g) * D_ATTN
        return qkv_s[:, c0:c0 + D_ATTN].reshape(DB, HEADS, HEAD_DIM)

    def position_minor(c):
        return jnp.transpose(c, (0, 1, 3, 4, 2)).reshape(c.shape[0] * DB, HEADS, HEAD_DIM, c.shape[2])

    os_, ls_, kp, vp = [], [], [], []
    os_s, ls_s, ks, vs = [], [], [], []
    for g, (window, dil) in enumerate(DIL_CONFIGS):
        steps = window // dil
        assert steps == qb
        tab = rel_bias[:, g * HEADS:(g + 1) * HEADS]
        cols = [w_qkv_b[:, (part * N_GROUPS + g) * D_ATTN:(part * N_GROUPS + g + 1) * D_ATTN] for part in range(3)]
        keep = min(window, S)
        q, k, v, kt, vt = _qkv_prompt(hp, jnp.concatenate(cols, axis=1), dil, keep)
        bias = _bias_lookup(tab, _rel_bucket(jnp.maximum(dist, 0) * dil)).transpose(2, 0, 1)
        o, lse = _attn_prompt(q, k, v, bias, steps)
        os_.append(o)
        ls_.append(lse)
        kp.append(jnp.transpose(kt.reshape(B, HEADS, HEAD_DIM, keep), (0, 3, 1, 2)))
        vp.append(jnp.transpose(vt.reshape(B, HEADS, HEAD_DIM, keep), (0, 3, 1, 2)))
        ck, cv = caches[g]
        wb = ck.shape[2]
        assert wb == steps * dil and ck.shape[1] == DB
        pos = jnp.arange(wb, dtype=I32)
        cache_bias = jnp.where((pos % dil == 0)[:, None], _bias_lookup(tab, _rel_bucket(wb - pos)), NEG_INF)
        self_bias = _bias_lookup(tab, _rel_bucket(jnp.zeros((1,), I32)))
        q_s, k_s, v_s = sample_cols(0, g), sample_cols(1, g), sample_cols(2, g)
        o_s, lse_s = _attn_sample(q_s[..., None], k_s[..., None], v_s[..., None], position_minor(ck),
                                  position_minor(cv), j, cache_bias.T[:, None, :], self_bias.T[:, :, None],
                                  tb=max(1, 1024 // wb))
        os_s.append(o_s.reshape(DB, D_ATTN))
        ls_s.append(jnp.broadcast_to(lse_s.reshape(DB, HEADS, 1), (DB, HEADS, HEAD_DIM)).reshape(DB, D_ATTN))
        ks.append(k_s[:, None])
        vs.append(v_s[:, None])
    h_p = _merge(os_, ls_, hp.reshape(B * S, D), w_o_b, g1, b1).reshape(B, S, D)
    h_s = _merge(os_s, ls_s, hs, w_o_b, g1, b1)
    return h_p, h_s, kp, vp, ks, vs


def _pool_conv_layer(hp, hs, state_pool, state_conv, w_in, pool_w, pool_scale, conv_w, conv_b, ln_g, ln_b, w_out,
                     g1, b1):
    params = (w_in.astype(BF16), pool_w.astype(BF16), pool_scale.reshape(1, D_POOL), conv_w,
              conv_b.reshape(1, D_CONV), ln_g.reshape(1, D_CONV), ln_b.reshape(1, D_CONV), w_out.astype(BF16), g1, b1)
    h_p, pst, cst = _l0_prompt(hp, *params)
    h_s, u_s, glu_s = _l0_sample(hs, state_pool, state_conv, *params)
    pool_p = pst[:, POOL_HALO - POOL_STATE:]
    conv_p = cst[:, CONV_HALO - CONV_STATE:]
    pool_s = jnp.concatenate([state_pool[:, 1:], u_s[:, None, :]], axis=1)
    conv_s = jnp.concatenate([state_conv[:, 1:], glu_s[:, None, :]], axis=1)
    return h_p, h_s, pool_p, conv_p, pool_s, conv_s


def kernel(x_prompt, x_sample, state_pool, state_conv, cache_k1, cache_v1, cache_k2, cache_v2, cache_k3, cache_v3,
           w_in_ab, pool_w, pool_scale, conv_w, conv_b, conv_ln_g, conv_ln_b, w_out_ab, w_qkv, w_o, rel_bias,
           ln1_g, ln1_b, ln2_g, ln2_b, w_router, router_bias, we_gate, we_up, we_down, ws_gate, ws_up, ws_down):
    B, S, D = x_prompt.shape
    DB, T, _ = x_sample.shape
    assert T == 1 and D == D_MODEL
    hp = x_prompt
    hs = x_sample.reshape(DB, D)
    caches_k = (cache_k1, cache_k2, cache_k3)
    caches_v = (cache_v1, cache_v2, cache_v3)
    pool_p, conv_p, pool_s, conv_s = [], [], [], []
    kp = [[] for _ in range(N_GROUPS)]
    vp = [[] for _ in range(N_GROUPS)]
    ksm = [[] for _ in range(N_GROUPS)]
    vsm = [[] for _ in range(N_GROUPS)]
    for layer in range(DEPTH):
        j = layer // 2
        g1 = ln1_g[layer].reshape(1, D)
        b1 = ln1_b[layer].reshape(1, D)
        if layer % 2 == 0:
            hp, hs, pp, cp, ps, cs = _pool_conv_layer(
                hp, hs, state_pool[j], state_conv[j], w_in_ab[j], pool_w[j], pool_scale[j], conv_w[j], conv_b[j],
                conv_ln_g[j], conv_ln_b[j], w_out_ab[j], g1, b1)
            pool_p.append(pp)
            conv_p.append(cp)
            pool_s.append(ps)
            conv_s.append(cs)
        else:
            caches = [(caches_k[g], caches_v[g]) for g in range(N_GROUPS)]
            hp, hs, nkp, nvp, nks, nvs = _dilated_layer(hp, hs, caches, j, w_qkv[j], w_o[j], rel_bias, g1, b1)
            for g in range(N_GROUPS):
                kp[g].append(nkp[g])
                vp[g].append(nvp[g])
                ksm[g].append(nks[g])
                vsm[g].append(nvs[g])
        tok = jnp.concatenate([hp.reshape(B * S, D), hs], axis=0)
        tok = _moe_layer(tok, layer, w_router[layer], router_bias[layer], we_gate, we_up, we_down,
                         ws_gate[layer], ws_up[layer], ws_down[layer], ln2_g[layer], ln2_b[layer])
        hp = tok[:B * S].reshape(B, S, D)
        hs = tok[B * S:]
    return (hp, hs.reshape(DB, T, D),
            jnp.stack(pool_p), jnp.stack(conv_p),
            jnp.stack(kp[0]), jnp.stack(vp[0]), jnp.stack(kp[1]), jnp.stack(vp[1]), jnp.stack(kp[2]), jnp.stack(vp[2]),
            jnp.stack(pool_s), jnp.stack(conv_s),
            jnp.stack(ksm[0]), jnp.stack(vsm[0]), jnp.stack(ksm[1]), jnp.stack(vsm[1]),
            jnp.stack(ksm[2]), jnp.stack(vsm[2]))
```

```python
import functools
import math

import jax
import jax.numpy as jnp
from jax import lax
from jax.experimental import pallas as pl
from jax.experimental.pallas import tpu as pltpu

F32 = jnp.float32
BF16 = jnp.bfloat16
I32 = jnp.int32
U32 = jnp.uint32

D_MODEL = 1024
D_POOL = 512
D_CONV = 512
POOL_WINDOWS = (2, 4, 8, 16)
POOL_GROUP = 128
POOL_STATE = 15
CONV_WIDTH = 31
CONV_STATE = 30
DIL_CONFIGS = ((128, 1), (512, 4), (2048, 16))
N_GROUPS = 3
HEADS = 8
HEAD_DIM = 64
D_ATTN = HEADS * HEAD_DIM
QUERY_BLOCK = 128
N_BUCKETS = 32
MAX_DISTANCE = 2048
N_EXPERTS = 256
TOP_K = 8
N_EXPERT_GROUPS = 8
EXPERTS_PER_GROUP = N_EXPERTS // N_EXPERT_GROUPS
TOPK_GROUPS = 4
D_EXPERT = 256
ROUTED_SCALE = 2.5
DEPTH = 2
DEEPNORM_ALPHA = (2.0 * DEPTH) ** 0.25
LN_EPS = 1e-5
NEG_INF = -1e30

LANES = 128
SUBLANES = 8
PACKED_WORDS = D_MODEL // 2
PACKED_SUBLANES = PACKED_WORDS // LANES
VMEM_LIMIT_BYTES = 56 * 1024 * 1024

POOL_HALO = 16
CONV_HALO = 32
L0_TIME_TILE = 256
ROUTER_TILE = 384
EXPERT_BLOCK = 256
MERGE_TILE = 512


def _cparams(sem):
    return pltpu.CompilerParams(dimension_semantics=sem, vmem_limit_bytes=VMEM_LIMIT_BYTES)


def _layer_norm(x, g, b):
    mu = jnp.mean(x, axis=-1, keepdims=True)
    xc = x - mu
    var = jnp.mean(xc * xc, axis=-1, keepdims=True)
    return xc * lax.rsqrt(var + LN_EPS) * g + b


def _const_spec(shape):
    nd = len(shape)
    return pl.BlockSpec(shape, lambda *_: (0,) * nd)


def _l0_prompt_kernel(x_ref, win_ref, pw_ref, ps_ref, cw_ref, cb_ref, cg_ref, cbeta_ref, wout_ref,
                      g1_ref, b1_ref, h_ref, pstate_ref, cstate_ref, ue_ref, ge_ref):
    tt = x_ref.shape[1]
    t = pl.program_id(1)

    @pl.when(t == 0)
    def _():
        ue_ref[0:POOL_HALO, :] = jnp.zeros((POOL_HALO, D_POOL), F32)
        ge_ref[0:CONV_HALO, :] = jnp.zeros((CONV_HALO, D_CONV), F32)

    @pl.when(t > 0)
    def _():
        ue_ref[0:POOL_HALO, :] = ue_ref[tt:tt + POOL_HALO, :]
        ge_ref[0:CONV_HALO, :] = ge_ref[tt:tt + CONV_HALO, :]

    x = x_ref[0]
    proj = jnp.dot(x.astype(BF16), win_ref[...], preferred_element_type=F32)
    u = proj[:, :D_POOL]
    a = proj[:, D_POOL:D_POOL + D_CONV]
    gate = proj[:, D_POOL + D_CONV:]
    glu = a * jax.nn.sigmoid(gate)
    ue_ref[POOL_HALO:POOL_HALO + tt, :] = u
    ge_ref[CONV_HALO:CONV_HALO + tt, :] = glu

    tg = t * tt + lax.broadcasted_iota(I32, (tt, 1), 0)
    parts = []
    for g, w in enumerate(POOL_WINDOWS):
        c0 = g * POOL_GROUP
        ug = u[:, c0:c0 + POOL_GROUP]
        s = ug
        for j in range(1, w):
            s = s + ue_ref[POOL_HALO - j:POOL_HALO - j + tt, c0:c0 + POOL_GROUP]
        cnt = jnp.minimum(tg + 1, w).astype(F32)
        pooled = s / cnt - ug
        parts.append(jnp.dot(pooled.astype(BF16), pw_ref[g], preferred_element_type=F32))
    yp = jnp.concatenate(parts, axis=1) * ps_ref[...]

    acc = glu * cw_ref[CONV_STATE:CONV_STATE + 1, :]
    off = CONV_HALO - CONV_STATE
    for j in range(CONV_STATE):
        acc = acc + ge_ref[off + j:off + j + tt, :] * cw_ref[j:j + 1, :]
    yn = _layer_norm(acc + cb_ref[...], cg_ref[...], cbeta_ref[...])
    yc = yn * jax.nn.sigmoid(yn)

    cat = jnp.concatenate([yp, yc], axis=1).astype(BF16)
    m = jnp.dot(cat, wout_ref[...], preferred_element_type=F32)
    h_ref[0] = _layer_norm(DEEPNORM_ALPHA * x + m, g1_ref[...], b1_ref[...])
    pstate_ref[0] = ue_ref[tt:tt + POOL_HALO, :]
    cstate_ref[0] = ge_ref[tt:tt + CONV_HALO, :]


def _l0_prompt(x, win, pw, ps, cw, cb, cg, cbeta, wout, g1, b1):
    B, S, D = x.shape
    tt = min(L0_TIME_TILE, S)
    assert S % tt == 0 and tt >= CONV_HALO
    return pl.pallas_call(
        _l0_prompt_kernel,
        grid=(B, S // tt),
        in_specs=[
            pl.BlockSpec((1, tt, D), lambda b, t: (b, t, 0)),
            _const_spec(win.shape), _const_spec(pw.shape), _const_spec(ps.shape), _const_spec(cw.shape),
            _const_spec(cb.shape), _const_spec(cg.shape), _const_spec(cbeta.shape), _const_spec(wout.shape),
            _const_spec(g1.shape), _const_spec(b1.shape),
        ],
        out_specs=[
            pl.BlockSpec((1, tt, D), lambda b, t: (b, t, 0)),
            pl.BlockSpec((1, POOL_HALO, D_POOL), lambda b, t: (b, 0, 0)),
            pl.BlockSpec((1, CONV_HALO, D_CONV), lambda b, t: (b, 0, 0)),
        ],
        out_shape=[
            jax.ShapeDtypeStruct((B, S, D), F32),
            jax.ShapeDtypeStruct((B, POOL_HALO, D_POOL), F32),
            jax.ShapeDtypeStruct((B, CONV_HALO, D_CONV), F32),
        ],
        scratch_shapes=[pltpu.VMEM((tt + POOL_HALO, D_POOL), F32), pltpu.VMEM((tt + CONV_HALO, D_CONV), F32)],
        compiler_params=_cparams(("arbitrary", "arbitrary")),
        name="l0_prompt",
    )(x, win, pw, ps, cw, cb, cg, cbeta, wout, g1, b1)


def _l0_sample_kernel(x_ref, sp_ref, sc_ref, win_ref, pw_ref, ps_ref, cw_ref, cb_ref, cg_ref, cbeta_ref,
                      wout_ref, g1_ref, b1_ref, h_ref, u_ref, glu_ref):
    x = x_ref[...]
    proj = jnp.dot(x.astype(BF16), win_ref[...], preferred_element_type=F32)
    u = proj[:, :D_POOL]
    a = proj[:, D_POOL:D_POOL + D_CONV]
    gate = proj[:, D_POOL + D_CONV:]
    glu = a * jax.nn.sigmoid(gate)
    u_ref[...] = u
    glu_ref[...] = glu

    parts = []
    for g, w in enumerate(POOL_WINDOWS):
        c0 = g * POOL_GROUP
        ug = u[:, c0:c0 + POOL_GROUP]
        past = sp_ref[:, POOL_STATE - (w - 1):POOL_STATE, c0:c0 + POOL_GROUP]
        s = ug + jnp.sum(past, axis=1)
        pooled = s / float(w) - ug
        parts.append(jnp.dot(pooled.astype(BF16), pw_ref[g], preferred_element_type=F32))
    yp = jnp.concatenate(parts, axis=1) * ps_ref[...]

    acc = glu * cw_ref[CONV_STATE:CONV_STATE + 1, :]
    acc = acc + jnp.sum(sc_ref[...] * cw_ref[0:CONV_STATE, :][None, :, :], axis=1)
    yn = _layer_norm(acc + cb_ref[...], cg_ref[...], cbeta_ref[...])
    yc = yn * jax.nn.sigmoid(yn)

    cat = jnp.concatenate([yp, yc], axis=1).astype(BF16)
    m = jnp.dot(cat, wout_ref[...], preferred_element_type=F32)
    h_ref[...] = _layer_norm(DEEPNORM_ALPHA * x + m, g1_ref[...], b1_ref[...])


def _l0_sample(x, sp, sc, win, pw, ps, cw, cb, cg, cbeta, wout, g1, b1):
    DB, D = x.shape
    args = (x, sp, sc, win, pw, ps, cw, cb, cg, cbeta, wout, g1, b1)
    return pl.pallas_call(
        _l0_sample_kernel,
        grid=(1,),
        in_specs=[_const_spec(a.shape) for a in args],
        out_specs=[_const_spec((DB, D)), _const_spec((DB, D_POOL)), _const_spec((DB, D_CONV))],
        out_shape=[
            jax.ShapeDtypeStruct((DB, D), F32),
            jax.ShapeDtypeStruct((DB, D_POOL), F32),
            jax.ShapeDtypeStruct((DB, D_CONV), F32),
        ],
        compiler_params=_cparams(("arbitrary",)),
        name="l0_sample",
    )(*args)


def _first_index_of_max(x, iota, size):
    m = jnp.max(x, axis=0, keepdims=True)
    f = jnp.min(jnp.where(x == m, iota, size), axis=0, keepdims=True)
    return m, f


def _router_kernel(h_ref, wrt_ref, bias_ref, tri_ref, idx_ref, w_ref, rank_ref, cnt_ref, packed_ref, run_ref):
    tm = h_ref.shape[0]
    E = N_EXPERTS
    PG = EXPERTS_PER_GROUP

    @pl.when(pl.program_id(0) == 0)
    def _():
        run_ref[...] = jnp.zeros_like(run_ref)

    hb = h_ref[...].astype(BF16)

    bits = lax.bitcast_convert_type(hb.astype(F32), U32)
    words = (bits[:, :PACKED_WORDS] >> 16) | bits[:, PACKED_WORDS:]
    for c in range(PACKED_SUBLANES):
        packed_ref[pl.ds(c, tm, stride=PACKED_SUBLANES), :] = words[:, c * LANES:(c + 1) * LANES]

    logits = lax.dot_general(wrt_ref[...], hb, (((1,), (1,)), ((), ())),
                             preferred_element_type=F32)
    scores = jax.nn.sigmoid(logits)
    sel = scores + bias_ref[...]

    io_g = lax.broadcasted_iota(I32, (PG, tm), 0)
    rows = []
    for g in range(N_EXPERT_GROUPS):
        blk = sel[g * PG:(g + 1) * PG, :]
        m1, f1 = _first_index_of_max(blk, io_g, PG)
        m2 = jnp.max(jnp.where(io_g == f1, -jnp.inf, blk), axis=0, keepdims=True)
        rows.append(m1 + m2)
    gs = jnp.concatenate(rows, axis=0)

    io_n = lax.broadcasted_iota(I32, (N_EXPERT_GROUPS, tm), 0)
    gsel = jnp.zeros((N_EXPERT_GROUPS, tm), F32)
    cur = gs
    for _ in range(TOPK_GROUPS):
        _, f = _first_index_of_max(cur, io_n, N_EXPERT_GROUPS)
        hit = io_n == f
        gsel = jnp.where(hit, 1.0, gsel)
        cur = jnp.where(hit, -jnp.inf, cur)
    masked = jnp.concatenate(
        [jnp.where(gsel[g:g + 1, :] > 0.5, sel[g * PG:(g + 1) * PG, :], -jnp.inf) for g in range(N_EXPERT_GROUPS)],
        axis=0)

    io_e = lax.broadcasted_iota(I32, (E, tm), 0)
    onehot = jnp.zeros((E, tm), F32)
    idx_rows, sc_rows = [], []
    cur = masked
    for _ in range(TOP_K):
        _, f = _first_index_of_max(cur, io_e, E)
        hit = io_e == f
        idx_rows.append(f)
        sc_rows.append(jnp.sum(jnp.where(hit, scores, 0.0), axis=0, keepdims=True))
        onehot = jnp.where(hit, 1.0, onehot)
        cur = jnp.where(hit, -jnp.inf, cur)
    sc = jnp.concatenate(sc_rows, axis=0)
    idx_ref[0] = jnp.concatenate(idx_rows, axis=0)
    w_ref[...] = sc / jnp.sum(sc, axis=0, keepdims=True) * ROUTED_SCALE

    before = jnp.dot(onehot.astype(BF16), tri_ref[...], preferred_element_type=F32) + run_ref[...]
    rank_rows = [jnp.sum(jnp.where(io_e == f, before, 0.0), axis=0, keepdims=True) for f in idx_rows]
    rank_ref[0] = jnp.concatenate(rank_rows, axis=0).astype(I32)
    run_ref[...] = run_ref[...] + jnp.sum(onehot, axis=1, keepdims=True)
    cnt_ref[...] = run_ref[...]


def _router(tok, wrt, bias):
    N, D = tok.shape
    tm = ROUTER_TILE
    assert N % tm == 0
    tri = (jnp.arange(tm)[:, None] < jnp.arange(tm)[None, :]).astype(BF16)
    return pl.pallas_call(
        _router_kernel,
        grid=(N // tm,),
        in_specs=[
            pl.BlockSpec((tm, D), lambda i: (i, 0)),
            _const_spec(wrt.shape), _const_spec(bias.shape), _const_spec(tri.shape),
        ],
        out_specs=[
            pl.BlockSpec((1, TOP_K, tm), lambda i: (i, 0, 0)),
            pl.BlockSpec((TOP_K, tm), lambda i: (0, i)),
            pl.BlockSpec((1, TOP_K, tm), lambda i: (i, 0, 0)),
            _const_spec((N_EXPERTS, 1)),
            pl.BlockSpec((tm * PACKED_SUBLANES, LANES), lambda i: (i, 0)),
        ],
        out_shape=[
            jax.ShapeDtypeStruct((N // tm, TOP_K, tm), I32),
            jax.ShapeDtypeStruct((TOP_K, N), F32),
            jax.ShapeDtypeStruct((N // tm, TOP_K, tm), I32),
            jax.ShapeDtypeStruct((N_EXPERTS, 1), F32),
            jax.ShapeDtypeStruct((N * PACKED_SUBLANES, LANES), U32),
        ],
        scratch_shapes=[pltpu.VMEM((N_EXPERTS, 1), F32)],
        compiler_params=_cparams(("arbitrary",)),
        name="moe_router",
    )(tok, wrt, bias, tri)


def _slots_kernel(idx_ref, rank_ref, pstart_ref, dest_ref):
    idx = idx_ref[0]
    tm = idx.shape[1]
    io_e = lax.broadcasted_iota(I32, (N_EXPERTS, tm), 0)
    start = pstart_ref[...]
    rows = [jnp.sum(jnp.where(io_e == idx[k:k + 1, :], start, 0.0), axis=0, keepdims=True) for k in range(TOP_K)]
    dest_ref[0] = jnp.concatenate(rows, axis=0).astype(I32) + rank_ref[0]


def _slots(idx3, rank3, pstart):
    nt, _, tm = idx3.shape
    spec = pl.BlockSpec((1, TOP_K, tm), lambda i: (i, 0, 0))
    return pl.pallas_call(
        _slots_kernel,
        grid=(nt,),
        in_specs=[spec, spec, _const_spec((N_EXPERTS, 1))],
        out_specs=spec,
        out_shape=jax.ShapeDtypeStruct(idx3.shape, I32),
        compiler_params=_cparams(("arbitrary",)),
        name="moe_slots",
    )(idx3, rank3, pstart.astype(F32).reshape(N_EXPERTS, 1))


def _load_slot_table(i, dest_hbm, dest_s, sem):
    copy = pltpu.make_async_copy(dest_hbm.at[i], dest_s, sem)
    copy.start()
    copy.wait()


def _dispatch_kernel(dest_hbm, pend_ref, cnt_ref, tok_ref, xs_hbm, dest_s, zero_ref, sem_idx, sem_rows):
    i = pl.program_id(0)
    tm = tok_ref.shape[0]
    blk = zero_ref.shape[0]

    @pl.when(i == 0)
    def _():
        zero_ref[...] = jnp.zeros_like(zero_ref)

        def zero_copy(e):
            last_block = pl.multiple_of(pend_ref[e] - blk, blk)
            return pltpu.make_async_copy(zero_ref, xs_hbm.at[pl.ds(last_block, blk)], sem_rows)

        def start(e, c):
            @pl.when(cnt_ref[e] > 0)
            def _():
                zero_copy(e).start()
            return c

        def wait(e, c):
            @pl.when(cnt_ref[e] > 0)
            def _():
                zero_copy(e).wait()
            return c

        lax.fori_loop(0, N_EXPERTS, start, 0)
        lax.fori_loop(0, N_EXPERTS, wait, 0)

    _load_slot_table(i, dest_hbm, dest_s, sem_idx)

    def start(t, c):
        for k in range(TOP_K):
            pltpu.make_async_copy(tok_ref.at[t], xs_hbm.at[dest_s[k, t]], sem_rows).start()
        return c

    def wait(t, c):
        for k in range(TOP_K):
            pltpu.make_async_copy(tok_ref.at[0], xs_hbm.at[0], sem_rows).wait()
        return c

    lax.fori_loop(0, tm, start, 0)
    lax.fori_loop(0, tm, wait, 0)


def _dispatch(rows, dest3, pend, counts, n_slots):
    N = rows.shape[0]
    tm = dest3.shape[2]
    assert N % tm == 0 and dest3.shape == (N // tm, TOP_K, tm)
    smem = pl.BlockSpec(memory_space=pltpu.SMEM)
    return pl.pallas_call(
        _dispatch_kernel,
        grid=(N // tm,),
        in_specs=[
            pl.BlockSpec(memory_space=pl.ANY), smem, smem,
            pl.BlockSpec((tm,) + rows.shape[1:], lambda i: (i, 0, 0)),
        ],
        out_specs=pl.BlockSpec(memory_space=pl.ANY),
        scratch_shapes=[
            pltpu.SMEM((TOP_K, tm), I32),
            pltpu.VMEM((EXPERT_BLOCK,) + rows.shape[1:], rows.dtype),
            pltpu.SemaphoreType.DMA,
            pltpu.SemaphoreType.DMA,
        ],
        out_shape=jax.ShapeDtypeStruct((n_slots,) + rows.shape[1:], rows.dtype),
        compiler_params=_cparams(("arbitrary",)),
        name="moe_dispatch",
    )(dest3, pend, counts, rows)


def _ffn_kernel(be_ref, nu_ref, buf_ref, nxt_ref, xs_ref, wg_hbm, wu_hbm, wd_hbm, ys_ref,
                wg_buf, wu_buf, wd_buf, wgu_s, wd_s, sems, *, layer):
    b = pl.program_id(0)
    used = b < nu_ref[0]
    prev = be_ref[jnp.maximum(b - 1, 0)]
    new_expert = jnp.logical_or(b == 0, be_ref[b] != prev)

    def weight_copies(e, s):
        return [pltpu.make_async_copy(wg_hbm.at[layer, e], wg_buf.at[s], sems.at[s]),
                pltpu.make_async_copy(wu_hbm.at[layer, e], wu_buf.at[s], sems.at[s]),
                pltpu.make_async_copy(wd_hbm.at[layer, e], wd_buf.at[s], sems.at[s])]

    @pl.when(jnp.logical_and(used, b == 0))
    def _():
        for c in weight_copies(be_ref[0], buf_ref[0]):
            c.start()

    @pl.when(jnp.logical_and(used, new_expert))
    def _():
        s = buf_ref[b]

        @pl.when(nxt_ref[b] >= 0)
        def _():
            for c in weight_copies(nxt_ref[b], 1 - s):
                c.start()

        for c in weight_copies(be_ref[b], s):
            c.wait()
        wgu_s[:, :D_EXPERT] = wg_buf[s].astype(BF16)
        wgu_s[:, D_EXPERT:] = wu_buf[s].astype(BF16)
        wd_s[...] = wd_buf[s].astype(BF16)

    @pl.when(used)
    def _():
        blk = xs_ref.shape[0] // PACKED_SUBLANES
        words = [xs_ref[pl.ds(c, blk, stride=PACKED_SUBLANES), :] for c in range(PACKED_SUBLANES)]
        low = [lax.bitcast_convert_type(w << 16, F32) for w in words]
        high = [lax.bitcast_convert_type(w & jnp.uint32(0xFFFF0000), F32) for w in words]
        x = jnp.concatenate(low + high, axis=1)
        gu = jnp.dot(x.astype(BF16), wgu_s[...], preferred_element_type=F32)
        gt = gu[:, :D_EXPERT]
        hid = gt * jax.nn.sigmoid(gt) * gu[:, D_EXPERT:]
        y = jnp.dot(hid.astype(BF16), wd_s[...], preferred_element_type=F32)
        for c in range(SUBLANES):
            ys_ref[pl.ds(c, blk, stride=SUBLANES), :] = y[:, c * LANES:(c + 1) * LANES]


def _expert_ffn(xs, block_e, n_used, layer, wg, wu, wd):
    n_slots = xs.shape[0]
    D = SUBLANES * LANES
    blk = EXPERT_BLOCK
    n_blocks = n_slots // blk

    def row_map(b, be, nu, buf, nxt):
        return (jnp.minimum(b, nu[0] - 1), 0)

    first = jnp.concatenate([jnp.ones((1,), bool), block_e[1:] != block_e[:-1]])
    buf = ((jnp.cumsum(first.astype(I32)) - 1) % 2).astype(I32)
    nxt_block = jnp.sum(block_e[None, :] <= block_e[:, None], axis=1).astype(I32)
    nxt = jnp.where(nxt_block < n_used[0], block_e[jnp.minimum(nxt_block, n_blocks - 1)], -1).astype(I32)

    hbm = pl.BlockSpec(memory_space=pl.ANY)
    return pl.pallas_call(
        functools.partial(_ffn_kernel, layer=layer),
        grid_spec=pltpu.PrefetchScalarGridSpec(
            num_scalar_prefetch=4,
            grid=(n_blocks,),
            in_specs=[pl.BlockSpec((blk * PACKED_SUBLANES, LANES), row_map), hbm, hbm, hbm],
            out_specs=pl.BlockSpec((blk * SUBLANES, LANES), row_map),
            scratch_shapes=[
                pltpu.VMEM((2, D, D_EXPERT), F32), pltpu.VMEM((2, D, D_EXPERT), F32), pltpu.VMEM((2, D_EXPERT, D), F32),
                pltpu.VMEM((D, 2 * D_EXPERT), BF16), pltpu.VMEM((D_EXPERT, D), BF16),
                pltpu.SemaphoreType.DMA((2,)),
            ],
        ),
        out_shape=jax.ShapeDtypeStruct((n_slots * SUBLANES, LANES), F32),
        compiler_params=_cparams(("arbitrary",)),
        name="moe_ffn",
    )(block_e, n_used, buf, nxt, xs.reshape(n_slots * PACKED_SUBLANES, LANES), wg, wu, wd
      ).reshape(n_slots, SUBLANES, LANES)


def _combine_kernel(dest_hbm, ys_hbm, tok_ref, w_ref, wsg_ref, wsu_ref, wsd_ref, g_ref, b_ref,
                    out_ref, dest_s, buf_ref, sem_idx, sem_rows):
    i = pl.program_id(0)
    tm = tok_ref.shape[0]

    _load_slot_table(i, dest_hbm, dest_s, sem_idx)

    def start(t, c):
        row = pl.multiple_of(t * SUBLANES, SUBLANES)
        for k in range(TOP_K):
            pltpu.make_async_copy(ys_hbm.at[dest_s[k, t]], buf_ref.at[k, pl.ds(row, SUBLANES)], sem_rows).start()
        return c

    def wait(t, c):
        for k in range(TOP_K):
            pltpu.make_async_copy(ys_hbm.at[0], buf_ref.at[0, pl.ds(0, SUBLANES)], sem_rows).wait()
        return c

    lax.fori_loop(0, tm, start, 0)

    h = tok_ref[...]
    hb = h.astype(BF16)
    gt = jnp.dot(hb, wsg_ref[...], preferred_element_type=F32)
    up = jnp.dot(hb, wsu_ref[...], preferred_element_type=F32)
    hid = gt * jax.nn.sigmoid(gt) * up
    f = jnp.dot(hid.astype(BF16), wsd_ref[...], preferred_element_type=F32)

    lax.fori_loop(0, tm, wait, 0)
    w = w_ref[...]
    wb = [jnp.broadcast_to(w[:, k:k + 1], (tm, LANES)) for k in range(TOP_K)]
    chunks = []
    for c in range(SUBLANES):
        acc = f[:, c * LANES:(c + 1) * LANES]
        for k in range(TOP_K):
            acc = acc + buf_ref[k, pl.ds(c, tm, stride=SUBLANES), :] * wb[k]
        chunks.append(acc)
    f = jnp.concatenate(chunks, axis=1)
    out_ref[...] = _layer_norm(DEEPNORM_ALPHA * h + f, g_ref[...], b_ref[...])


def _combine(tok, ys, dest3, w_tok, wsg, wsu, wsd, g, b):
    N, D = tok.shape
    tm = dest3.shape[2]
    assert N % tm == 0 and dest3.shape == (N // tm, TOP_K, tm)
    return pl.pallas_call(
        _combine_kernel,
        grid=(N // tm,),
        in_specs=[
            pl.BlockSpec(memory_space=pl.ANY),
            pl.BlockSpec(memory_space=pl.ANY),
            pl.BlockSpec((tm, D), lambda i: (i, 0)),
            pl.BlockSpec((tm, TOP_K), lambda i: (i, 0)),
            _const_spec(wsg.shape), _const_spec(wsu.shape), _const_spec(wsd.shape),
            _const_spec(g.shape), _const_spec(b.shape),
        ],
        out_specs=pl.BlockSpec((tm, D), lambda i: (i, 0)),
        scratch_shapes=[
            pltpu.SMEM((TOP_K, tm), I32),
            pltpu.VMEM((TOP_K, tm * SUBLANES, LANES), F32),
            pltpu.SemaphoreType.DMA,
            pltpu.SemaphoreType.DMA,
        ],
        out_shape=jax.ShapeDtypeStruct((N, D), F32),
        compiler_params=_cparams(("arbitrary",)),
        name="moe_combine",
    )(dest3, ys, tok, w_tok, wsg, wsu, wsd, g, b)


def _moe_layer(tok, layer, w_router, router_bias, we_gate, we_up, we_down, ws_gate, ws_up, ws_down, ln_g, ln_b):
    N, D = tok.shape
    blk = EXPERT_BLOCK
    idx3, w_t, rank3, cnt, packed = _router(tok, w_router.T.astype(BF16), router_bias.reshape(N_EXPERTS, 1))

    counts = cnt[:, 0].astype(I32)
    pcounts = (counts + blk - 1) // blk * blk
    pend = jnp.cumsum(pcounts).astype(I32)
    pstart = pend - pcounts
    n_blocks = N * TOP_K // blk + N_EXPERTS
    block_start = jnp.arange(n_blocks, dtype=I32) * blk
    block_e = jnp.minimum(jnp.sum(pend[None, :] <= block_start[:, None], axis=1), N_EXPERTS - 1).astype(I32)
    n_used = (pend[-1] // blk).reshape(1)

    dest3 = _slots(idx3, rank3, pstart)
    xs = _dispatch(packed.reshape(N, PACKED_SUBLANES, LANES), dest3, pend, counts, n_blocks * blk)
    ys = _expert_ffn(xs, block_e, n_used, layer, we_gate, we_up, we_down)
    return _combine(tok, ys, dest3, w_t.T,
                    ws_gate.astype(BF16), ws_up.astype(BF16), ws_down.astype(BF16),
                    ln_g.reshape(1, D), ln_b.reshape(1, D))


def _rel_bucket(dist):
    exact = N_BUCKETS // 2
    df = jnp.maximum(dist, 1).astype(F32)
    large = exact + (jnp.log(df / exact) / math.log(MAX_DISTANCE / exact) * (N_BUCKETS - exact)).astype(I32)
    large = jnp.minimum(large, N_BUCKETS - 1)
    return jnp.where(dist < exact, dist, large)


def _qkv_kernel(x_ref, perm_ref, w_ref, wkvt_ref, q_ref, k_ref, v_ref, kt_ref, vt_ref, *, dil, first_kept_tile,
                kept_cols):
    tl = x_ref.shape[0]
    chunk = tl // dil
    x = x_ref[...].astype(BF16)
    xp = x if dil == 1 else jnp.dot(perm_ref[...], x, preferred_element_type=F32).astype(BF16)
    y = jnp.dot(xp, w_ref[...], preferred_element_type=F32)
    for r in range(dil):
        rows = slice(r * chunk, (r + 1) * chunk)
        q_ref[r] = y[rows, :D_ATTN].astype(BF16)
        k_ref[r] = y[rows, D_ATTN:2 * D_ATTN].astype(BF16)
        v_ref[r] = y[rows, 2 * D_ATTN:].astype(BF16)

    @pl.when(pl.program_id(1) >= first_kept_tile)
    def _():
        kvt = lax.dot_general(wkvt_ref[...], x[tl - kept_cols:, :], (((1,), (1,)), ((), ())),
                              preferred_element_type=F32)
        kt_ref[...] = kvt[:D_ATTN]
        vt_ref[...] = kvt[D_ATTN:]


def _qkv_prompt(h, w_g, dil, keep):
    B, S, D = h.shape
    tl = min(S, 512)
    kept_cols = min(tl, keep)
    assert S % tl == 0 and tl % dil == 0 and keep % kept_cols == 0 and (S - keep) % kept_cols == 0
    first_kept_tile = (S - keep) // tl
    phase_spec = pl.BlockSpec((None, dil, tl // dil, D_ATTN), lambda b, t: (b, 0, t, 0))
    state_spec = pl.BlockSpec((None, D_ATTN, kept_cols), lambda b, t: (b, 0, jnp.maximum(t - first_kept_tile, 0)))
    wkvt = w_g[:, D_ATTN:].T
    dst = jnp.arange(tl, dtype=I32)
    src = (dst % (tl // dil)) * dil + dst // (tl // dil)
    perm = (src[:, None] == jnp.arange(tl, dtype=I32)[None, :]).astype(BF16)
    return pl.pallas_call(
        functools.partial(_qkv_kernel, dil=dil, first_kept_tile=first_kept_tile, kept_cols=kept_cols),
        grid=(B, S // tl),
        in_specs=[pl.BlockSpec((None, tl, D), lambda b, t: (b, t, 0)), _const_spec(perm.shape),
                  _const_spec(w_g.shape), _const_spec(wkvt.shape)],
        out_specs=[phase_spec, phase_spec, phase_spec, state_spec, state_spec],
        out_shape=[jax.ShapeDtypeStruct((B, dil, S // dil, D_ATTN), BF16)] * 3
        + [jax.ShapeDtypeStruct((B, D_ATTN, keep), F32)] * 2,
        compiler_params=_cparams(("arbitrary", "arbitrary")),
        name="qkv_prompt",
    )(h, perm, w_g, wkvt)


def _attn_prompt_kernel(q_ref, kp_ref, kc_ref, vp_ref, vc_ref, bias_ref, o_ref, lse_ref, *, steps):
    n = pl.program_id(1)
    qb = QUERY_BLOCK
    qi = lax.broadcasted_iota(I32, (qb, 2 * qb), 0)
    kj = lax.broadcasted_iota(I32, (qb, 2 * qb), 1)
    dist = qi + qb - kj
    valid = (dist >= 0) & (dist <= steps) & ((n > 0) | (kj >= qb))
    low = lax.broadcasted_iota(I32, (1, 2 * HEAD_DIM), 1) < HEAD_DIM
    scale = HEAD_DIM ** -0.5
    for p in range(HEADS // 2):
        c0 = p * 2 * HEAD_DIM
        q2 = q_ref[0, :, c0:c0 + 2 * HEAD_DIM]
        k2 = jnp.concatenate([kp_ref[0, :, c0:c0 + 2 * HEAD_DIM], kc_ref[0, :, c0:c0 + 2 * HEAD_DIM]], axis=0)
        v2 = jnp.concatenate([vp_ref[0, :, c0:c0 + 2 * HEAD_DIM], vc_ref[0, :, c0:c0 + 2 * HEAD_DIM]], axis=0)
        outs, lses = [], []
        for half in range(2):
            keep = low if half == 0 else jnp.logical_not(low)
            qh = jnp.where(keep, q2, jnp.zeros_like(q2))
            s = lax.dot_general(qh, k2, (((1,), (1,)), ((), ())), preferred_element_type=F32)
            s = s * scale + bias_ref[2 * p + half]
            s = jnp.where(valid, s, NEG_INF)
            m = jnp.max(s, axis=-1, keepdims=True)
            e = jnp.exp(s - m)
            l = jnp.sum(e, axis=-1, keepdims=True)
            prob = e / l
            outs.append(jnp.dot(prob.astype(BF16), v2, preferred_element_type=F32))
            lses.append(m + jnp.log(l))
        o_ref[0, :, c0:c0 + 2 * HEAD_DIM] = jnp.where(low, outs[0], outs[1])
        lse_ref[0, :, c0:c0 + 2 * HEAD_DIM] = jnp.where(low, lses[0], lses[1])


def _attn_prompt(q, k, v, bias, steps):
    B, dil, L, _ = q.shape
    qb = QUERY_BLOCK
    assert L % qb == 0
    cur = pl.BlockSpec((1, qb, D_ATTN), lambda bd, n: (bd, n, 0))
    prev = pl.BlockSpec((1, qb, D_ATTN), lambda bd, n: (bd, jnp.maximum(n - 1, 0), 0))
    out = pl.BlockSpec((1, qb, D_ATTN), lambda bd, n: (bd // dil, n, bd % dil))
    qv, kv, vv = (a.reshape(B * dil, L, D_ATTN) for a in (q, k, v))
    o, lse = pl.pallas_call(
        functools.partial(_attn_prompt_kernel, steps=steps),
        grid=(B * dil, L // qb),
        in_specs=[cur, prev, cur, prev, cur, _const_spec(bias.shape)],
        out_specs=[out, out],
        out_shape=[jax.ShapeDtypeStruct((B, L, dil * D_ATTN), F32)] * 2,
        compiler_params=_cparams(("arbitrary", "arbitrary")),
        name="attn_prompt",
    )(qv, kv, kv, vv, vv, bias)
    return o.reshape(B * L * dil, D_ATTN), lse.reshape(B * L * dil, D_ATTN)


def _merge_kernel(o1, o2, o3, l1, l2, l3, h_ref, wo_ref, g_ref, b_ref, out_ref):
    a1, a2, a3 = l1[...], l2[...], l3[...]
    m = jnp.maximum(jnp.maximum(a1, a2), a3)
    e1, e2, e3 = jnp.exp(a1 - m), jnp.exp(a2 - m), jnp.exp(a3 - m)
    o = (e1 * o1[...] + e2 * o2[...] + e3 * o3[...]) / (e1 + e2 + e3)
    y = jnp.dot(o.astype(BF16), wo_ref[...], preferred_element_type=F32)
    out_ref[...] = _layer_norm(DEEPNORM_ALPHA * h_ref[...] + y, g_ref[...], b_ref[...])


def _merge(os_, ls_, h, wo, g, b):
    M, D = h.shape
    tm = min(MERGE_TILE, M)
    assert M % tm == 0
    a_spec = pl.BlockSpec((tm, D_ATTN), lambda i: (i, 0))
    return pl.pallas_call(
        _merge_kernel,
        grid=(M // tm,),
        in_specs=[a_spec] * 6 + [pl.BlockSpec((tm, D), lambda i: (i, 0)), _const_spec(wo.shape),
                                 _const_spec(g.shape), _const_spec(b.shape)],
        out_specs=pl.BlockSpec((tm, D), lambda i: (i, 0)),
        out_shape=jax.ShapeDtypeStruct((M, D), F32),
        compiler_params=_cparams(("arbitrary",)),
        name="attn_merge",
    )(*os_, *ls_, h, wo, g, b)


def _mm_kernel(x_ref, w_ref, o_ref):
    o_ref[...] = jnp.dot(x_ref[...].astype(BF16), w_ref[...], preferred_element_type=F32)


def _matmul(x, w, tn):
    M, K = x.shape
    _, N = w.shape
    assert N % tn == 0
    return pl.pallas_call(
        _mm_kernel,
        grid=(N // tn,),
        in_specs=[_const_spec((M, K)), pl.BlockSpec((K, tn), lambda j: (0, j))],
        out_specs=pl.BlockSpec((M, tn), lambda j: (0, j)),
        out_shape=jax.ShapeDtypeStruct((M, N), F32),
        compiler_params=_cparams(("arbitrary",)),
        name="matmul",
    )(x, w)


def _round_bf16(x):
    return x.astype(BF16).astype(F32)


def _attn_sample_kernel(q_ref, kn_ref, vn_ref, kc_ref, vc_ref, bias_ref, b0_ref, o_ref, lse_ref):
    scale = HEAD_DIM ** -0.5
    tb = kc_ref.shape[0]
    eye = (lax.broadcasted_iota(I32, (HEAD_DIM, HEAD_DIM), 0)
           == lax.broadcasted_iota(I32, (HEAD_DIM, HEAD_DIM), 1)).astype(F32)

    def column(ref):
        rows = _round_bf16(ref[...])
        return jnp.sum(rows * eye[None], axis=2, keepdims=True).reshape(tb, HEADS, HEAD_DIM, 1)

    q = column(q_ref)
    kn = column(kn_ref)
    vn = column(vn_ref)
    kc = _round_bf16(kc_ref[...])
    s = jnp.sum(kc * q, axis=2, keepdims=True) * scale + bias_ref[...][None]
    s0 = jnp.sum(q * kn, axis=2, keepdims=True) * scale + b0_ref[...][None]
    m = jnp.maximum(jnp.max(s, axis=3, keepdims=True), s0)
    e = jnp.exp(s - m)
    e0 = jnp.exp(s0 - m)
    l = jnp.sum(e, axis=3, keepdims=True) + e0
    p = _round_bf16(e / l)
    p0 = _round_bf16(e0 / l)
    vc = _round_bf16(vc_ref[...])
    o = jnp.sum(vc * p, axis=3, keepdims=True) + p0 * vn
    o_ref[...] = jnp.sum(o.reshape(tb * HEADS, HEAD_DIM, 1) * eye[None], axis=1, keepdims=True)
    lse_ref[...] = (m + jnp.log(l)).reshape(tb * HEADS, 1, 1)


def _attn_sample(q, kn, vn, ckt, cvt, j, bias, bias_self, tb=1):
    DB = q.shape[0] // HEADS
    Wb = ckt.shape[3]
    assert DB % tb == 0
    new_spec = pl.BlockSpec((tb * HEADS, 1, HEAD_DIM), lambda i: (i, 0, 0))
    cache_spec = pl.BlockSpec((tb, HEADS, HEAD_DIM, Wb), lambda i: (j * (DB // tb) + i, 0, 0, 0))
    return pl.pallas_call(
        _attn_sample_kernel,
        grid=(DB // tb,),
        in_specs=[new_spec, new_spec, new_spec, cache_spec, cache_spec,
                  _const_spec(bias.shape), _const_spec(bias_self.shape)],
        out_specs=[new_spec, pl.BlockSpec((tb * HEADS, 1, 1), lambda i: (i, 0, 0))],
        out_shape=[jax.ShapeDtypeStruct((DB * HEADS, 1, HEAD_DIM), F32), jax.ShapeDtypeStruct((DB * HEADS, 1, 1), F32)],
        compiler_params=_cparams(("arbitrary",)),
        name="attn_sample",
    )(q, kn, vn, ckt, cvt, bias, bias_self)


def _bias_lookup(tab, buckets):
    onehot = (buckets[..., None] == jnp.arange(N_BUCKETS, dtype=I32)).astype(F32)
    return jnp.einsum('...n,nh->...h', onehot, tab, precision=lax.Precision.HIGHEST)


def _dilated_layer(hp, hs, caches, j, w_qkv, w_o, rel_bias, g1, b1):
    B, S, D = hp.shape
    DB = hs.shape[0]
    w_qkv_b = w_qkv.astype(BF16)
    w_o_b = w_o.astype(BF16)
    qb = QUERY_BLOCK
    qi = jnp.arange(qb, dtype=I32)[:, None]
    kj = jnp.arange(2 * qb, dtype=I32)[None, :]
    dist = qi + qb - kj
    qkv_s = _matmul(hs, w_qkv_b, 512)

    def sample_cols(part, g):
        c0 = (part * N_GROUPS + g) * D_ATTN
        return qkv_s[:, c0:c0 + D_ATTN].reshape(DB, HEADS, HEAD_DIM)

    def position_minor(c):
        return jnp.transpose(c, (0, 1, 3, 4, 2)).reshape(c.shape[0] * DB, HEADS, HEAD_DIM, c.shape[2])

    os_, ls_, kp, vp = [], [], [], []
    os_s, ls_s, ks, vs = [], [], [], []
    for g, (window, dil) in enumerate(DIL_CONFIGS):
        steps = window // dil
        assert steps == qb
        tab = rel_bias[:, g * HEADS:(g + 1) * HEADS]
        cols = [w_qkv_b[:, (part * N_GROUPS + g) * D_ATTN:(part * N_GROUPS + g + 1) * D_ATTN] for part in range(3)]
        keep = min(window, S)
        q, k, v, kt, vt = _qkv_prompt(hp, jnp.concatenate(cols, axis=1), dil, keep)
        bias = _bias_lookup(tab, _rel_bucket(jnp.maximum(dist, 0) * dil)).transpose(2, 0, 1)
        o, lse = _attn_prompt(q, k, v, bias, steps)
        os_.append(o)
        ls_.append(lse)
        kp.append(jnp.transpose(kt.reshape(B, HEADS, HEAD_DIM, keep), (0, 3, 1, 2)))
        vp.append(jnp.transpose(vt.reshape(B, HEADS, HEAD_DIM, keep), (0, 3, 1, 2)))
        ck, cv = caches[g]
        wb = ck.shape[2]
        assert wb == steps * dil and ck.shape[1] == DB
        pos = jnp.arange(wb, dtype=I32)
        cache_bias = jnp.where((pos % dil == 0)[:, None], _bias_lookup(tab, _rel_bucket(wb - pos)), NEG_INF)
        self_bias = _bias_lookup(tab, _rel_bucket(jnp.zeros((1,), I32)))
        q_s, k_s, v_s = sample_cols(0, g), sample_cols(1, g), sample_cols(2, g)
        head_rows = (DB * HEADS, 1, HEAD_DIM)
        o_s, lse_s = _attn_sample(q_s.reshape(head_rows), k_s.reshape(head_rows), v_s.reshape(head_rows),
                                  position_minor(ck), position_minor(cv), j, cache_bias.T[:, None, :],
                                  self_bias.T[:, :, None], tb=max(1, 1024 // wb))
        os_s.append(o_s.reshape(DB, D_ATTN))
        ls_s.append(jnp.broadcast_to(lse_s.reshape(DB, HEADS, 1), (DB, HEADS, HEAD_DIM)).reshape(DB, D_ATTN))
        ks.append(k_s[:, None])
        vs.append(v_s[:, None])
    h_p = _merge(os_, ls_, hp.reshape(B * S, D), w_o_b, g1, b1).reshape(B, S, D)
    h_s = _merge(os_s, ls_s, hs, w_o_b, g1, b1)
    return h_p, h_s, kp, vp, ks, vs


def _pool_conv_layer(hp, hs, state_pool, state_conv, w_in, pool_w, pool_scale, conv_w, conv_b, ln_g, ln_b, w_out,
                     g1, b1):
    params = (w_in.astype(BF16), pool_w.astype(BF16), pool_scale.reshape(1, D_POOL), conv_w,
              conv_b.reshape(1, D_CONV), ln_g.reshape(1, D_CONV), ln_b.reshape(1, D_CONV), w_out.astype(BF16), g1, b1)
    h_p, pst, cst = _l0_prompt(hp, *params)
    h_s, u_s, glu_s = _l0_sample(hs, state_pool, state_conv, *params)
    pool_p = pst[:, POOL_HALO - POOL_STATE:]
    conv_p = cst[:, CONV_HALO - CONV_STATE:]
    pool_s = jnp.concatenate([state_pool[:, 1:], u_s[:, None, :]], axis=1)
    conv_s = jnp.concatenate([state_conv[:, 1:], glu_s[:, None, :]], axis=1)
    return h_p, h_s, pool_p, conv_p, pool_s, conv_s


def kernel(x_prompt, x_sample, state_pool, state_conv, cache_k1, cache_v1, cache_k2, cache_v2, cache_k3, cache_v3,
           w_in_ab, pool_w, pool_scale, conv_w, conv_b, conv_ln_g, conv_ln_b, w_out_ab, w_qkv, w_o, rel_bias,
           ln1_g, ln1_b, ln2_g, ln2_b, w_router, router_bias, we_gate, we_up, we_down, ws_gate, ws_up, ws_down):
    B, S, D = x_prompt.shape
    DB, T, _ = x_sample.shape
    assert T == 1 and D == D_MODEL
    hp = x_prompt
    hs = x_sample.reshape(DB, D)
    caches_k = (cache_k1, cache_k2, cache_k3)
    caches_v = (cache_v1, cache_v2, cache_v3)
    pool_p, conv_p, pool_s, conv_s = [], [], [], []
    kp = [[] for _ in range(N_GROUPS)]
    vp = [[] for _ in range(N_GROUPS)]
    ksm = [[] for _ in range(N_GROUPS)]
    vsm = [[] for _ in range(N_GROUPS)]
    for layer in range(DEPTH):
        j = layer // 2
        g1 = ln1_g[layer].reshape(1, D)
        b1 = ln1_b[layer].reshape(1, D)
        if layer % 2 == 0:
            hp, hs, pp, cp, ps, cs = _pool_conv_layer(
                hp, hs, state_pool[j], state_conv[j], w_in_ab[j], pool_w[j], pool_scale[j], conv_w[j], conv_b[j],
                conv_ln_g[j], conv_ln_b[j], w_out_ab[j], g1, b1)
            pool_p.append(pp)
            conv_p.append(cp)
            pool_s.append(ps)
            conv_s.append(cs)
        else:
            caches = [(caches_k[g], caches_v[g]) for g in range(N_GROUPS)]
            hp, hs, nkp, nvp, nks, nvs = _dilated_layer(hp, hs, caches, j, w_qkv[j], w_o[j], rel_bias, g1, b1)
            for g in range(N_GROUPS):
                kp[g].append(nkp[g])
                vp[g].append(nvp[g])
                ksm[g].append(nks[g])
                vsm[g].append(nvs[g])
        tok = jnp.concatenate([hp.reshape(B * S, D), hs], axis=0)
        tok = _moe_layer(tok, layer, w_router[layer], router_bias[layer], we_gate, we_up, we_down,
                         ws_gate[layer], ws_up[layer], ws_down[layer], ln2_g[layer], ln2_b[layer])
        hp = tok[:B * S].reshape(B, S, D)
        hs = tok[B * S:]
    return (hp, hs.reshape(DB, T, D),
            jnp.stack(pool_p), jnp.stack(conv_p),
            jnp.stack(kp[0]), jnp.stack(vp[0]), jnp.stack(kp[1]), jnp.stack(vp[1]), jnp.stack(kp[2]), jnp.stack(vp[2]),
            jnp.stack(pool_s), jnp.stack(conv_s),
            jnp.stack(ksm[0]), jnp.stack(vsm[0]), jnp.stack(ksm[1]), jnp.stack(vsm[1]),
            jnp.stack(ksm[2]), jnp.stack(vsm[2]))
```

```python
import functools
import math

import jax
import jax.numpy as jnp
from jax import lax
from jax.experimental import pallas as pl
from jax.experimental.pallas import tpu as pltpu

F32 = jnp.float32
BF16 = jnp.bfloat16
I32 = jnp.int32
U32 = jnp.uint32

D_MODEL = 1024
D_POOL = 512
D_CONV = 512
POOL_WINDOWS = (2, 4, 8, 16)
POOL_GROUP = 128
POOL_STATE = 15
CONV_WIDTH = 31
CONV_STATE = 30
DIL_CONFIGS = ((128, 1), (512, 4), (2048, 16))
N_GROUPS = 3
HEADS = 8
HEAD_DIM = 64
D_ATTN = HEADS * HEAD_DIM
QUERY_BLOCK = 128
N_BUCKETS = 32
MAX_DISTANCE = 2048
N_EXPERTS = 256
TOP_K = 8
N_EXPERT_GROUPS = 8
EXPERTS_PER_GROUP = N_EXPERTS // N_EXPERT_GROUPS
TOPK_GROUPS = 4
D_EXPERT = 256
ROUTED_SCALE = 2.5
DEPTH = 2
DEEPNORM_ALPHA = (2.0 * DEPTH) ** 0.25
LN_EPS = 1e-5
NEG_INF = -1e30

LANES = 128
SUBLANES = 8
PACKED_WORDS = D_MODEL // 2
PACKED_SUBLANES = PACKED_WORDS // LANES
VMEM_LIMIT_BYTES = 56 * 1024 * 1024

POOL_HALO = 16
CONV_HALO = 32
L0_TIME_TILE = 256
ROUTER_TILE = 384
EXPERT_BLOCK = 256
MERGE_TILE = 512


def _cparams(sem):
    return pltpu.CompilerParams(dimension_semantics=sem, vmem_limit_bytes=VMEM_LIMIT_BYTES)


def _layer_norm(x, g, b):
    mu = jnp.mean(x, axis=-1, keepdims=True)
    xc = x - mu
    var = jnp.mean(xc * xc, axis=-1, keepdims=True)
    return xc * lax.rsqrt(var + LN_EPS) * g + b


def _const_spec(shape):
    nd = len(shape)
    return pl.BlockSpec(shape, lambda *_: (0,) * nd)


def _l0_prompt_kernel(x_ref, win_ref, pw_ref, ps_ref, cw_ref, cb_ref, cg_ref, cbeta_ref, wout_ref,
                      g1_ref, b1_ref, tok_hbm, h_ref, pstate_ref, cstate_ref, ue_ref, ge_ref, gs_ref):
    del tok_hbm
    tt = x_ref.shape[1]
    t = pl.program_id(1)

    @pl.when(t == 0)
    def _():
        ue_ref[0:POOL_HALO, :] = jnp.zeros((POOL_HALO, D_POOL), F32)
        ge_ref[0:CONV_HALO, :] = jnp.zeros((CONV_HALO, D_CONV), F32)

    @pl.when(t > 0)
    def _():
        ue_ref[0:POOL_HALO, :] = ue_ref[tt:tt + POOL_HALO, :]
        ge_ref[0:CONV_HALO, :] = ge_ref[tt:tt + CONV_HALO, :]

    x = x_ref[0]
    proj = jnp.dot(x.astype(BF16), win_ref[...], preferred_element_type=F32)
    u = proj[:, :D_POOL]
    a = proj[:, D_POOL:D_POOL + D_CONV]
    gate = proj[:, D_POOL + D_CONV:]
    glu = a * jax.nn.sigmoid(gate)
    ue_ref[POOL_HALO:POOL_HALO + tt, :] = u
    ge_ref[CONV_HALO:CONV_HALO + tt, :] = glu

    tg = t * tt + lax.broadcasted_iota(I32, (tt, 1), 0)
    parts = []
    for g, w in enumerate(POOL_WINDOWS):
        c0 = g * POOL_GROUP
        ug = u[:, c0:c0 + POOL_GROUP]
        s = ug
        for j in range(1, w):
            s = s + ue_ref[POOL_HALO - j:POOL_HALO - j + tt, c0:c0 + POOL_GROUP]
        cnt = jnp.minimum(tg + 1, w).astype(F32)
        pooled = s / cnt - ug
        parts.append(jnp.dot(pooled.astype(BF16), pw_ref[g], preferred_element_type=F32))
    yp = jnp.concatenate(parts, axis=1) * ps_ref[...]

    rows = tt + CONV_HALO - SUBLANES
    for s in range(1, SUBLANES):
        gs_ref[s, 0:rows, :] = ge_ref[s:s + rows, :]
    acc = glu * cw_ref[CONV_STATE:CONV_STATE + 1, :]
    off = CONV_HALO - CONV_STATE
    for j in range(CONV_STATE):
        s = (off + j) % SUBLANES
        base = off + j - s
        window = ge_ref[base:base + tt, :] if s == 0 else gs_ref[s, base:base + tt, :]
        acc = acc + window * cw_ref[j:j + 1, :]
    yn = _layer_norm(acc + cb_ref[...], cg_ref[...], cbeta_ref[...])
    yc = yn * jax.nn.sigmoid(yn)

    cat = jnp.concatenate([yp, yc], axis=1).astype(BF16)
    m = jnp.dot(cat, wout_ref[...], preferred_element_type=F32)
    h_ref[...] = _layer_norm(DEEPNORM_ALPHA * x + m, g1_ref[...], b1_ref[...])
    pstate_ref[0] = ue_ref[tt:tt + POOL_HALO, :]
    cstate_ref[0] = ge_ref[tt:tt + CONV_HALO, :]


def _l0_prompt(x, tok_buf, win, pw, ps, cw, cb, cg, cbeta, wout, g1, b1):
    B, S, D = x.shape
    n_rows = tok_buf.shape[0]
    tt = min(L0_TIME_TILE, S)
    assert S % tt == 0 and tt >= CONV_HALO and n_rows >= B * S
    return pl.pallas_call(
        _l0_prompt_kernel,
        grid=(B, S // tt),
        in_specs=[
            pl.BlockSpec((1, tt, D), lambda b, t: (b, t, 0)),
            _const_spec(win.shape), _const_spec(pw.shape), _const_spec(ps.shape), _const_spec(cw.shape),
            _const_spec(cb.shape), _const_spec(cg.shape), _const_spec(cbeta.shape), _const_spec(wout.shape),
            _const_spec(g1.shape), _const_spec(b1.shape), pl.BlockSpec(memory_space=pl.ANY),
        ],
        out_specs=[
            pl.BlockSpec((tt, D), lambda b, t: (b * (S // tt) + t, 0)),
            pl.BlockSpec((1, POOL_HALO, D_POOL), lambda b, t: (b, 0, 0)),
            pl.BlockSpec((1, CONV_HALO, D_CONV), lambda b, t: (b, 0, 0)),
        ],
        out_shape=[
            jax.ShapeDtypeStruct((n_rows, D), F32),
            jax.ShapeDtypeStruct((B, POOL_HALO, D_POOL), F32),
            jax.ShapeDtypeStruct((B, CONV_HALO, D_CONV), F32),
        ],
        scratch_shapes=[pltpu.VMEM((tt + POOL_HALO, D_POOL), F32), pltpu.VMEM((tt + CONV_HALO, D_CONV), F32),
                        pltpu.VMEM((SUBLANES, tt + CONV_HALO, D_CONV), F32)],
        input_output_aliases={11: 0},
        compiler_params=_cparams(("arbitrary", "arbitrary")),
        name="l0_prompt",
    )(x, win, pw, ps, cw, cb, cg, cbeta, wout, g1, b1, tok_buf)


def _l0_sample_kernel(x_ref, sp_ref, sc_ref, win_ref, pw_ref, ps_ref, cw_ref, cb_ref, cg_ref, cbeta_ref,
                      wout_ref, g1_ref, b1_ref, tok_hbm, h_ref, u_ref, glu_ref):
    del tok_hbm
    x = x_ref[...]
    proj = jnp.dot(x.astype(BF16), win_ref[...], preferred_element_type=F32)
    u = proj[:, :D_POOL]
    a = proj[:, D_POOL:D_POOL + D_CONV]
    gate = proj[:, D_POOL + D_CONV:]
    glu = a * jax.nn.sigmoid(gate)
    u_ref[...] = u
    glu_ref[...] = glu

    parts = []
    for g, w in enumerate(POOL_WINDOWS):
        c0 = g * POOL_GROUP
        ug = u[:, c0:c0 + POOL_GROUP]
        past = sp_ref[:, POOL_STATE - (w - 1):POOL_STATE, c0:c0 + POOL_GROUP]
        s = ug + jnp.sum(past, axis=1)
        pooled = s / float(w) - ug
        parts.append(jnp.dot(pooled.astype(BF16), pw_ref[g], preferred_element_type=F32))
    yp = jnp.concatenate(parts, axis=1) * ps_ref[...]

    acc = glu * cw_ref[CONV_STATE:CONV_STATE + 1, :]
    acc = acc + jnp.sum(sc_ref[...] * cw_ref[0:CONV_STATE, :][None, :, :], axis=1)
    yn = _layer_norm(acc + cb_ref[...], cg_ref[...], cbeta_ref[...])
    yc = yn * jax.nn.sigmoid(yn)

    cat = jnp.concatenate([yp, yc], axis=1).astype(BF16)
    m = jnp.dot(cat, wout_ref[...], preferred_element_type=F32)
    h_ref[...] = _layer_norm(DEEPNORM_ALPHA * x + m, g1_ref[...], b1_ref[...])


def _l0_sample(x, tok_buf, sp, sc, win, pw, ps, cw, cb, cg, cbeta, wout, g1, b1):
    DB, D = x.shape
    N = tok_buf.shape[0]
    assert (N - DB) % DB == 0
    args = (x, sp, sc, win, pw, ps, cw, cb, cg, cbeta, wout, g1, b1)
    return pl.pallas_call(
        _l0_sample_kernel,
        grid=(1,),
        in_specs=[_const_spec(a.shape) for a in args] + [pl.BlockSpec(memory_space=pl.ANY)],
        out_specs=[pl.BlockSpec((DB, D), lambda i: (N // DB - 1, 0)), _const_spec((DB, D_POOL)),
                   _const_spec((DB, D_CONV))],
        out_shape=[
            jax.ShapeDtypeStruct((N, D), F32),
            jax.ShapeDtypeStruct((DB, D_POOL), F32),
            jax.ShapeDtypeStruct((DB, D_CONV), F32),
        ],
        input_output_aliases={len(args): 0},
        compiler_params=_cparams(("arbitrary",)),
        name="l0_sample",
    )(*args, tok_buf)


def _first_index_of_max(x, iota, size):
    m = jnp.max(x, axis=0, keepdims=True)
    f = jnp.min(jnp.where(x == m, iota, size), axis=0, keepdims=True)
    return m, f


def _router_kernel(h_ref, wrt_ref, bias_ref, tri_ref, idx_ref, w_ref, rank_ref, cnt_ref, packed_ref, run_ref):
    tm = h_ref.shape[0]
    E = N_EXPERTS
    PG = EXPERTS_PER_GROUP

    @pl.when(pl.program_id(0) == 0)
    def _():
        run_ref[...] = jnp.zeros_like(run_ref)

    hb = h_ref[...].astype(BF16)

    bits = lax.bitcast_convert_type(hb.astype(F32), U32)
    words = (bits[:, :PACKED_WORDS] >> 16) | bits[:, PACKED_WORDS:]
    for c in range(PACKED_SUBLANES):
        packed_ref[pl.ds(c, tm, stride=PACKED_SUBLANES), :] = words[:, c * LANES:(c + 1) * LANES]

    logits = lax.dot_general(wrt_ref[...], hb, (((1,), (1,)), ((), ())),
                             preferred_element_type=F32)
    scores = jax.nn.sigmoid(logits)
    sel = scores + bias_ref[...]

    io_g = lax.broadcasted_iota(I32, (PG, tm), 0)
    rows = []
    for g in range(N_EXPERT_GROUPS):
        blk = sel[g * PG:(g + 1) * PG, :]
        m1, f1 = _first_index_of_max(blk, io_g, PG)
        m2 = jnp.max(jnp.where(io_g == f1, -jnp.inf, blk), axis=0, keepdims=True)
        rows.append(m1 + m2)
    gs = jnp.concatenate(rows, axis=0)

    io_n = lax.broadcasted_iota(I32, (N_EXPERT_GROUPS, tm), 0)
    gsel = jnp.zeros((N_EXPERT_GROUPS, tm), F32)
    cur = gs
    for _ in range(TOPK_GROUPS):
        _, f = _first_index_of_max(cur, io_n, N_EXPERT_GROUPS)
        hit = io_n == f
        gsel = jnp.where(hit, 1.0, gsel)
        cur = jnp.where(hit, -jnp.inf, cur)
    masked = jnp.concatenate(
        [jnp.where(gsel[g:g + 1, :] > 0.5, sel[g * PG:(g + 1) * PG, :], -jnp.inf) for g in range(N_EXPERT_GROUPS)],
        axis=0)

    io_e = lax.broadcasted_iota(I32, (E, tm), 0)
    onehot = jnp.zeros((E, tm), F32)
    idx_rows, sc_rows = [], []
    cur = masked
    for _ in range(TOP_K):
        _, f = _first_index_of_max(cur, io_e, E)
        hit = io_e == f
        idx_rows.append(f)
        sc_rows.append(jnp.sum(jnp.where(hit, scores, 0.0), axis=0, keepdims=True))
        onehot = jnp.where(hit, 1.0, onehot)
        cur = jnp.where(hit, -jnp.inf, cur)
    sc = jnp.concatenate(sc_rows, axis=0)
    idx_ref[0] = jnp.concatenate(idx_rows, axis=0)
    w_ref[...] = sc / jnp.sum(sc, axis=0, keepdims=True) * ROUTED_SCALE

    before = jnp.dot(onehot.astype(BF16), tri_ref[...], preferred_element_type=F32) + run_ref[...]
    rank_rows = [jnp.sum(jnp.where(io_e == f, before, 0.0), axis=0, keepdims=True) for f in idx_rows]
    rank_ref[0] = jnp.concatenate(rank_rows, axis=0).astype(I32)
    run_ref[...] = run_ref[...] + jnp.sum(onehot, axis=1, keepdims=True)
    cnt_ref[...] = run_ref[...]


def _router(tok, wrt, bias):
    N, D = tok.shape
    tm = ROUTER_TILE
    assert N % tm == 0
    tri = (jnp.arange(tm)[:, None] < jnp.arange(tm)[None, :]).astype(BF16)
    return pl.pallas_call(
        _router_kernel,
        grid=(N // tm,),
        in_specs=[
            pl.BlockSpec((tm, D), lambda i: (i, 0)),
            _const_spec(wrt.shape), _const_spec(bias.shape), _const_spec(tri.shape),
        ],
        out_specs=[
            pl.BlockSpec((1, TOP_K, tm), lambda i: (i, 0, 0)),
            pl.BlockSpec((TOP_K, tm), lambda i: (0, i)),
            pl.BlockSpec((1, TOP_K, tm), lambda i: (i, 0, 0)),
            _const_spec((N_EXPERTS, 1)),
            pl.BlockSpec((tm * PACKED_SUBLANES, LANES), lambda i: (i, 0)),
        ],
        out_shape=[
            jax.ShapeDtypeStruct((N // tm, TOP_K, tm), I32),
            jax.ShapeDtypeStruct((TOP_K, N), F32),
            jax.ShapeDtypeStruct((N // tm, TOP_K, tm), I32),
            jax.ShapeDtypeStruct((N_EXPERTS, 1), F32),
            jax.ShapeDtypeStruct((N * PACKED_SUBLANES, LANES), U32),
        ],
        scratch_shapes=[pltpu.VMEM((N_EXPERTS, 1), F32)],
        compiler_params=_cparams(("arbitrary",)),
        name="moe_router",
    )(tok, wrt, bias, tri)


def _slots_kernel(idx_ref, rank_ref, pstart_ref, dest_ref):
    idx = idx_ref[0]
    tm = idx.shape[1]
    io_e = lax.broadcasted_iota(I32, (N_EXPERTS, tm), 0)
    start = pstart_ref[...]
    rows = [jnp.sum(jnp.where(io_e == idx[k:k + 1, :], start, 0.0), axis=0, keepdims=True) for k in range(TOP_K)]
    dest_ref[0] = jnp.concatenate(rows, axis=0).astype(I32) + rank_ref[0]


def _slots(idx3, rank3, pstart):
    nt, _, tm = idx3.shape
    spec = pl.BlockSpec((1, TOP_K, tm), lambda i: (i, 0, 0))
    return pl.pallas_call(
        _slots_kernel,
        grid=(nt,),
        in_specs=[spec, spec, _const_spec((N_EXPERTS, 1))],
        out_specs=spec,
        out_shape=jax.ShapeDtypeStruct(idx3.shape, I32),
        compiler_params=_cparams(("arbitrary",)),
        name="moe_slots",
    )(idx3, rank3, pstart.astype(F32).reshape(N_EXPERTS, 1))


def _load_slot_table(i, dest_hbm, dest_s, sem):
    copy = pltpu.make_async_copy(dest_hbm.at[i], dest_s, sem)
    copy.start()
    copy.wait()


def _dispatch_kernel(dest_hbm, pend_ref, cnt_ref, tok_ref, xs_hbm, dest_s, zero_ref, sem_idx, sem_rows):
    i = pl.program_id(0)
    tm = tok_ref.shape[0]
    blk = zero_ref.shape[0]

    @pl.when(i == 0)
    def _():
        zero_ref[...] = jnp.zeros_like(zero_ref)

        def zero_copy(e):
            last_block = pl.multiple_of(pend_ref[e] - blk, blk)
            return pltpu.make_async_copy(zero_ref, xs_hbm.at[pl.ds(last_block, blk)], sem_rows)

        def start(e, c):
            @pl.when(cnt_ref[e] > 0)
            def _():
                zero_copy(e).start()
            return c

        def wait(e, c):
            @pl.when(cnt_ref[e] > 0)
            def _():
                zero_copy(e).wait()
            return c

        lax.fori_loop(0, N_EXPERTS, start, 0)
        lax.fori_loop(0, N_EXPERTS, wait, 0)

    _load_slot_table(i, dest_hbm, dest_s, sem_idx)

    def start(t, c):
        for k in range(TOP_K):
            pltpu.make_async_copy(tok_ref.at[t], xs_hbm.at[dest_s[k, t]], sem_rows).start()
        return c

    def wait(t, c):
        for k in range(TOP_K):
            pltpu.make_async_copy(tok_ref.at[0], xs_hbm.at[0], sem_rows).wait()
        return c

    lax.fori_loop(0, tm, start, 0)
    lax.fori_loop(0, tm, wait, 0)


def _dispatch(rows, dest3, pend, counts, n_slots):
    N = rows.shape[0]
    tm = dest3.shape[2]
    assert N % tm == 0 and dest3.shape == (N // tm, TOP_K, tm)
    smem = pl.BlockSpec(memory_space=pltpu.SMEM)
    return pl.pallas_call(
        _dispatch_kernel,
        grid=(N // tm,),
        in_specs=[
            pl.BlockSpec(memory_space=pl.ANY), smem, smem,
            pl.BlockSpec((tm,) + rows.shape[1:], lambda i: (i, 0, 0)),
        ],
        out_specs=pl.BlockSpec(memory_space=pl.ANY),
        scratch_shapes=[
            pltpu.SMEM((TOP_K, tm), I32),
            pltpu.VMEM((EXPERT_BLOCK,) + rows.shape[1:], rows.dtype),
            pltpu.SemaphoreType.DMA,
            pltpu.SemaphoreType.DMA,
        ],
        out_shape=jax.ShapeDtypeStruct((n_slots,) + rows.shape[1:], rows.dtype),
        compiler_params=_cparams(("arbitrary",)),
        name="moe_dispatch",
    )(dest3, pend, counts, rows)


def _ffn_kernel(be_ref, nu_ref, buf_ref, nxt_ref, xs_ref, wg_hbm, wu_hbm, wd_hbm, ys_ref,
                wg_buf, wu_buf, wd_buf, wgu_s, wd_s, sems, *, layer):
    b = pl.program_id(0)
    used = b < nu_ref[0]
    prev = be_ref[jnp.maximum(b - 1, 0)]
    new_expert = jnp.logical_or(b == 0, be_ref[b] != prev)

    def weight_copies(e, s):
        return [pltpu.make_async_copy(wg_hbm.at[layer, e], wg_buf.at[s], sems.at[s]),
                pltpu.make_async_copy(wu_hbm.at[layer, e], wu_buf.at[s], sems.at[s]),
                pltpu.make_async_copy(wd_hbm.at[layer, e], wd_buf.at[s], sems.at[s])]

    @pl.when(jnp.logical_and(used, b == 0))
    def _():
        for c in weight_copies(be_ref[0], buf_ref[0]):
            c.start()

    @pl.when(jnp.logical_and(used, new_expert))
    def _():
        s = buf_ref[b]

        @pl.when(nxt_ref[b] >= 0)
        def _():
            for c in weight_copies(nxt_ref[b], 1 - s):
                c.start()

        for c in weight_copies(be_ref[b], s):
            c.wait()
        wgu_s[:, :D_EXPERT] = wg_buf[s].astype(BF16)
        wgu_s[:, D_EXPERT:] = wu_buf[s].astype(BF16)
        wd_s[...] = wd_buf[s].astype(BF16)

    @pl.when(used)
    def _():
        blk = xs_ref.shape[0] // PACKED_SUBLANES
        words = [xs_ref[pl.ds(c, blk, stride=PACKED_SUBLANES), :] for c in range(PACKED_SUBLANES)]
        low = [lax.bitcast_convert_type(w << 16, F32) for w in words]
        high = [lax.bitcast_convert_type(w & jnp.uint32(0xFFFF0000), F32) for w in words]
        x = jnp.concatenate(low + high, axis=1)
        gu = jnp.dot(x.astype(BF16), wgu_s[...], preferred_element_type=F32)
        gt = gu[:, :D_EXPERT]
        hid = gt * jax.nn.sigmoid(gt) * gu[:, D_EXPERT:]
        y = jnp.dot(hid.astype(BF16), wd_s[...], preferred_element_type=F32)
        for c in range(SUBLANES):
            ys_ref[pl.ds(c, blk, stride=SUBLANES), :] = y[:, c * LANES:(c + 1) * LANES]


def _expert_ffn(xs, block_e, n_used, layer, wg, wu, wd):
    n_slots = xs.shape[0]
    D = SUBLANES * LANES
    blk = EXPERT_BLOCK
    n_blocks = n_slots // blk

    def row_map(b, be, nu, buf, nxt):
        return (jnp.minimum(b, nu[0] - 1), 0)

    first = jnp.concatenate([jnp.ones((1,), bool), block_e[1:] != block_e[:-1]])
    buf = ((jnp.cumsum(first.astype(I32)) - 1) % 2).astype(I32)
    nxt_block = jnp.sum(block_e[None, :] <= block_e[:, None], axis=1).astype(I32)
    nxt = jnp.where(nxt_block < n_used[0], block_e[jnp.minimum(nxt_block, n_blocks - 1)], -1).astype(I32)

    hbm = pl.BlockSpec(memory_space=pl.ANY)
    return pl.pallas_call(
        functools.partial(_ffn_kernel, layer=layer),
        grid_spec=pltpu.PrefetchScalarGridSpec(
            num_scalar_prefetch=4,
            grid=(n_blocks,),
            in_specs=[pl.BlockSpec((blk * PACKED_SUBLANES, LANES), row_map), hbm, hbm, hbm],
            out_specs=pl.BlockSpec((blk * SUBLANES, LANES), row_map),
            scratch_shapes=[
                pltpu.VMEM((2, D, D_EXPERT), F32), pltpu.VMEM((2, D, D_EXPERT), F32), pltpu.VMEM((2, D_EXPERT, D), F32),
                pltpu.VMEM((D, 2 * D_EXPERT), BF16), pltpu.VMEM((D_EXPERT, D), BF16),
                pltpu.SemaphoreType.DMA((2,)),
            ],
        ),
        out_shape=jax.ShapeDtypeStruct((n_slots * SUBLANES, LANES), F32),
        compiler_params=_cparams(("arbitrary",)),
        name="moe_ffn",
    )(block_e, n_used, buf, nxt, xs.reshape(n_slots * PACKED_SUBLANES, LANES), wg, wu, wd
      ).reshape(n_slots, SUBLANES, LANES)


def _combine_kernel(dest_hbm, ys_hbm, tok_ref, w_ref, wsg_ref, wsu_ref, wsd_ref, g_ref, b_ref,
                    out_ref, dest_s, buf_ref, sem_idx, sem_rows):
    i = pl.program_id(0)
    tm = tok_ref.shape[0]

    _load_slot_table(i, dest_hbm, dest_s, sem_idx)

    def start(t, c):
        row = pl.multiple_of(t * SUBLANES, SUBLANES)
        for k in range(TOP_K):
            pltpu.make_async_copy(ys_hbm.at[dest_s[k, t]], buf_ref.at[k, pl.ds(row, SUBLANES)], sem_rows).start()
        return c

    def wait(t, c):
        for k in range(TOP_K):
            pltpu.make_async_copy(ys_hbm.at[0], buf_ref.at[0, pl.ds(0, SUBLANES)], sem_rows).wait()
        return c

    lax.fori_loop(0, tm, start, 0)

    h = tok_ref[...]
    hb = h.astype(BF16)
    gt = jnp.dot(hb, wsg_ref[...], preferred_element_type=F32)
    up = jnp.dot(hb, wsu_ref[...], preferred_element_type=F32)
    hid = gt * jax.nn.sigmoid(gt) * up
    f = jnp.dot(hid.astype(BF16), wsd_ref[...], preferred_element_type=F32)

    lax.fori_loop(0, tm, wait, 0)
    w = w_ref[...]
    wb = [jnp.broadcast_to(w[:, k:k + 1], (tm, LANES)) for k in range(TOP_K)]
    chunks = []
    for c in range(SUBLANES):
        acc = f[:, c * LANES:(c + 1) * LANES]
        for k in range(TOP_K):
            acc = acc + buf_ref[k, pl.ds(c, tm, stride=SUBLANES), :] * wb[k]
        chunks.append(acc)
    f = jnp.concatenate(chunks, axis=1)
    out_ref[...] = _layer_norm(DEEPNORM_ALPHA * h + f, g_ref[...], b_ref[...])


def _combine(tok, ys, dest3, w_tok, wsg, wsu, wsd, g, b):
    N, D = tok.shape
    tm = dest3.shape[2]
    assert N % tm == 0 and dest3.shape == (N // tm, TOP_K, tm)
    return pl.pallas_call(
        _combine_kernel,
        grid=(N // tm,),
        in_specs=[
            pl.BlockSpec(memory_space=pl.ANY),
            pl.BlockSpec(memory_space=pl.ANY),
            pl.BlockSpec((tm, D), lambda i: (i, 0)),
            pl.BlockSpec((tm, TOP_K), lambda i: (i, 0)),
            _const_spec(wsg.shape), _const_spec(wsu.shape), _const_spec(wsd.shape),
            _const_spec(g.shape), _const_spec(b.shape),
        ],
        out_specs=pl.BlockSpec((tm, D), lambda i: (i, 0)),
        scratch_shapes=[
            pltpu.SMEM((TOP_K, tm), I32),
            pltpu.VMEM((TOP_K, tm * SUBLANES, LANES), F32),
            pltpu.SemaphoreType.DMA,
            pltpu.SemaphoreType.DMA,
        ],
        out_shape=jax.ShapeDtypeStruct((N, D), F32),
        compiler_params=_cparams(("arbitrary",)),
        name="moe_combine",
    )(dest3, ys, tok, w_tok, wsg, wsu, wsd, g, b)


def _moe_layer(tok, layer, w_router, router_bias, we_gate, we_up, we_down, ws_gate, ws_up, ws_down, ln_g, ln_b):
    N, D = tok.shape
    blk = EXPERT_BLOCK
    idx3, w_t, rank3, cnt, packed = _router(tok, w_router.T.astype(BF16), router_bias.reshape(N_EXPERTS, 1))

    counts = cnt[:, 0].astype(I32)
    pcounts = (counts + blk - 1) // blk * blk
    pend = jnp.cumsum(pcounts).astype(I32)
    pstart = pend - pcounts
    n_blocks = N * TOP_K // blk + N_EXPERTS
    block_start = jnp.arange(n_blocks, dtype=I32) * blk
    block_e = jnp.minimum(jnp.sum(pend[None, :] <= block_start[:, None], axis=1), N_EXPERTS - 1).astype(I32)
    n_used = (pend[-1] // blk).reshape(1)

    dest3 = _slots(idx3, rank3, pstart)
    xs = _dispatch(packed.reshape(N, PACKED_SUBLANES, LANES), dest3, pend, counts, n_blocks * blk)
    ys = _expert_ffn(xs, block_e, n_used, layer, we_gate, we_up, we_down)
    return _combine(tok, ys, dest3, w_t.T,
                    ws_gate.astype(BF16), ws_up.astype(BF16), ws_down.astype(BF16),
                    ln_g.reshape(1, D), ln_b.reshape(1, D))


def _rel_bucket(dist):
    exact = N_BUCKETS // 2
    df = jnp.maximum(dist, 1).astype(F32)
    large = exact + (jnp.log(df / exact) / math.log(MAX_DISTANCE / exact) * (N_BUCKETS - exact)).astype(I32)
    large = jnp.minimum(large, N_BUCKETS - 1)
    return jnp.where(dist < exact, dist, large)


def _qkv_kernel(x_ref, perm_ref, w_ref, wkvt_ref, q_ref, k_ref, v_ref, kt_ref, vt_ref, *, dil, first_kept_tile,
                kept_cols):
    tl = x_ref.shape[0]
    chunk = tl // dil
    x = x_ref[...].astype(BF16)
    xp = x if dil == 1 else jnp.dot(perm_ref[...], x, preferred_element_type=F32).astype(BF16)
    y = jnp.dot(xp, w_ref[...], preferred_element_type=F32)
    for r in range(dil):
        rows = slice(r * chunk, (r + 1) * chunk)
        q_ref[r] = y[rows, :D_ATTN].astype(BF16)
        k_ref[r] = y[rows, D_ATTN:2 * D_ATTN].astype(BF16)
        v_ref[r] = y[rows, 2 * D_ATTN:].astype(BF16)

    @pl.when(pl.program_id(1) >= first_kept_tile)
    def _():
        kvt = lax.dot_general(wkvt_ref[...], x[tl - kept_cols:, :], (((1,), (1,)), ((), ())),
                              preferred_element_type=F32)
        kt_ref[...] = kvt[:D_ATTN]
        vt_ref[...] = kvt[D_ATTN:]


def _qkv_prompt(tok, B, S, w_g, dil, keep):
    D = tok.shape[1]
    tl = min(S, 512)
    kept_cols = min(tl, keep)
    assert S % tl == 0 and tl % dil == 0 and keep % kept_cols == 0 and (S - keep) % kept_cols == 0
    first_kept_tile = (S - keep) // tl
    phase_spec = pl.BlockSpec((None, dil, tl // dil, D_ATTN), lambda b, t: (b, 0, t, 0))
    state_spec = pl.BlockSpec((None, D_ATTN, kept_cols), lambda b, t: (b, 0, jnp.maximum(t - first_kept_tile, 0)))
    wkvt = w_g[:, D_ATTN:].T
    dst = jnp.arange(tl, dtype=I32)
    src = (dst % (tl // dil)) * dil + dst // (tl // dil)
    perm = (src[:, None] == jnp.arange(tl, dtype=I32)[None, :]).astype(BF16)
    return pl.pallas_call(
        functools.partial(_qkv_kernel, dil=dil, first_kept_tile=first_kept_tile, kept_cols=kept_cols),
        grid=(B, S // tl),
        in_specs=[pl.BlockSpec((tl, D), lambda b, t: (b * (S // tl) + t, 0)), _const_spec(perm.shape),
                  _const_spec(w_g.shape), _const_spec(wkvt.shape)],
        out_specs=[phase_spec, phase_spec, phase_spec, state_spec, state_spec],
        out_shape=[jax.ShapeDtypeStruct((B, dil, S // dil, D_ATTN), BF16)] * 3
        + [jax.ShapeDtypeStruct((B, D_ATTN, keep), F32)] * 2,
        compiler_params=_cparams(("arbitrary", "arbitrary")),
        name="qkv_prompt",
    )(tok, perm, w_g, wkvt)


def _attn_prompt_kernel(q_ref, kp_ref, kc_ref, vp_ref, vc_ref, bias_ref, o_ref, lse_ref, *, steps):
    n = pl.program_id(1)
    qb = QUERY_BLOCK
    qi = lax.broadcasted_iota(I32, (qb, 2 * qb), 0)
    kj = lax.broadcasted_iota(I32, (qb, 2 * qb), 1)
    dist = qi + qb - kj
    valid = (dist >= 0) & (dist <= steps) & ((n > 0) | (kj >= qb))
    low = lax.broadcasted_iota(I32, (1, 2 * HEAD_DIM), 1) < HEAD_DIM
    scale = HEAD_DIM ** -0.5
    for p in range(HEADS // 2):
        c0 = p * 2 * HEAD_DIM
        q2 = q_ref[0, :, c0:c0 + 2 * HEAD_DIM]
        k2 = jnp.concatenate([kp_ref[0, :, c0:c0 + 2 * HEAD_DIM], kc_ref[0, :, c0:c0 + 2 * HEAD_DIM]], axis=0)
        v2 = jnp.concatenate([vp_ref[0, :, c0:c0 + 2 * HEAD_DIM], vc_ref[0, :, c0:c0 + 2 * HEAD_DIM]], axis=0)
        outs, lses = [], []
        for half in range(2):
            keep = low if half == 0 else jnp.logical_not(low)
            qh = jnp.where(keep, q2, jnp.zeros_like(q2))
            s = lax.dot_general(qh, k2, (((1,), (1,)), ((), ())), preferred_element_type=F32)
            s = s * scale + bias_ref[2 * p + half]
            s = jnp.where(valid, s, NEG_INF)
            m = jnp.max(s, axis=-1, keepdims=True)
            e = jnp.exp(s - m)
            l = jnp.sum(e, axis=-1, keepdims=True)
            prob = e / l
            outs.append(jnp.dot(prob.astype(BF16), v2, preferred_element_type=F32))
            lses.append(m + jnp.log(l))
        o_ref[0, :, c0:c0 + 2 * HEAD_DIM] = jnp.where(low, outs[0], outs[1])
        lse_ref[0, :, c0:c0 + 2 * HEAD_DIM] = jnp.where(low, lses[0], lses[1])


def _attn_prompt(q, k, v, bias, steps):
    B, dil, L, _ = q.shape
    qb = QUERY_BLOCK
    assert L % qb == 0
    cur = pl.BlockSpec((1, qb, D_ATTN), lambda bd, n: (bd, n, 0))
    prev = pl.BlockSpec((1, qb, D_ATTN), lambda bd, n: (bd, jnp.maximum(n - 1, 0), 0))
    out = pl.BlockSpec((1, qb, D_ATTN), lambda bd, n: (bd // dil, n, bd % dil))
    qv, kv, vv = (a.reshape(B * dil, L, D_ATTN) for a in (q, k, v))
    o, lse = pl.pallas_call(
        functools.partial(_attn_prompt_kernel, steps=steps),
        grid=(B * dil, L // qb),
        in_specs=[cur, prev, cur, prev, cur, _const_spec(bias.shape)],
        out_specs=[out, out],
        out_shape=[jax.ShapeDtypeStruct((B, L, dil * D_ATTN), F32)] * 2,
        compiler_params=_cparams(("arbitrary", "arbitrary")),
        name="attn_prompt",
    )(qv, kv, kv, vv, vv, bias)
    return o.reshape(B * L * dil, D_ATTN), lse.reshape(B * L * dil, D_ATTN)


def _merge_kernel(o1, o2, o3, l1, l2, l3, h_ref, wo_ref, g_ref, b_ref, out_ref):
    a1, a2, a3 = l1[...], l2[...], l3[...]
    m = jnp.maximum(jnp.maximum(a1, a2), a3)
    e1, e2, e3 = jnp.exp(a1 - m), jnp.exp(a2 - m), jnp.exp(a3 - m)
    o = (e1 * o1[...] + e2 * o2[...] + e3 * o3[...]) / (e1 + e2 + e3)
    y = jnp.dot(o.astype(BF16), wo_ref[...], preferred_element_type=F32)
    out_ref[...] = _layer_norm(DEEPNORM_ALPHA * h_ref[...] + y, g_ref[...], b_ref[...])


def _merge(os_, ls_, tok, row0, wo, g, b):
    M = os_[0].shape[0]
    N, D = tok.shape
    tm = math.gcd(MERGE_TILE, M, N)
    assert row0 % tm == 0
    a_spec = pl.BlockSpec((tm, D_ATTN), lambda i: (i, 0))
    row_spec = pl.BlockSpec((tm, D), lambda i: (row0 // tm + i, 0))
    args = (*os_, *ls_, tok, wo, g, b)
    return pl.pallas_call(
        _merge_kernel,
        grid=(M // tm,),
        in_specs=[a_spec] * 6 + [row_spec, _const_spec(wo.shape), _const_spec(g.shape), _const_spec(b.shape)],
        out_specs=row_spec,
        out_shape=jax.ShapeDtypeStruct((N, D), F32),
        input_output_aliases={len(os_) + len(ls_): 0},
        compiler_params=_cparams(("arbitrary",)),
        name="attn_merge",
    )(*args)


def _mm_kernel(x_ref, w_ref, o_ref):
    o_ref[...] = jnp.dot(x_ref[...].astype(BF16), w_ref[...], preferred_element_type=F32)


def _matmul(x, row0, M, w, tn):
    K = x.shape[1]
    _, N = w.shape
    assert N % tn == 0 and row0 % M == 0
    return pl.pallas_call(
        _mm_kernel,
        grid=(N // tn,),
        in_specs=[pl.BlockSpec((M, K), lambda j: (row0 // M, 0)), pl.BlockSpec((K, tn), lambda j: (0, j))],
        out_specs=pl.BlockSpec((M, tn), lambda j: (0, j)),
        out_shape=jax.ShapeDtypeStruct((M, N), F32),
        compiler_params=_cparams(("arbitrary",)),
        name="matmul",
    )(x, w)


def _round_bf16(x):
    return x.astype(BF16).astype(F32)


def _attn_sample_kernel(q_ref, kn_ref, vn_ref, kc_ref, vc_ref, bias_ref, b0_ref, o_ref, lse_ref):
    scale = HEAD_DIM ** -0.5
    tb = kc_ref.shape[0]
    eye = (lax.broadcasted_iota(I32, (HEAD_DIM, HEAD_DIM), 0)
           == lax.broadcasted_iota(I32, (HEAD_DIM, HEAD_DIM), 1)).astype(F32)

    def column(ref):
        rows = _round_bf16(ref[...])
        return jnp.sum(rows * eye[None], axis=2, keepdims=True).reshape(tb, HEADS, HEAD_DIM, 1)

    q = column(q_ref)
    kn = column(kn_ref)
    vn = column(vn_ref)
    kc = _round_bf16(kc_ref[...])
    s = jnp.sum(kc * q, axis=2, keepdims=True) * scale + bias_ref[...][None]
    s0 = jnp.sum(q * kn, axis=2, keepdims=True) * scale + b0_ref[...][None]
    m = jnp.maximum(jnp.max(s, axis=3, keepdims=True), s0)
    e = jnp.exp(s - m)
    e0 = jnp.exp(s0 - m)
    l = jnp.sum(e, axis=3, keepdims=True) + e0
    p = _round_bf16(e / l)
    p0 = _round_bf16(e0 / l)
    vc = _round_bf16(vc_ref[...])
    o = jnp.sum(vc * p, axis=3, keepdims=True) + p0 * vn
    o_ref[...] = jnp.sum(o.reshape(tb * HEADS, HEAD_DIM, 1) * eye[None], axis=1, keepdims=True)
    lse_ref[...] = (m + jnp.log(l)).reshape(tb * HEADS, 1, 1)


def _attn_sample(q, kn, vn, ckt, cvt, j, bias, bias_self, tb=1):
    DB = q.shape[0] // HEADS
    Wb = ckt.shape[3]
    assert DB % tb == 0
    new_spec = pl.BlockSpec((tb * HEADS, 1, HEAD_DIM), lambda i: (i, 0, 0))
    cache_spec = pl.BlockSpec((tb, HEADS, HEAD_DIM, Wb), lambda i: (j * (DB // tb) + i, 0, 0, 0))
    return pl.pallas_call(
        _attn_sample_kernel,
        grid=(DB // tb,),
        in_specs=[new_spec, new_spec, new_spec, cache_spec, cache_spec,
                  _const_spec(bias.shape), _const_spec(bias_self.shape)],
        out_specs=[new_spec, pl.BlockSpec((tb * HEADS, 1, 1), lambda i: (i, 0, 0))],
        out_shape=[jax.ShapeDtypeStruct((DB * HEADS, 1, HEAD_DIM), F32), jax.ShapeDtypeStruct((DB * HEADS, 1, 1), F32)],
        compiler_params=_cparams(("arbitrary",)),
        name="attn_sample",
    )(q, kn, vn, ckt, cvt, bias, bias_self)


def _bias_lookup(tab, buckets):
    onehot = (buckets[..., None] == jnp.arange(N_BUCKETS, dtype=I32)).astype(F32)
    return jnp.einsum('...n,nh->...h', onehot, tab, precision=lax.Precision.HIGHEST)


def _dilated_layer(tok, B, S, caches, j, w_qkv, w_o, rel_bias, g1, b1):
    DB = tok.shape[0] - B * S
    w_qkv_b = w_qkv.astype(BF16)
    w_o_b = w_o.astype(BF16)
    qb = QUERY_BLOCK
    qi = jnp.arange(qb, dtype=I32)[:, None]
    kj = jnp.arange(2 * qb, dtype=I32)[None, :]
    dist = qi + qb - kj
    qkv_s = _matmul(tok, B * S, DB, w_qkv_b, 512)

    def sample_cols(part, g):
        c0 = (part * N_GROUPS + g) * D_ATTN
        return qkv_s[:, c0:c0 + D_ATTN].reshape(DB, HEADS, HEAD_DIM)

    def position_minor(c):
        return jnp.transpose(c, (0, 1, 3, 4, 2)).reshape(c.shape[0] * DB, HEADS, HEAD_DIM, c.shape[2])

    os_, ls_, kp, vp = [], [], [], []
    os_s, ls_s, ks, vs = [], [], [], []
    for g, (window, dil) in enumerate(DIL_CONFIGS):
        steps = window // dil
        assert steps == qb
        tab = rel_bias[:, g * HEADS:(g + 1) * HEADS]
        cols = [w_qkv_b[:, (part * N_GROUPS + g) * D_ATTN:(part * N_GROUPS + g + 1) * D_ATTN] for part in range(3)]
        keep = min(window, S)
        q, k, v, kt, vt = _qkv_prompt(tok, B, S, jnp.concatenate(cols, axis=1), dil, keep)
        bias = _bias_lookup(tab, _rel_bucket(jnp.maximum(dist, 0) * dil)).transpose(2, 0, 1)
        o, lse = _attn_prompt(q, k, v, bias, steps)
        os_.append(o)
        ls_.append(lse)
        kp.append(jnp.transpose(kt.reshape(B, HEADS, HEAD_DIM, keep), (0, 3, 1, 2)))
        vp.append(jnp.transpose(vt.reshape(B, HEADS, HEAD_DIM, keep), (0, 3, 1, 2)))
        ck, cv = caches[g]
        wb = ck.shape[2]
        assert wb == steps * dil and ck.shape[1] == DB
        pos = jnp.arange(wb, dtype=I32)
        cache_bias = jnp.where((pos % dil == 0)[:, None], _bias_lookup(tab, _rel_bucket(wb - pos)), NEG_INF)
        self_bias = _bias_lookup(tab, _rel_bucket(jnp.zeros((1,), I32)))
        q_s, k_s, v_s = sample_cols(0, g), sample_cols(1, g), sample_cols(2, g)
        head_rows = (DB * HEADS, 1, HEAD_DIM)
        o_s, lse_s = _attn_sample(q_s.reshape(head_rows), k_s.reshape(head_rows), v_s.reshape(head_rows),
                                  position_minor(ck), position_minor(cv), j, cache_bias.T[:, None, :],
                                  self_bias.T[:, :, None], tb=max(1, 1024 // wb))
        os_s.append(o_s.reshape(DB, D_ATTN))
        ls_s.append(jnp.broadcast_to(lse_s.reshape(DB, HEADS, 1), (DB, HEADS, HEAD_DIM)).reshape(DB, D_ATTN))
        ks.append(k_s[:, None])
        vs.append(v_s[:, None])
    tok = _merge(os_, ls_, tok, 0, w_o_b, g1, b1)
    tok = _merge(os_s, ls_s, tok, B * S, w_o_b, g1, b1)
    return tok, kp, vp, ks, vs


def _pool_conv_layer(hp, hs, state_pool, state_conv, w_in, pool_w, pool_scale, conv_w, conv_b, ln_g, ln_b, w_out,
                     g1, b1):
    params = (w_in.astype(BF16), pool_w.astype(BF16), pool_scale.reshape(1, D_POOL), conv_w,
              conv_b.reshape(1, D_CONV), ln_g.reshape(1, D_CONV), ln_b.reshape(1, D_CONV), w_out.astype(BF16), g1, b1)
    B, S, _ = hp.shape
    tok = jnp.zeros((B * S + hs.shape[0], hp.shape[2]), F32)
    tok, pst, cst = _l0_prompt(hp, tok, *params)
    tok, u_s, glu_s = _l0_sample(hs, tok, state_pool, state_conv, *params)
    pool_p = pst[:, POOL_HALO - POOL_STATE:]
    conv_p = cst[:, CONV_HALO - CONV_STATE:]
    pool_s = jnp.concatenate([state_pool[:, 1:], u_s[:, None, :]], axis=1)
    conv_s = jnp.concatenate([state_conv[:, 1:], glu_s[:, None, :]], axis=1)
    return tok, pool_p, conv_p, pool_s, conv_s


def kernel(x_prompt, x_sample, state_pool, state_conv, cache_k1, cache_v1, cache_k2, cache_v2, cache_k3, cache_v3,
           w_in_ab, pool_w, pool_scale, conv_w, conv_b, conv_ln_g, conv_ln_b, w_out_ab, w_qkv, w_o, rel_bias,
           ln1_g, ln1_b, ln2_g, ln2_b, w_router, router_bias, we_gate, we_up, we_down, ws_gate, ws_up, ws_down):
    B, S, D = x_prompt.shape
    DB, T, _ = x_sample.shape
    assert T == 1 and D == D_MODEL
    tok = None
    caches_k = (cache_k1, cache_k2, cache_k3)
    caches_v = (cache_v1, cache_v2, cache_v3)
    pool_p, conv_p, pool_s, conv_s = [], [], [], []
    kp = [[] for _ in range(N_GROUPS)]
    vp = [[] for _ in range(N_GROUPS)]
    ksm = [[] for _ in range(N_GROUPS)]
    vsm = [[] for _ in range(N_GROUPS)]
    for layer in range(DEPTH):
        j = layer // 2
        g1 = ln1_g[layer].reshape(1, D)
        b1 = ln1_b[layer].reshape(1, D)
        if layer % 2 == 0:
            hp = x_prompt if tok is None else tok[:B * S].reshape(B, S, D)
            hs = x_sample.reshape(DB, D) if tok is None else tok[B * S:]
            tok, pp, cp, ps, cs = _pool_conv_layer(
                hp, hs, state_pool[j], state_conv[j], w_in_ab[j], pool_w[j], pool_scale[j], conv_w[j], conv_b[j],
                conv_ln_g[j], conv_ln_b[j], w_out_ab[j], g1, b1)
            pool_p.append(pp)
            conv_p.append(cp)
            pool_s.append(ps)
            conv_s.append(cs)
        else:
            caches = [(caches_k[g], caches_v[g]) for g in range(N_GROUPS)]
            tok, nkp, nvp, nks, nvs = _dilated_layer(tok, B, S, caches, j, w_qkv[j], w_o[j], rel_bias, g1, b1)
            for g in range(N_GROUPS):
                kp[g].append(nkp[g])
                vp[g].append(nvp[g])
                ksm[g].append(nks[g])
                vsm[g].append(nvs[g])
        tok = _moe_layer(tok, layer, w_router[layer], router_bias[layer], we_gate, we_up, we_down,
                         ws_gate[layer], ws_up[layer], ws_down[layer], ln2_g[layer], ln2_b[layer])
    return (tok[:B * S].reshape(B, S, D), tok[B * S:].reshape(DB, T, D),
            jnp.stack(pool_p), jnp.stack(conv_p),
            jnp.stack(kp[0]), jnp.stack(vp[0]), jnp.stack(kp[1]), jnp.stack(vp[1]), jnp.stack(kp[2]), jnp.stack(vp[2]),
            jnp.stack(pool_s), jnp.stack(conv_s),
            jnp.stack(ksm[0]), jnp.stack(vsm[0]), jnp.stack(ksm[1]), jnp.stack(vsm[1]),
            jnp.stack(ksm[2]), jnp.stack(vsm[2]))
```

```python
import functools
import math

import jax
import jax.numpy as jnp
from jax import lax
from jax.experimental import pallas as pl
from jax.experimental.pallas import tpu as pltpu

F32 = jnp.float32
BF16 = jnp.bfloat16
I32 = jnp.int32
U32 = jnp.uint32

D_MODEL = 1024
D_POOL = 512
D_CONV = 512
POOL_WINDOWS = (2, 4, 8, 16)
POOL_GROUP = 128
POOL_STATE = 15
CONV_WIDTH = 31
CONV_STATE = 30
DIL_CONFIGS = ((128, 1), (512, 4), (2048, 16))
N_GROUPS = 3
HEADS = 8
HEAD_DIM = 64
D_ATTN = HEADS * HEAD_DIM
QUERY_BLOCK = 128
N_BUCKETS = 32
MAX_DISTANCE = 2048
N_EXPERTS = 256
TOP_K = 8
N_EXPERT_GROUPS = 8
EXPERTS_PER_GROUP = N_EXPERTS // N_EXPERT_GROUPS
TOPK_GROUPS = 4
D_EXPERT = 256
ROUTED_SCALE = 2.5
DEPTH = 2
DEEPNORM_ALPHA = (2.0 * DEPTH) ** 0.25
LN_EPS = 1e-5
NEG_INF = -1e30

LANES = 128
SUBLANES = 8
PACKED_WORDS = D_MODEL // 2
PACKED_SUBLANES = PACKED_WORDS // LANES
VMEM_LIMIT_BYTES = 56 * 1024 * 1024

POOL_HALO = 16
CONV_HALO = 32
L0_TIME_TILE = 256
ROUTER_TILE = 384
EXPERT_BLOCK = 256
MERGE_TILE = 512
ATTN_STREAMS = 2


def _cparams(sem):
    return pltpu.CompilerParams(dimension_semantics=sem, vmem_limit_bytes=VMEM_LIMIT_BYTES)


def _layer_norm(x, g, b):
    mu = jnp.mean(x, axis=-1, keepdims=True)
    xc = x - mu
    var = jnp.mean(xc * xc, axis=-1, keepdims=True)
    return xc * lax.rsqrt(var + LN_EPS) * g + b


def _const_spec(shape):
    nd = len(shape)
    return pl.BlockSpec(shape, lambda *_: (0,) * nd)


def _l0_prompt_kernel(x_ref, win_ref, pw_ref, ps_ref, cw_ref, cb_ref, cg_ref, cbeta_ref, wout_ref,
                      g1_ref, b1_ref, tok_hbm, h_ref, pstate_ref, cstate_ref, ue_ref, ge_ref, gs_ref):
    del tok_hbm
    tt = x_ref.shape[1]
    t = pl.program_id(1)

    @pl.when(t == 0)
    def _():
        ue_ref[0:POOL_HALO, :] = jnp.zeros((POOL_HALO, D_POOL), F32)
        ge_ref[0:CONV_HALO, :] = jnp.zeros((CONV_HALO, D_CONV), F32)

    @pl.when(t > 0)
    def _():
        ue_ref[0:POOL_HALO, :] = ue_ref[tt:tt + POOL_HALO, :]
        ge_ref[0:CONV_HALO, :] = ge_ref[tt:tt + CONV_HALO, :]

    x = x_ref[0]
    proj = jnp.dot(x.astype(BF16), win_ref[...], preferred_element_type=F32)
    u = proj[:, :D_POOL]
    a = proj[:, D_POOL:D_POOL + D_CONV]
    gate = proj[:, D_POOL + D_CONV:]
    glu = a * jax.nn.sigmoid(gate)
    ue_ref[POOL_HALO:POOL_HALO + tt, :] = u
    ge_ref[CONV_HALO:CONV_HALO + tt, :] = glu

    tg = t * tt + lax.broadcasted_iota(I32, (tt, 1), 0)
    parts = []
    for g, w in enumerate(POOL_WINDOWS):
        c0 = g * POOL_GROUP
        ug = u[:, c0:c0 + POOL_GROUP]
        s = ug
        for j in range(1, w):
            s = s + ue_ref[POOL_HALO - j:POOL_HALO - j + tt, c0:c0 + POOL_GROUP]
        cnt = jnp.minimum(tg + 1, w).astype(F32)
        pooled = s / cnt - ug
        parts.append(jnp.dot(pooled.astype(BF16), pw_ref[g], preferred_element_type=F32))
    yp = jnp.concatenate(parts, axis=1) * ps_ref[...]

    rows = tt + CONV_HALO - SUBLANES
    for s in range(1, SUBLANES):
        gs_ref[s, 0:rows, :] = ge_ref[s:s + rows, :]
    acc = glu * cw_ref[CONV_STATE:CONV_STATE + 1, :]
    off = CONV_HALO - CONV_STATE
    for j in range(CONV_STATE):
        s = (off + j) % SUBLANES
        base = off + j - s
        window = ge_ref[base:base + tt, :] if s == 0 else gs_ref[s, base:base + tt, :]
        acc = acc + window * cw_ref[j:j + 1, :]
    yn = _layer_norm(acc + cb_ref[...], cg_ref[...], cbeta_ref[...])
    yc = yn * jax.nn.sigmoid(yn)

    cat = jnp.concatenate([yp, yc], axis=1).astype(BF16)
    m = jnp.dot(cat, wout_ref[...], preferred_element_type=F32)
    h_ref[...] = _layer_norm(DEEPNORM_ALPHA * x + m, g1_ref[...], b1_ref[...])
    pstate_ref[0] = ue_ref[tt:tt + POOL_HALO, :]
    cstate_ref[0] = ge_ref[tt:tt + CONV_HALO, :]


def _l0_prompt(x, tok_buf, win, pw, ps, cw, cb, cg, cbeta, wout, g1, b1):
    B, S, D = x.shape
    n_rows = tok_buf.shape[0]
    tt = min(L0_TIME_TILE, S)
    assert S % tt == 0 and tt >= CONV_HALO and n_rows >= B * S
    return pl.pallas_call(
        _l0_prompt_kernel,
        grid=(B, S // tt),
        in_specs=[
            pl.BlockSpec((1, tt, D), lambda b, t: (b, t, 0)),
            _const_spec(win.shape), _const_spec(pw.shape), _const_spec(ps.shape), _const_spec(cw.shape),
            _const_spec(cb.shape), _const_spec(cg.shape), _const_spec(cbeta.shape), _const_spec(wout.shape),
            _const_spec(g1.shape), _const_spec(b1.shape), pl.BlockSpec(memory_space=pl.ANY),
        ],
        out_specs=[
            pl.BlockSpec((tt, D), lambda b, t: (b * (S // tt) + t, 0)),
            pl.BlockSpec((1, POOL_HALO, D_POOL), lambda b, t: (b, 0, 0)),
            pl.BlockSpec((1, CONV_HALO, D_CONV), lambda b, t: (b, 0, 0)),
        ],
        out_shape=[
            jax.ShapeDtypeStruct((n_rows, D), F32),
            jax.ShapeDtypeStruct((B, POOL_HALO, D_POOL), F32),
            jax.ShapeDtypeStruct((B, CONV_HALO, D_CONV), F32),
        ],
        scratch_shapes=[pltpu.VMEM((tt + POOL_HALO, D_POOL), F32), pltpu.VMEM((tt + CONV_HALO, D_CONV), F32),
                        pltpu.VMEM((SUBLANES, tt + CONV_HALO, D_CONV), F32)],
        input_output_aliases={11: 0},
        compiler_params=_cparams(("arbitrary", "arbitrary")),
        name="l0_prompt",
    )(x, win, pw, ps, cw, cb, cg, cbeta, wout, g1, b1, tok_buf)


def _l0_sample_kernel(x_ref, sp_ref, sc_ref, win_ref, pw_ref, ps_ref, cw_ref, cb_ref, cg_ref, cbeta_ref,
                      wout_ref, g1_ref, b1_ref, tok_hbm, h_ref, u_ref, glu_ref):
    del tok_hbm
    x = x_ref[...]
    proj = jnp.dot(x.astype(BF16), win_ref[...], preferred_element_type=F32)
    u = proj[:, :D_POOL]
    a = proj[:, D_POOL:D_POOL + D_CONV]
    gate = proj[:, D_POOL + D_CONV:]
    glu = a * jax.nn.sigmoid(gate)
    u_ref[...] = u
    glu_ref[...] = glu

    parts = []
    for g, w in enumerate(POOL_WINDOWS):
        c0 = g * POOL_GROUP
        ug = u[:, c0:c0 + POOL_GROUP]
        past = sp_ref[:, POOL_STATE - (w - 1):POOL_STATE, c0:c0 + POOL_GROUP]
        s = ug + jnp.sum(past, axis=1)
        pooled = s / float(w) - ug
        parts.append(jnp.dot(pooled.astype(BF16), pw_ref[g], preferred_element_type=F32))
    yp = jnp.concatenate(parts, axis=1) * ps_ref[...]

    acc = glu * cw_ref[CONV_STATE:CONV_STATE + 1, :]
    acc = acc + jnp.sum(sc_ref[...] * cw_ref[0:CONV_STATE, :][None, :, :], axis=1)
    yn = _layer_norm(acc + cb_ref[...], cg_ref[...], cbeta_ref[...])
    yc = yn * jax.nn.sigmoid(yn)

    cat = jnp.concatenate([yp, yc], axis=1).astype(BF16)
    m = jnp.dot(cat, wout_ref[...], preferred_element_type=F32)
    h_ref[...] = _layer_norm(DEEPNORM_ALPHA * x + m, g1_ref[...], b1_ref[...])


def _l0_sample(x, tok_buf, sp, sc, win, pw, ps, cw, cb, cg, cbeta, wout, g1, b1):
    DB, D = x.shape
    N = tok_buf.shape[0]
    assert (N - DB) % DB == 0
    args = (x, sp, sc, win, pw, ps, cw, cb, cg, cbeta, wout, g1, b1)
    return pl.pallas_call(
        _l0_sample_kernel,
        grid=(1,),
        in_specs=[_const_spec(a.shape) for a in args] + [pl.BlockSpec(memory_space=pl.ANY)],
        out_specs=[pl.BlockSpec((DB, D), lambda i: (N // DB - 1, 0)), _const_spec((DB, D_POOL)),
                   _const_spec((DB, D_CONV))],
        out_shape=[
            jax.ShapeDtypeStruct((N, D), F32),
            jax.ShapeDtypeStruct((DB, D_POOL), F32),
            jax.ShapeDtypeStruct((DB, D_CONV), F32),
        ],
        input_output_aliases={len(args): 0},
        compiler_params=_cparams(("arbitrary",)),
        name="l0_sample",
    )(*args, tok_buf)


def _first_index_of_max(x, iota, size):
    m = jnp.max(x, axis=0, keepdims=True)
    f = jnp.min(jnp.where(x == m, iota, size), axis=0, keepdims=True)
    return m, f


def _router_kernel(h_ref, wrt_ref, bias_ref, tri_ref, idx_ref, w_ref, rank_ref, cnt_ref, packed_ref, run_ref):
    tm = h_ref.shape[0]
    E = N_EXPERTS
    PG = EXPERTS_PER_GROUP

    @pl.when(pl.program_id(0) == 0)
    def _():
        run_ref[...] = jnp.zeros_like(run_ref)

    hb = h_ref[...].astype(BF16)

    bits = lax.bitcast_convert_type(hb.astype(F32), U32)
    words = (bits[:, :PACKED_WORDS] >> 16) | bits[:, PACKED_WORDS:]
    for c in range(PACKED_SUBLANES):
        packed_ref[pl.ds(c, tm, stride=PACKED_SUBLANES), :] = words[:, c * LANES:(c + 1) * LANES]

    logits = lax.dot_general(wrt_ref[...], hb, (((1,), (1,)), ((), ())),
                             preferred_element_type=F32)
    scores = jax.nn.sigmoid(logits)
    sel = scores + bias_ref[...]

    io_g = lax.broadcasted_iota(I32, (PG, tm), 0)
    rows = []
    for g in range(N_EXPERT_GROUPS):
        blk = sel[g * PG:(g + 1) * PG, :]
        m1, f1 = _first_index_of_max(blk, io_g, PG)
        m2 = jnp.max(jnp.where(io_g == f1, -jnp.inf, blk), axis=0, keepdims=True)
        rows.append(m1 + m2)
    gs = jnp.concatenate(rows, axis=0)

    io_n = lax.broadcasted_iota(I32, (N_EXPERT_GROUPS, tm), 0)
    gsel = jnp.zeros((N_EXPERT_GROUPS, tm), F32)
    cur = gs
    for _ in range(TOPK_GROUPS):
        _, f = _first_index_of_max(cur, io_n, N_EXPERT_GROUPS)
        hit = io_n == f
        gsel = jnp.where(hit, 1.0, gsel)
        cur = jnp.where(hit, -jnp.inf, cur)
    masked = jnp.concatenate(
        [jnp.where(gsel[g:g + 1, :] > 0.5, sel[g * PG:(g + 1) * PG, :], -jnp.inf) for g in range(N_EXPERT_GROUPS)],
        axis=0)

    io_e = lax.broadcasted_iota(I32, (E, tm), 0)
    onehot = jnp.zeros((E, tm), F32)
    idx_rows, sc_rows = [], []
    cur = masked
    for _ in range(TOP_K):
        _, f = _first_index_of_max(cur, io_e, E)
        hit = io_e == f
        idx_rows.append(f)
        sc_rows.append(jnp.sum(jnp.where(hit, scores, 0.0), axis=0, keepdims=True))
        onehot = jnp.where(hit, 1.0, onehot)
        cur = jnp.where(hit, -jnp.inf, cur)
    sc = jnp.concatenate(sc_rows, axis=0)
    idx_ref[0] = jnp.concatenate(idx_rows, axis=0)
    w_ref[...] = sc / jnp.sum(sc, axis=0, keepdims=True) * ROUTED_SCALE

    before = jnp.dot(onehot.astype(BF16), tri_ref[...], preferred_element_type=F32) + run_ref[...]
    rank_rows = [jnp.sum(jnp.where(io_e == f, before, 0.0), axis=0, keepdims=True) for f in idx_rows]
    rank_ref[0] = jnp.concatenate(rank_rows, axis=0).astype(I32)
    run_ref[...] = run_ref[...] + jnp.sum(onehot, axis=1, keepdims=True)
    cnt_ref[...] = run_ref[...]


def _router(tok, wrt, bias):
    N, D = tok.shape
    tm = ROUTER_TILE
    assert N % tm == 0
    tri = (jnp.arange(tm)[:, None] < jnp.arange(tm)[None, :]).astype(BF16)
    return pl.pallas_call(
        _router_kernel,
        grid=(N // tm,),
        in_specs=[
            pl.BlockSpec((tm, D), lambda i: (i, 0)),
            _const_spec(wrt.shape), _const_spec(bias.shape), _const_spec(tri.shape),
        ],
        out_specs=[
            pl.BlockSpec((1, TOP_K, tm), lambda i: (i, 0, 0)),
            pl.BlockSpec((TOP_K, tm), lambda i: (0, i)),
            pl.BlockSpec((1, TOP_K, tm), lambda i: (i, 0, 0)),
            _const_spec((N_EXPERTS, 1)),
            pl.BlockSpec((tm * PACKED_SUBLANES, LANES), lambda i: (i, 0)),
        ],
        out_shape=[
            jax.ShapeDtypeStruct((N // tm, TOP_K, tm), I32),
            jax.ShapeDtypeStruct((TOP_K, N), F32),
            jax.ShapeDtypeStruct((N // tm, TOP_K, tm), I32),
            jax.ShapeDtypeStruct((N_EXPERTS, 1), F32),
            jax.ShapeDtypeStruct((N * PACKED_SUBLANES, LANES), U32),
        ],
        scratch_shapes=[pltpu.VMEM((N_EXPERTS, 1), F32)],
        compiler_params=_cparams(("arbitrary",)),
        name="moe_router",
    )(tok, wrt, bias, tri)


def _slots_kernel(idx_ref, rank_ref, pstart_ref, dest_ref):
    idx = idx_ref[0]
    tm = idx.shape[1]
    io_e = lax.broadcasted_iota(I32, (N_EXPERTS, tm), 0)
    start = pstart_ref[...]
    rows = [jnp.sum(jnp.where(io_e == idx[k:k + 1, :], start, 0.0), axis=0, keepdims=True) for k in range(TOP_K)]
    dest_ref[0] = jnp.concatenate(rows, axis=0).astype(I32) + rank_ref[0]


def _slots(idx3, rank3, pstart):
    nt, _, tm = idx3.shape
    spec = pl.BlockSpec((1, TOP_K, tm), lambda i: (i, 0, 0))
    return pl.pallas_call(
        _slots_kernel,
        grid=(nt,),
        in_specs=[spec, spec, _const_spec((N_EXPERTS, 1))],
        out_specs=spec,
        out_shape=jax.ShapeDtypeStruct(idx3.shape, I32),
        compiler_params=_cparams(("arbitrary",)),
        name="moe_slots",
    )(idx3, rank3, pstart.astype(F32).reshape(N_EXPERTS, 1))


def _load_slot_table(i, dest_hbm, dest_s, sem):
    copy = pltpu.make_async_copy(dest_hbm.at[i], dest_s, sem)
    copy.start()
    copy.wait()


def _dispatch_kernel(dest_hbm, pend_ref, cnt_ref, tok_ref, xs_hbm, dest_s, zero_ref, sem_idx, sem_rows):
    i = pl.program_id(0)
    tm = tok_ref.shape[0]
    blk = zero_ref.shape[0]

    @pl.when(i == 0)
    def _():
        zero_ref[...] = jnp.zeros_like(zero_ref)

        def zero_copy(e):
            last_block = pl.multiple_of(pend_ref[e] - blk, blk)
            return pltpu.make_async_copy(zero_ref, xs_hbm.at[pl.ds(last_block, blk)], sem_rows)

        def start(e, c):
            @pl.when(cnt_ref[e] > 0)
            def _():
                zero_copy(e).start()
            return c

        def wait(e, c):
            @pl.when(cnt_ref[e] > 0)
            def _():
                zero_copy(e).wait()
            return c

        lax.fori_loop(0, N_EXPERTS, start, 0)
        lax.fori_loop(0, N_EXPERTS, wait, 0)

    _load_slot_table(i, dest_hbm, dest_s, sem_idx)

    def start(t, c):
        for k in range(TOP_K):
            pltpu.make_async_copy(tok_ref.at[t], xs_hbm.at[dest_s[k, t]], sem_rows).start()
        return c

    def wait(t, c):
        for k in range(TOP_K):
            pltpu.make_async_copy(tok_ref.at[0], xs_hbm.at[0], sem_rows).wait()
        return c

    lax.fori_loop(0, tm, start, 0)
    lax.fori_loop(0, tm, wait, 0)


def _dispatch(rows, dest3, pend, counts, n_slots):
    N = rows.shape[0]
    tm = dest3.shape[2]
    assert N % tm == 0 and dest3.shape == (N // tm, TOP_K, tm)
    smem = pl.BlockSpec(memory_space=pltpu.SMEM)
    return pl.pallas_call(
        _dispatch_kernel,
        grid=(N // tm,),
        in_specs=[
            pl.BlockSpec(memory_space=pl.ANY), smem, smem,
            pl.BlockSpec((tm,) + rows.shape[1:], lambda i: (i, 0, 0)),
        ],
        out_specs=pl.BlockSpec(memory_space=pl.ANY),
        scratch_shapes=[
            pltpu.SMEM((TOP_K, tm), I32),
            pltpu.VMEM((EXPERT_BLOCK,) + rows.shape[1:], rows.dtype),
            pltpu.SemaphoreType.DMA,
            pltpu.SemaphoreType.DMA,
        ],
        out_shape=jax.ShapeDtypeStruct((n_slots,) + rows.shape[1:], rows.dtype),
        compiler_params=_cparams(("arbitrary",)),
        name="moe_dispatch",
    )(dest3, pend, counts, rows)


def _ffn_kernel(be_ref, nu_ref, buf_ref, nxt_ref, xs_ref, wg_hbm, wu_hbm, wd_hbm, ys_ref,
                wg_buf, wu_buf, wd_buf, wgu_s, wd_s, sems, *, layer):
    b = pl.program_id(0)
    used = b < nu_ref[0]
    prev = be_ref[jnp.maximum(b - 1, 0)]
    new_expert = jnp.logical_or(b == 0, be_ref[b] != prev)

    def weight_copies(e, s):
        return [pltpu.make_async_copy(wg_hbm.at[layer, e], wg_buf.at[s], sems.at[s]),
                pltpu.make_async_copy(wu_hbm.at[layer, e], wu_buf.at[s], sems.at[s]),
                pltpu.make_async_copy(wd_hbm.at[layer, e], wd_buf.at[s], sems.at[s])]

    @pl.when(jnp.logical_and(used, b == 0))
    def _():
        for c in weight_copies(be_ref[0], buf_ref[0]):
            c.start()

    @pl.when(jnp.logical_and(used, new_expert))
    def _():
        s = buf_ref[b]

        @pl.when(nxt_ref[b] >= 0)
        def _():
            for c in weight_copies(nxt_ref[b], 1 - s):
                c.start()

        for c in weight_copies(be_ref[b], s):
            c.wait()
        wgu_s[:, :D_EXPERT] = wg_buf[s].astype(BF16)
        wgu_s[:, D_EXPERT:] = wu_buf[s].astype(BF16)
        wd_s[...] = wd_buf[s].astype(BF16)

    @pl.when(used)
    def _():
        blk = xs_ref.shape[0] // PACKED_SUBLANES
        words = [xs_ref[pl.ds(c, blk, stride=PACKED_SUBLANES), :] for c in range(PACKED_SUBLANES)]
        low = [lax.bitcast_convert_type(w << 16, F32) for w in words]
        high = [lax.bitcast_convert_type(w & jnp.uint32(0xFFFF0000), F32) for w in words]
        x = jnp.concatenate(low + high, axis=1)
        gu = jnp.dot(x.astype(BF16), wgu_s[...], preferred_element_type=F32)
        gt = gu[:, :D_EXPERT]
        hid = gt * jax.nn.sigmoid(gt) * gu[:, D_EXPERT:]
        y = jnp.dot(hid.astype(BF16), wd_s[...], preferred_element_type=F32)
        for c in range(SUBLANES):
            ys_ref[pl.ds(c, blk, stride=SUBLANES), :] = y[:, c * LANES:(c + 1) * LANES]


def _expert_ffn(xs, block_e, n_used, layer, wg, wu, wd):
    n_slots = xs.shape[0]
    D = SUBLANES * LANES
    blk = EXPERT_BLOCK
    n_blocks = n_slots // blk

    def row_map(b, be, nu, buf, nxt):
        return (jnp.minimum(b, nu[0] - 1), 0)

    first = jnp.concatenate([jnp.ones((1,), bool), block_e[1:] != block_e[:-1]])
    buf = ((jnp.cumsum(first.astype(I32)) - 1) % 2).astype(I32)
    nxt_block = jnp.sum(block_e[None, :] <= block_e[:, None], axis=1).astype(I32)
    nxt = jnp.where(nxt_block < n_used[0], block_e[jnp.minimum(nxt_block, n_blocks - 1)], -1).astype(I32)

    hbm = pl.BlockSpec(memory_space=pl.ANY)
    return pl.pallas_call(
        functools.partial(_ffn_kernel, layer=layer),
        grid_spec=pltpu.PrefetchScalarGridSpec(
            num_scalar_prefetch=4,
            grid=(n_blocks,),
            in_specs=[pl.BlockSpec((blk * PACKED_SUBLANES, LANES), row_map), hbm, hbm, hbm],
            out_specs=pl.BlockSpec((blk * SUBLANES, LANES), row_map),
            scratch_shapes=[
                pltpu.VMEM((2, D, D_EXPERT), F32), pltpu.VMEM((2, D, D_EXPERT), F32), pltpu.VMEM((2, D_EXPERT, D), F32),
                pltpu.VMEM((D, 2 * D_EXPERT), BF16), pltpu.VMEM((D_EXPERT, D), BF16),
                pltpu.SemaphoreType.DMA((2,)),
            ],
        ),
        out_shape=jax.ShapeDtypeStruct((n_slots * SUBLANES, LANES), F32),
        compiler_params=_cparams(("arbitrary",)),
        name="moe_ffn",
    )(block_e, n_used, buf, nxt, xs.reshape(n_slots * PACKED_SUBLANES, LANES), wg, wu, wd
      ).reshape(n_slots, SUBLANES, LANES)


def _combine_kernel(dest_hbm, ys_hbm, tok_ref, w_ref, wsg_ref, wsu_ref, wsd_ref, g_ref, b_ref,
                    out_ref, dest_s, buf_ref, sem_idx, sem_rows):
    i = pl.program_id(0)
    tm = tok_ref.shape[0]

    _load_slot_table(i, dest_hbm, dest_s, sem_idx)

    def start(t, c):
        row = pl.multiple_of(t * SUBLANES, SUBLANES)
        for k in range(TOP_K):
            pltpu.make_async_copy(ys_hbm.at[dest_s[k, t]], buf_ref.at[k, pl.ds(row, SUBLANES)], sem_rows).start()
        return c

    def wait(t, c):
        for k in range(TOP_K):
            pltpu.make_async_copy(ys_hbm.at[0], buf_ref.at[0, pl.ds(0, SUBLANES)], sem_rows).wait()
        return c

    lax.fori_loop(0, tm, start, 0)

    h = tok_ref[...]
    hb = h.astype(BF16)
    gt = jnp.dot(hb, wsg_ref[...], preferred_element_type=F32)
    up = jnp.dot(hb, wsu_ref[...], preferred_element_type=F32)
    hid = gt * jax.nn.sigmoid(gt) * up
    f = jnp.dot(hid.astype(BF16), wsd_ref[...], preferred_element_type=F32)

    lax.fori_loop(0, tm, wait, 0)
    w = w_ref[...]
    wb = [jnp.broadcast_to(w[:, k:k + 1], (tm, LANES)) for k in range(TOP_K)]
    chunks = []
    for c in range(SUBLANES):
        acc = f[:, c * LANES:(c + 1) * LANES]
        for k in range(TOP_K):
            acc = acc + buf_ref[k, pl.ds(c, tm, stride=SUBLANES), :] * wb[k]
        chunks.append(acc)
    f = jnp.concatenate(chunks, axis=1)
    out_ref[...] = _layer_norm(DEEPNORM_ALPHA * h + f, g_ref[...], b_ref[...])


def _combine(tok, ys, dest3, w_tok, wsg, wsu, wsd, g, b):
    N, D = tok.shape
    tm = dest3.shape[2]
    assert N % tm == 0 and dest3.shape == (N // tm, TOP_K, tm)
    return pl.pallas_call(
        _combine_kernel,
        grid=(N // tm,),
        in_specs=[
            pl.BlockSpec(memory_space=pl.ANY),
            pl.BlockSpec(memory_space=pl.ANY),
            pl.BlockSpec((tm, D), lambda i: (i, 0)),
            pl.BlockSpec((tm, TOP_K), lambda i: (i, 0)),
            _const_spec(wsg.shape), _const_spec(wsu.shape), _const_spec(wsd.shape),
            _const_spec(g.shape), _const_spec(b.shape),
        ],
        out_specs=pl.BlockSpec((tm, D), lambda i: (i, 0)),
        scratch_shapes=[
            pltpu.SMEM((TOP_K, tm), I32),
            pltpu.VMEM((TOP_K, tm * SUBLANES, LANES), F32),
            pltpu.SemaphoreType.DMA,
            pltpu.SemaphoreType.DMA,
        ],
        out_shape=jax.ShapeDtypeStruct((N, D), F32),
        compiler_params=_cparams(("arbitrary",)),
        name="moe_combine",
    )(dest3, ys, tok, w_tok, wsg, wsu, wsd, g, b)


def _moe_layer(tok, layer, w_router, router_bias, we_gate, we_up, we_down, ws_gate, ws_up, ws_down, ln_g, ln_b):
    N, D = tok.shape
    blk = EXPERT_BLOCK
    idx3, w_t, rank3, cnt, packed = _router(tok, w_router.T.astype(BF16), router_bias.reshape(N_EXPERTS, 1))

    counts = cnt[:, 0].astype(I32)
    pcounts = (counts + blk - 1) // blk * blk
    pend = jnp.cumsum(pcounts).astype(I32)
    pstart = pend - pcounts
    n_blocks = N * TOP_K // blk + N_EXPERTS
    block_start = jnp.arange(n_blocks, dtype=I32) * blk
    block_e = jnp.minimum(jnp.sum(pend[None, :] <= block_start[:, None], axis=1), N_EXPERTS - 1).astype(I32)
    n_used = (pend[-1] // blk).reshape(1)

    dest3 = _slots(idx3, rank3, pstart)
    xs = _dispatch(packed.reshape(N, PACKED_SUBLANES, LANES), dest3, pend, counts, n_blocks * blk)
    ys = _expert_ffn(xs, block_e, n_used, layer, we_gate, we_up, we_down)
    return _combine(tok, ys, dest3, w_t.T,
                    ws_gate.astype(BF16), ws_up.astype(BF16), ws_down.astype(BF16),
                    ln_g.reshape(1, D), ln_b.reshape(1, D))


def _rel_bucket(dist):
    exact = N_BUCKETS // 2
    df = jnp.maximum(dist, 1).astype(F32)
    large = exact + (jnp.log(df / exact) / math.log(MAX_DISTANCE / exact) * (N_BUCKETS - exact)).astype(I32)
    large = jnp.minimum(large, N_BUCKETS - 1)
    return jnp.where(dist < exact, dist, large)


def _qkv_kernel(x_ref, perm_ref, w_ref, wkvt_ref, q_ref, k_ref, v_ref, kt_ref, vt_ref, *, dil, first_kept_tile,
                kept_cols):
    tl = x_ref.shape[0]
    chunk = tl // dil
    x = x_ref[...].astype(BF16)
    xp = x if dil == 1 else jnp.dot(perm_ref[...], x, preferred_element_type=F32).astype(BF16)
    y = jnp.dot(xp, w_ref[...], preferred_element_type=F32)
    for r in range(dil):
        rows = slice(r * chunk, (r + 1) * chunk)
        q_ref[r] = y[rows, :D_ATTN].astype(BF16)
        k_ref[r] = y[rows, D_ATTN:2 * D_ATTN].astype(BF16)
        v_ref[r] = y[rows, 2 * D_ATTN:].astype(BF16)

    @pl.when(pl.program_id(1) >= first_kept_tile)
    def _():
        kvt = lax.dot_general(wkvt_ref[...], x[tl - kept_cols:, :], (((1,), (1,)), ((), ())),
                              preferred_element_type=F32)
        kt_ref[...] = kvt[:D_ATTN]
        vt_ref[...] = kvt[D_ATTN:]


def _qkv_prompt(tok, B, S, w_g, dil, keep):
    D = tok.shape[1]
    tl = min(S, 512)
    kept_cols = min(tl, keep)
    assert S % tl == 0 and tl % dil == 0 and keep % kept_cols == 0 and (S - keep) % kept_cols == 0
    first_kept_tile = (S - keep) // tl
    phase_spec = pl.BlockSpec((None, dil, tl // dil, D_ATTN), lambda b, t: (b, 0, t, 0))
    state_spec = pl.BlockSpec((None, D_ATTN, kept_cols), lambda b, t: (b, 0, jnp.maximum(t - first_kept_tile, 0)))
    wkvt = w_g[:, D_ATTN:].T
    dst = jnp.arange(tl, dtype=I32)
    src = (dst % (tl // dil)) * dil + dst // (tl // dil)
    perm = (src[:, None] == jnp.arange(tl, dtype=I32)[None, :]).astype(BF16)
    return pl.pallas_call(
        functools.partial(_qkv_kernel, dil=dil, first_kept_tile=first_kept_tile, kept_cols=kept_cols),
        grid=(B, S // tl),
        in_specs=[pl.BlockSpec((tl, D), lambda b, t: (b * (S // tl) + t, 0)), _const_spec(perm.shape),
                  _const_spec(w_g.shape), _const_spec(wkvt.shape)],
        out_specs=[phase_spec, phase_spec, phase_spec, state_spec, state_spec],
        out_shape=[jax.ShapeDtypeStruct((B, dil, S // dil, D_ATTN), BF16)] * 3
        + [jax.ShapeDtypeStruct((B, D_ATTN, keep), F32)] * 2,
        compiler_params=_cparams(("arbitrary", "arbitrary")),
        name="qkv_prompt",
    )(tok, perm, w_g, wkvt)


def _attn_prompt_kernel(q_ref, kp_ref, kc_ref, vp_ref, vc_ref, bias_ref, o_ref, lse_ref, *, steps, phases_on_lanes):
    n = pl.program_id(1)
    qb = QUERY_BLOCK
    qi = lax.broadcasted_iota(I32, (qb, 2 * qb), 0)
    kj = lax.broadcasted_iota(I32, (qb, 2 * qb), 1)
    dist = qi + qb - kj
    valid = (dist >= 0) & (dist <= steps) & ((n > 0) | (kj >= qb))
    low = lax.broadcasted_iota(I32, (1, 2 * HEAD_DIM), 1) < HEAD_DIM
    scale = HEAD_DIM ** -0.5
    for i in range(q_ref.shape[0]):
        for p in range(HEADS // 2):
            c0 = p * 2 * HEAD_DIM
            cols = slice(c0, c0 + 2 * HEAD_DIM)
            q2 = q_ref[i, :, cols]
            k2 = jnp.concatenate([kp_ref[i, :, cols], kc_ref[i, :, cols]], axis=0)
            v2 = jnp.concatenate([vp_ref[i, :, cols], vc_ref[i, :, cols]], axis=0)
            outs, lses = [], []
            for half in range(2):
                keep = low if half == 0 else jnp.logical_not(low)
                qh = jnp.where(keep, q2, jnp.zeros_like(q2))
                s = lax.dot_general(qh, k2, (((1,), (1,)), ((), ())), preferred_element_type=F32)
                s = s * scale + bias_ref[2 * p + half]
                s = jnp.where(valid, s, NEG_INF)
                m = jnp.max(s, axis=-1, keepdims=True)
                e = jnp.exp(s - m)
                l = jnp.sum(e, axis=-1, keepdims=True)
                prob = e / l
                outs.append(jnp.dot(prob.astype(BF16), v2, preferred_element_type=F32))
                lses.append(m + jnp.log(l))
            o_val = jnp.where(low, outs[0], outs[1])
            lse_val = jnp.where(low, lses[0], lses[1])
            if phases_on_lanes:
                out_cols = slice(i * D_ATTN + c0, i * D_ATTN + c0 + 2 * HEAD_DIM)
                o_ref[0, :, out_cols] = o_val
                lse_ref[0, :, out_cols] = lse_val
            else:
                o_ref[i, :, cols] = o_val
                lse_ref[i, :, cols] = lse_val


def _attn_prompt(q, k, v, bias, steps):
    B, dil, L, _ = q.shape
    qb = QUERY_BLOCK
    ns = ATTN_STREAMS
    assert L % qb == 0 and (B * dil) % ns == 0 and (dil == 1 or dil % ns == 0)
    cur = pl.BlockSpec((ns, qb, D_ATTN), lambda bd, n: (bd, n, 0))
    prev = pl.BlockSpec((ns, qb, D_ATTN), lambda bd, n: (bd, jnp.maximum(n - 1, 0), 0))
    if dil == 1:
        out = pl.BlockSpec((ns, qb, D_ATTN), lambda bd, n: (bd, n, 0))
    else:
        out = pl.BlockSpec((1, qb, ns * D_ATTN), lambda bd, n: (bd * ns // dil, n, bd % (dil // ns)))
    qv, kv, vv = (a.reshape(B * dil, L, D_ATTN) for a in (q, k, v))
    o, lse = pl.pallas_call(
        functools.partial(_attn_prompt_kernel, steps=steps, phases_on_lanes=dil > 1),
        grid=(B * dil // ns, L // qb),
        in_specs=[cur, prev, cur, prev, cur, _const_spec(bias.shape)],
        out_specs=[out, out],
        out_shape=[jax.ShapeDtypeStruct((B, L, dil * D_ATTN), F32)] * 2,
        compiler_params=_cparams(("arbitrary", "arbitrary")),
        name="attn_prompt",
    )(qv, kv, kv, vv, vv, bias)
    return o.reshape(B * L * dil, D_ATTN), lse.reshape(B * L * dil, D_ATTN)


def _merge_kernel(o1, o2, o3, l1, l2, l3, h_ref, wo_ref, g_ref, b_ref, out_ref):
    a1, a2, a3 = l1[...], l2[...], l3[...]
    m = jnp.maximum(jnp.maximum(a1, a2), a3)
    e1, e2, e3 = jnp.exp(a1 - m), jnp.exp(a2 - m), jnp.exp(a3 - m)
    o = (e1 * o1[...] + e2 * o2[...] + e3 * o3[...]) / (e1 + e2 + e3)
    y = jnp.dot(o.astype(BF16), wo_ref[...], preferred_element_type=F32)
    out_ref[...] = _layer_norm(DEEPNORM_ALPHA * h_ref[...] + y, g_ref[...], b_ref[...])


def _merge(os_, ls_, tok, row0, wo, g, b):
    M = os_[0].shape[0]
    N, D = tok.shape
    tm = math.gcd(MERGE_TILE, M, N)
    assert row0 % tm == 0
    a_spec = pl.BlockSpec((tm, D_ATTN), lambda i: (i, 0))
    row_spec = pl.BlockSpec((tm, D), lambda i: (row0 // tm + i, 0))
    args = (*os_, *ls_, tok, wo, g, b)
    return pl.pallas_call(
        _merge_kernel,
        grid=(M // tm,),
        in_specs=[a_spec] * 6 + [row_spec, _const_spec(wo.shape), _const_spec(g.shape), _const_spec(b.shape)],
        out_specs=row_spec,
        out_shape=jax.ShapeDtypeStruct((N, D), F32),
        input_output_aliases={len(os_) + len(ls_): 0},
        compiler_params=_cparams(("arbitrary",)),
        name="attn_merge",
    )(*args)


def _mm_kernel(x_ref, w_ref, o_ref):
    o_ref[...] = jnp.dot(x_ref[...].astype(BF16), w_ref[...], preferred_element_type=F32)


def _matmul(x, row0, M, w, tn):
    K = x.shape[1]
    _, N = w.shape
    assert N % tn == 0 and row0 % M == 0
    return pl.pallas_call(
        _mm_kernel,
        grid=(N // tn,),
        in_specs=[pl.BlockSpec((M, K), lambda j: (row0 // M, 0)), pl.BlockSpec((K, tn), lambda j: (0, j))],
        out_specs=pl.BlockSpec((M, tn), lambda j: (0, j)),
        out_shape=jax.ShapeDtypeStruct((M, N), F32),
        compiler_params=_cparams(("arbitrary",)),
        name="matmul",
    )(x, w)


def _round_bf16(x):
    return x.astype(BF16).astype(F32)


def _attn_sample_kernel(q_ref, kn_ref, vn_ref, kc_ref, vc_ref, bias_ref, b0_ref, o_ref, lse_ref):
    scale = HEAD_DIM ** -0.5
    tb = kc_ref.shape[0]
    eye = (lax.broadcasted_iota(I32, (HEAD_DIM, HEAD_DIM), 0)
           == lax.broadcasted_iota(I32, (HEAD_DIM, HEAD_DIM), 1)).astype(F32)

    def column(ref):
        rows = _round_bf16(ref[...])
        return jnp.sum(rows * eye[None], axis=2, keepdims=True).reshape(tb, HEADS, HEAD_DIM, 1)

    q = column(q_ref)
    kn = column(kn_ref)
    vn = column(vn_ref)
    kc = _round_bf16(kc_ref[...])
    s = jnp.sum(kc * q, axis=2, keepdims=True) * scale + bias_ref[...][None]
    s0 = jnp.sum(q * kn, axis=2, keepdims=True) * scale + b0_ref[...][None]
    m = jnp.maximum(jnp.max(s, axis=3, keepdims=True), s0)
    e = jnp.exp(s - m)
    e0 = jnp.exp(s0 - m)
    l = jnp.sum(e, axis=3, keepdims=True) + e0
    p = _round_bf16(e / l)
    p0 = _round_bf16(e0 / l)
    vc = _round_bf16(vc_ref[...])
    o = jnp.sum(vc * p, axis=3, keepdims=True) + p0 * vn
    o_ref[...] = jnp.sum(o.reshape(tb * HEADS, HEAD_DIM, 1) * eye[None], axis=1, keepdims=True)
    lse_ref[...] = (m + jnp.log(l)).reshape(tb * HEADS, 1, 1)


def _attn_sample(q, kn, vn, ckt, cvt, j, bias, bias_self, tb=1):
    DB = q.shape[0] // HEADS
    Wb = ckt.shape[3]
    assert DB % tb == 0
    new_spec = pl.BlockSpec((tb * HEADS, 1, HEAD_DIM), lambda i: (i, 0, 0))
    cache_spec = pl.BlockSpec((tb, HEADS, HEAD_DIM, Wb), lambda i: (j * (DB // tb) + i, 0, 0, 0))
    return pl.pallas_call(
        _attn_sample_kernel,
        grid=(DB // tb,),
        in_specs=[new_spec, new_spec, new_spec, cache_spec, cache_spec,
                  _const_spec(bias.shape), _const_spec(bias_self.shape)],
        out_specs=[new_spec, pl.BlockSpec((tb * HEADS, 1, 1), lambda i: (i, 0, 0))],
        out_shape=[jax.ShapeDtypeStruct((DB * HEADS, 1, HEAD_DIM), F32), jax.ShapeDtypeStruct((DB * HEADS, 1, 1), F32)],
        compiler_params=_cparams(("arbitrary",)),
        name="attn_sample",
    )(q, kn, vn, ckt, cvt, bias, bias_self)


def _bias_lookup(tab, buckets):
    onehot = (buckets[..., None] == jnp.arange(N_BUCKETS, dtype=I32)).astype(F32)
    return jnp.einsum('...n,nh->...h', onehot, tab, precision=lax.Precision.HIGHEST)


def _dilated_layer(tok, B, S, caches, j, w_qkv, w_o, rel_bias, g1, b1):
    DB = tok.shape[0] - B * S
    w_qkv_b = w_qkv.astype(BF16)
    w_o_b = w_o.astype(BF16)
    qb = QUERY_BLOCK
    qi = jnp.arange(qb, dtype=I32)[:, None]
    kj = jnp.arange(2 * qb, dtype=I32)[None, :]
    dist = qi + qb - kj
    qkv_s = _matmul(tok, B * S, DB, w_qkv_b, 512)

    def sample_cols(part, g):
        c0 = (part * N_GROUPS + g) * D_ATTN
        return qkv_s[:, c0:c0 + D_ATTN].reshape(DB, HEADS, HEAD_DIM)

    def position_minor(c):
        return jnp.transpose(c, (0, 1, 3, 4, 2)).reshape(c.shape[0] * DB, HEADS, HEAD_DIM, c.shape[2])

    os_, ls_, kp, vp = [], [], [], []
    os_s, ls_s, ks, vs = [], [], [], []
    for g, (window, dil) in enumerate(DIL_CONFIGS):
        steps = window // dil
        assert steps == qb
        tab = rel_bias[:, g * HEADS:(g + 1) * HEADS]
        cols = [w_qkv_b[:, (part * N_GROUPS + g) * D_ATTN:(part * N_GROUPS + g + 1) * D_ATTN] for part in range(3)]
        keep = min(window, S)
        q, k, v, kt, vt = _qkv_prompt(tok, B, S, jnp.concatenate(cols, axis=1), dil, keep)
        bias = _bias_lookup(tab, _rel_bucket(jnp.maximum(dist, 0) * dil)).transpose(2, 0, 1)
        o, lse = _attn_prompt(q, k, v, bias, steps)
        os_.append(o)
        ls_.append(lse)
        kp.append(jnp.transpose(kt.reshape(B, HEADS, HEAD_DIM, keep), (0, 3, 1, 2)))
        vp.append(jnp.transpose(vt.reshape(B, HEADS, HEAD_DIM, keep), (0, 3, 1, 2)))
        ck, cv = caches[g]
        wb = ck.shape[2]
        assert wb == steps * dil and ck.shape[1] == DB
        pos = jnp.arange(wb, dtype=I32)
        cache_bias = jnp.where((pos % dil == 0)[:, None], _bias_lookup(tab, _rel_bucket(wb - pos)), NEG_INF)
        self_bias = _bias_lookup(tab, _rel_bucket(jnp.zeros((1,), I32)))
        q_s, k_s, v_s = sample_cols(0, g), sample_cols(1, g), sample_cols(2, g)
        head_rows = (DB * HEADS, 1, HEAD_DIM)
        o_s, lse_s = _attn_sample(q_s.reshape(head_rows), k_s.reshape(head_rows), v_s.reshape(head_rows),
                                  position_minor(ck), position_minor(cv), j, cache_bias.T[:, None, :],
                                  self_bias.T[:, :, None], tb=max(1, 1024 // wb))
        os_s.append(o_s.reshape(DB, D_ATTN))
        ls_s.append(jnp.broadcast_to(lse_s.reshape(DB, HEADS, 1), (DB, HEADS, HEAD_DIM)).reshape(DB, D_ATTN))
        ks.append(k_s[:, None])
        vs.append(v_s[:, None])
    tok = _merge(os_, ls_, tok, 0, w_o_b, g1, b1)
    tok = _merge(os_s, ls_s, tok, B * S, w_o_b, g1, b1)
    return tok, kp, vp, ks, vs


def _pool_conv_layer(hp, hs, state_pool, state_conv, w_in, pool_w, pool_scale, conv_w, conv_b, ln_g, ln_b, w_out,
                     g1, b1):
    params = (w_in.astype(BF16), pool_w.astype(BF16), pool_scale.reshape(1, D_POOL), conv_w,
              conv_b.reshape(1, D_CONV), ln_g.reshape(1, D_CONV), ln_b.reshape(1, D_CONV), w_out.astype(BF16), g1, b1)
    B, S, _ = hp.shape
    tok = jnp.zeros((B * S + hs.shape[0], hp.shape[2]), F32)
    tok, pst, cst = _l0_prompt(hp, tok, *params)
    tok, u_s, glu_s = _l0_sample(hs, tok, state_pool, state_conv, *params)
    pool_p = pst[:, POOL_HALO - POOL_STATE:]
    conv_p = cst[:, CONV_HALO - CONV_STATE:]
    pool_s = jnp.concatenate([state_pool[:, 1:], u_s[:, None, :]], axis=1)
    conv_s = jnp.concatenate([state_conv[:, 1:], glu_s[:, None, :]], axis=1)
    return tok, pool_p, conv_p, pool_s, conv_s


def kernel(x_prompt, x_sample, state_pool, state_conv, cache_k1, cache_v1, cache_k2, cache_v2, cache_k3, cache_v3,
           w_in_ab, pool_w, pool_scale, conv_w, conv_b, conv_ln_g, conv_ln_b, w_out_ab, w_qkv, w_o, rel_bias,
           ln1_g, ln1_b, ln2_g, ln2_b, w_router, router_bias, we_gate, we_up, we_down, ws_gate, ws_up, ws_down):
    B, S, D = x_prompt.shape
    DB, T, _ = x_sample.shape
    assert T == 1 and D == D_MODEL
    tok = None
    caches_k = (cache_k1, cache_k2, cache_k3)
    caches_v = (cache_v1, cache_v2, cache_v3)
    pool_p, conv_p, pool_s, conv_s = [], [], [], []
    kp = [[] for _ in range(N_GROUPS)]
    vp = [[] for _ in range(N_GROUPS)]
    ksm = [[] for _ in range(N_GROUPS)]
    vsm = [[] for _ in range(N_GROUPS)]
    for layer in range(DEPTH):
        j = layer // 2
        g1 = ln1_g[layer].reshape(1, D)
        b1 = ln1_b[layer].reshape(1, D)
        if layer % 2 == 0:
            hp = x_prompt if tok is None else tok[:B * S].reshape(B, S, D)
            hs = x_sample.reshape(DB, D) if tok is None else tok[B * S:]
            tok, pp, cp, ps, cs = _pool_conv_layer(
                hp, hs, state_pool[j], state_conv[j], w_in_ab[j], pool_w[j], pool_scale[j], conv_w[j], conv_b[j],
                conv_ln_g[j], conv_ln_b[j], w_out_ab[j], g1, b1)
            pool_p.append(pp)
            conv_p.append(cp)
            pool_s.append(ps)
            conv_s.append(cs)
        else:
            caches = [(caches_k[g], caches_v[g]) for g in range(N_GROUPS)]
            tok, nkp, nvp, nks, nvs = _dilated_layer(tok, B, S, caches, j, w_qkv[j], w_o[j], rel_bias, g1, b1)
            for g in range(N_GROUPS):
                kp[g].append(nkp[g])
                vp[g].append(nvp[g])
                ksm[g].append(nks[g])
                vsm[g].append(nvs[g])
        tok = _moe_layer(tok, layer, w_router[layer], router_bias[layer], we_gate, we_up, we_down,
                         ws_gate[layer], ws_up[layer], ws_down[layer], ln2_g[layer], ln2_b[layer])
    return (tok[:B * S].reshape(B, S, D), tok[B * S:].reshape(DB, T, D),
            jnp.stack(pool_p), jnp.stack(conv_p),
            jnp.stack(kp[0]), jnp.stack(vp[0]), jnp.stack(kp[1]), jnp.stack(vp[1]), jnp.stack(kp[2]), jnp.stack(vp[2]),
            jnp.stack(pool_s), jnp.stack(conv_s),
            jnp.stack(ksm[0]), jnp.stack(vsm[0]), jnp.stack(ksm[1]), jnp.stack(vsm[1]),
            jnp.stack(ksm[2]), jnp.stack(vsm[2]))
```

```python
import functools
import math

import jax
import jax.numpy as jnp
from jax import lax
from jax.experimental import pallas as pl
from jax.experimental.pallas import tpu as pltpu

F32 = jnp.float32
BF16 = jnp.bfloat16
I32 = jnp.int32
U32 = jnp.uint32

D_MODEL = 1024
D_POOL = 512
D_CONV = 512
POOL_WINDOWS = (2, 4, 8, 16)
POOL_GROUP = 128
POOL_STATE = 15
CONV_WIDTH = 31
CONV_STATE = 30
DIL_CONFIGS = ((128, 1), (512, 4), (2048, 16))
N_GROUPS = 3
HEADS = 8
HEAD_DIM = 64
D_ATTN = HEADS * HEAD_DIM
QUERY_BLOCK = 128
N_BUCKETS = 32
MAX_DISTANCE = 2048
N_EXPERTS = 256
TOP_K = 8
N_EXPERT_GROUPS = 8
EXPERTS_PER_GROUP = N_EXPERTS // N_EXPERT_GROUPS
TOPK_GROUPS = 4
D_EXPERT = 256
ROUTED_SCALE = 2.5
DEPTH = 2
DEEPNORM_ALPHA = (2.0 * DEPTH) ** 0.25
LN_EPS = 1e-5
NEG_INF = -1e30

LANES = 128
SUBLANES = 8
PACKED_WORDS = D_MODEL // 2
PACKED_SUBLANES = PACKED_WORDS // LANES
VMEM_LIMIT_BYTES = 56 * 1024 * 1024

POOL_HALO = 16
CONV_HALO = 32
L0_TIME_TILE = 256
ROUTER_TILE = 384
EXPERT_BLOCK = 256
MERGE_TILE = 512
ATTN_STREAMS = 4


def _cparams(sem):
    return pltpu.CompilerParams(dimension_semantics=sem, vmem_limit_bytes=VMEM_LIMIT_BYTES)


def _layer_norm(x, g, b):
    mu = jnp.mean(x, axis=-1, keepdims=True)
    xc = x - mu
    var = jnp.mean(xc * xc, axis=-1, keepdims=True)
    return xc * lax.rsqrt(var + LN_EPS) * g + b


def _const_spec(shape):
    nd = len(shape)
    return pl.BlockSpec(shape, lambda *_: (0,) * nd)


def _l0_prompt_kernel(x_ref, win_ref, pw_ref, ps_ref, cw_ref, cb_ref, cg_ref, cbeta_ref, wout_ref,
                      g1_ref, b1_ref, tok_hbm, h_ref, pstate_ref, cstate_ref, ue_ref, ge_ref, gs_ref):
    del tok_hbm
    tt = x_ref.shape[1]
    t = pl.program_id(1)

    @pl.when(t == 0)
    def _():
        ue_ref[0:POOL_HALO, :] = jnp.zeros((POOL_HALO, D_POOL), F32)
        ge_ref[0:CONV_HALO, :] = jnp.zeros((CONV_HALO, D_CONV), F32)

    @pl.when(t > 0)
    def _():
        ue_ref[0:POOL_HALO, :] = ue_ref[tt:tt + POOL_HALO, :]
        ge_ref[0:CONV_HALO, :] = ge_ref[tt:tt + CONV_HALO, :]

    x = x_ref[0]
    proj = jnp.dot(x.astype(BF16), win_ref[...], preferred_element_type=F32)
    u = proj[:, :D_POOL]
    a = proj[:, D_POOL:D_POOL + D_CONV]
    gate = proj[:, D_POOL + D_CONV:]
    glu = a * jax.nn.sigmoid(gate)
    ue_ref[POOL_HALO:POOL_HALO + tt, :] = u
    ge_ref[CONV_HALO:CONV_HALO + tt, :] = glu

    tg = t * tt + lax.broadcasted_iota(I32, (tt, 1), 0)
    parts = []
    for g, w in enumerate(POOL_WINDOWS):
        c0 = g * POOL_GROUP
        ug = u[:, c0:c0 + POOL_GROUP]
        s = ug
        for j in range(1, w):
            s = s + ue_ref[POOL_HALO - j:POOL_HALO - j + tt, c0:c0 + POOL_GROUP]
        cnt = jnp.minimum(tg + 1, w).astype(F32)
        pooled = s / cnt - ug
        parts.append(jnp.dot(pooled.astype(BF16), pw_ref[g], preferred_element_type=F32))
    yp = jnp.concatenate(parts, axis=1) * ps_ref[...]

    rows = tt + CONV_HALO - SUBLANES
    for s in range(1, SUBLANES):
        gs_ref[s, 0:rows, :] = ge_ref[s:s + rows, :]
    acc = glu * cw_ref[CONV_STATE:CONV_STATE + 1, :]
    off = CONV_HALO - CONV_STATE
    for j in range(CONV_STATE):
        s = (off + j) % SUBLANES
        base = off + j - s
        window = ge_ref[base:base + tt, :] if s == 0 else gs_ref[s, base:base + tt, :]
        acc = acc + window * cw_ref[j:j + 1, :]
    yn = _layer_norm(acc + cb_ref[...], cg_ref[...], cbeta_ref[...])
    yc = yn * jax.nn.sigmoid(yn)

    cat = jnp.concatenate([yp, yc], axis=1).astype(BF16)
    m = jnp.dot(cat, wout_ref[...], preferred_element_type=F32)
    h_ref[...] = _layer_norm(DEEPNORM_ALPHA * x + m, g1_ref[...], b1_ref[...])
    pstate_ref[0] = ue_ref[tt:tt + POOL_HALO, :]
    cstate_ref[0] = ge_ref[tt:tt + CONV_HALO, :]


def _l0_prompt(x, tok_buf, win, pw, ps, cw, cb, cg, cbeta, wout, g1, b1):
    B, S, D = x.shape
    n_rows = tok_buf.shape[0]
    tt = min(L0_TIME_TILE, S)
    assert S % tt == 0 and tt >= CONV_HALO and n_rows >= B * S
    return pl.pallas_call(
        _l0_prompt_kernel,
        grid=(B, S // tt),
        in_specs=[
            pl.BlockSpec((1, tt, D), lambda b, t: (b, t, 0)),
            _const_spec(win.shape), _const_spec(pw.shape), _const_spec(ps.shape), _const_spec(cw.shape),
            _const_spec(cb.shape), _const_spec(cg.shape), _const_spec(cbeta.shape), _const_spec(wout.shape),
            _const_spec(g1.shape), _const_spec(b1.shape), pl.BlockSpec(memory_space=pl.ANY),
        ],
        out_specs=[
            pl.BlockSpec((tt, D), lambda b, t: (b * (S // tt) + t, 0)),
            pl.BlockSpec((1, POOL_HALO, D_POOL), lambda b, t: (b, 0, 0)),
            pl.BlockSpec((1, CONV_HALO, D_CONV), lambda b, t: (b, 0, 0)),
        ],
        out_shape=[
            jax.ShapeDtypeStruct((n_rows, D), F32),
            jax.ShapeDtypeStruct((B, POOL_HALO, D_POOL), F32),
            jax.ShapeDtypeStruct((B, CONV_HALO, D_CONV), F32),
        ],
        scratch_shapes=[pltpu.VMEM((tt + POOL_HALO, D_POOL), F32), pltpu.VMEM((tt + CONV_HALO, D_CONV), F32),
                        pltpu.VMEM((SUBLANES, tt + CONV_HALO, D_CONV), F32)],
        input_output_aliases={11: 0},
        compiler_params=_cparams(("arbitrary", "arbitrary")),
        name="l0_prompt",
    )(x, win, pw, ps, cw, cb, cg, cbeta, wout, g1, b1, tok_buf)


def _l0_sample_kernel(x_ref, sp_ref, sc_ref, win_ref, pw_ref, ps_ref, cw_ref, cb_ref, cg_ref, cbeta_ref,
                      wout_ref, g1_ref, b1_ref, tok_hbm, h_ref, u_ref, glu_ref):
    del tok_hbm
    x = x_ref[...]
    proj = jnp.dot(x.astype(BF16), win_ref[...], preferred_element_type=F32)
    u = proj[:, :D_POOL]
    a = proj[:, D_POOL:D_POOL + D_CONV]
    gate = proj[:, D_POOL + D_CONV:]
    glu = a * jax.nn.sigmoid(gate)
    u_ref[...] = u
    glu_ref[...] = glu

    parts = []
    for g, w in enumerate(POOL_WINDOWS):
        c0 = g * POOL_GROUP
        ug = u[:, c0:c0 + POOL_GROUP]
        past = sp_ref[:, POOL_STATE - (w - 1):POOL_STATE, c0:c0 + POOL_GROUP]
        s = ug + jnp.sum(past, axis=1)
        pooled = s / float(w) - ug
        parts.append(jnp.dot(pooled.astype(BF16), pw_ref[g], preferred_element_type=F32))
    yp = jnp.concatenate(parts, axis=1) * ps_ref[...]

    acc = glu * cw_ref[CONV_STATE:CONV_STATE + 1, :]
    acc = acc + jnp.sum(sc_ref[...] * cw_ref[0:CONV_STATE, :][None, :, :], axis=1)
    yn = _layer_norm(acc + cb_ref[...], cg_ref[...], cbeta_ref[...])
    yc = yn * jax.nn.sigmoid(yn)

    cat = jnp.concatenate([yp, yc], axis=1).astype(BF16)
    m = jnp.dot(cat, wout_ref[...], preferred_element_type=F32)
    h_ref[...] = _layer_norm(DEEPNORM_ALPHA * x + m, g1_ref[...], b1_ref[...])


def _l0_sample(x, tok_buf, sp, sc, win, pw, ps, cw, cb, cg, cbeta, wout, g1, b1):
    DB, D = x.shape
    N = tok_buf.shape[0]
    assert (N - DB) % DB == 0
    args = (x, sp, sc, win, pw, ps, cw, cb, cg, cbeta, wout, g1, b1)
    return pl.pallas_call(
        _l0_sample_kernel,
        grid=(1,),
        in_specs=[_const_spec(a.shape) for a in args] + [pl.BlockSpec(memory_space=pl.ANY)],
        out_specs=[pl.BlockSpec((DB, D), lambda i: (N // DB - 1, 0)), _const_spec((DB, D_POOL)),
                   _const_spec((DB, D_CONV))],
        out_shape=[
            jax.ShapeDtypeStruct((N, D), F32),
            jax.ShapeDtypeStruct((DB, D_POOL), F32),
            jax.ShapeDtypeStruct((DB, D_CONV), F32),
        ],
        input_output_aliases={len(args): 0},
        compiler_params=_cparams(("arbitrary",)),
        name="l0_sample",
    )(*args, tok_buf)


def _first_index_of_max(x, iota, size):
    m = jnp.max(x, axis=0, keepdims=True)
    f = jnp.min(jnp.where(x == m, iota, size), axis=0, keepdims=True)
    return m, f


def _router_kernel(h_ref, wrt_ref, bias_ref, tri_ref, idx_ref, w_ref, rank_ref, cnt_ref, packed_ref, run_ref):
    tm = h_ref.shape[0]
    E = N_EXPERTS
    PG = EXPERTS_PER_GROUP

    @pl.when(pl.program_id(0) == 0)
    def _():
        run_ref[...] = jnp.zeros_like(run_ref)

    hb = h_ref[...].astype(BF16)

    bits = lax.bitcast_convert_type(hb.astype(F32), U32)
    words = (bits[:, :PACKED_WORDS] >> 16) | bits[:, PACKED_WORDS:]
    for c in range(PACKED_SUBLANES):
        packed_ref[pl.ds(c, tm, stride=PACKED_SUBLANES), :] = words[:, c * LANES:(c + 1) * LANES]

    logits = lax.dot_general(wrt_ref[...], hb, (((1,), (1,)), ((), ())),
                             preferred_element_type=F32)
    scores = jax.nn.sigmoid(logits)
    sel = scores + bias_ref[...]

    io_g = lax.broadcasted_iota(I32, (PG, tm), 0)
    rows = []
    for g in range(N_EXPERT_GROUPS):
        blk = sel[g * PG:(g + 1) * PG, :]
        m1, f1 = _first_index_of_max(blk, io_g, PG)
        m2 = jnp.max(jnp.where(io_g == f1, -jnp.inf, blk), axis=0, keepdims=True)
        rows.append(m1 + m2)
    gs = jnp.concatenate(rows, axis=0)

    io_n = lax.broadcasted_iota(I32, (N_EXPERT_GROUPS, tm), 0)
    gsel = jnp.zeros((N_EXPERT_GROUPS, tm), F32)
    cur = gs
    for _ in range(TOPK_GROUPS):
        _, f = _first_index_of_max(cur, io_n, N_EXPERT_GROUPS)
        hit = io_n == f
        gsel = jnp.where(hit, 1.0, gsel)
        cur = jnp.where(hit, -jnp.inf, cur)
    masked = jnp.concatenate(
        [jnp.where(gsel[g:g + 1, :] > 0.5, sel[g * PG:(g + 1) * PG, :], -jnp.inf) for g in range(N_EXPERT_GROUPS)],
        axis=0)

    io_e = lax.broadcasted_iota(I32, (E, tm), 0)
    onehot = jnp.zeros((E, tm), F32)
    idx_rows, sc_rows = [], []
    cur = masked
    for _ in range(TOP_K):
        _, f = _first_index_of_max(cur, io_e, E)
        hit = io_e == f
        idx_rows.append(f)
        sc_rows.append(jnp.sum(jnp.where(hit, scores, 0.0), axis=0, keepdims=True))
        onehot = jnp.where(hit, 1.0, onehot)
        cur = jnp.where(hit, -jnp.inf, cur)
    sc = jnp.concatenate(sc_rows, axis=0)
    idx_ref[0] = jnp.concatenate(idx_rows, axis=0)
    w_ref[...] = sc / jnp.sum(sc, axis=0, keepdims=True) * ROUTED_SCALE

    before = jnp.dot(onehot.astype(BF16), tri_ref[...], preferred_element_type=F32) + run_ref[...]
    rank_rows = [jnp.sum(jnp.where(io_e == f, before, 0.0), axis=0, keepdims=True) for f in idx_rows]
    rank_ref[0] = jnp.concatenate(rank_rows, axis=0).astype(I32)
    run_ref[...] = run_ref[...] + jnp.sum(onehot, axis=1, keepdims=True)
    cnt_ref[...] = run_ref[...]


def _router(tok, wrt, bias):
    N, D = tok.shape
    tm = ROUTER_TILE
    assert N % tm == 0
    tri = (jnp.arange(tm)[:, None] < jnp.arange(tm)[None, :]).astype(BF16)
    return pl.pallas_call(
        _router_kernel,
        grid=(N // tm,),
        in_specs=[
            pl.BlockSpec((tm, D), lambda i: (i, 0)),
            _const_spec(wrt.shape), _const_spec(bias.shape), _const_spec(tri.shape),
        ],
        out_specs=[
            pl.BlockSpec((1, TOP_K, tm), lambda i: (i, 0, 0)),
            pl.BlockSpec((TOP_K, tm), lambda i: (0, i)),
            pl.BlockSpec((1, TOP_K, tm), lambda i: (i, 0, 0)),
            _const_spec((N_EXPERTS, 1)),
            pl.BlockSpec((tm * PACKED_SUBLANES, LANES), lambda i: (i, 0)),
        ],
        out_shape=[
            jax.ShapeDtypeStruct((N // tm, TOP_K, tm), I32),
            jax.ShapeDtypeStruct((TOP_K, N), F32),
            jax.ShapeDtypeStruct((N // tm, TOP_K, tm), I32),
            jax.ShapeDtypeStruct((N_EXPERTS, 1), F32),
            jax.ShapeDtypeStruct((N * PACKED_SUBLANES, LANES), U32),
        ],
        scratch_shapes=[pltpu.VMEM((N_EXPERTS, 1), F32)],
        compiler_params=_cparams(("arbitrary",)),
        name="moe_router",
    )(tok, wrt, bias, tri)


def _slots_kernel(idx_ref, rank_ref, pstart_ref, dest_ref):
    idx = idx_ref[0]
    tm = idx.shape[1]
    io_e = lax.broadcasted_iota(I32, (N_EXPERTS, tm), 0)
    start = pstart_ref[...]
    rows = [jnp.sum(jnp.where(io_e == idx[k:k + 1, :], start, 0.0), axis=0, keepdims=True) for k in range(TOP_K)]
    dest_ref[0] = jnp.concatenate(rows, axis=0).astype(I32) + rank_ref[0]


def _slots(idx3, rank3, pstart):
    nt, _, tm = idx3.shape
    spec = pl.BlockSpec((1, TOP_K, tm), lambda i: (i, 0, 0))
    return pl.pallas_call(
        _slots_kernel,
        grid=(nt,),
        in_specs=[spec, spec, _const_spec((N_EXPERTS, 1))],
        out_specs=spec,
        out_shape=jax.ShapeDtypeStruct(idx3.shape, I32),
        compiler_params=_cparams(("arbitrary",)),
        name="moe_slots",
    )(idx3, rank3, pstart.astype(F32).reshape(N_EXPERTS, 1))


def _load_slot_table(i, dest_hbm, dest_s, sem):
    copy = pltpu.make_async_copy(dest_hbm.at[i], dest_s, sem)
    copy.start()
    copy.wait()


def _dispatch_kernel(dest_hbm, pend_ref, cnt_ref, tok_ref, xs_hbm, dest_s, zero_ref, sem_idx, sem_rows):
    i = pl.program_id(0)
    tm = tok_ref.shape[0]
    blk = zero_ref.shape[0]

    @pl.when(i == 0)
    def _():
        zero_ref[...] = jnp.zeros_like(zero_ref)

        def zero_copy(e):
            last_block = pl.multiple_of(pend_ref[e] - blk, blk)
            return pltpu.make_async_copy(zero_ref, xs_hbm.at[pl.ds(last_block, blk)], sem_rows)

        def start(e, c):
            @pl.when(cnt_ref[e] > 0)
            def _():
                zero_copy(e).start()
            return c

        def wait(e, c):
            @pl.when(cnt_ref[e] > 0)
            def _():
                zero_copy(e).wait()
            return c

        lax.fori_loop(0, N_EXPERTS, start, 0)
        lax.fori_loop(0, N_EXPERTS, wait, 0)

    _load_slot_table(i, dest_hbm, dest_s, sem_idx)

    def start(t, c):
        for k in range(TOP_K):
            pltpu.make_async_copy(tok_ref.at[t], xs_hbm.at[dest_s[k, t]], sem_rows).start()
        return c

    def wait(t, c):
        for k in range(TOP_K):
            pltpu.make_async_copy(tok_ref.at[0], xs_hbm.at[0], sem_rows).wait()
        return c

    lax.fori_loop(0, tm, start, 0)
    lax.fori_loop(0, tm, wait, 0)


def _dispatch(rows, dest3, pend, counts, n_slots):
    N = rows.shape[0]
    tm = dest3.shape[2]
    assert N % tm == 0 and dest3.shape == (N // tm, TOP_K, tm)
    smem = pl.BlockSpec(memory_space=pltpu.SMEM)
    return pl.pallas_call(
        _dispatch_kernel,
        grid=(N // tm,),
        in_specs=[
            pl.BlockSpec(memory_space=pl.ANY), smem, smem,
            pl.BlockSpec((tm,) + rows.shape[1:], lambda i: (i, 0, 0)),
        ],
        out_specs=pl.BlockSpec(memory_space=pl.ANY),
        scratch_shapes=[
            pltpu.SMEM((TOP_K, tm), I32),
            pltpu.VMEM((EXPERT_BLOCK,) + rows.shape[1:], rows.dtype),
            pltpu.SemaphoreType.DMA,
            pltpu.SemaphoreType.DMA,
        ],
        out_shape=jax.ShapeDtypeStruct((n_slots,) + rows.shape[1:], rows.dtype),
        compiler_params=_cparams(("arbitrary",)),
        name="moe_dispatch",
    )(dest3, pend, counts, rows)


def _ffn_kernel(be_ref, nu_ref, buf_ref, nxt_ref, xs_ref, wg_hbm, wu_hbm, wd_hbm, ys_ref,
                wg_buf, wu_buf, wd_buf, wgu_s, wd_s, sems, *, layer):
    b = pl.program_id(0)
    used = b < nu_ref[0]
    prev = be_ref[jnp.maximum(b - 1, 0)]
    new_expert = jnp.logical_or(b == 0, be_ref[b] != prev)

    def weight_copies(e, s):
        return [pltpu.make_async_copy(wg_hbm.at[layer, e], wg_buf.at[s], sems.at[s]),
                pltpu.make_async_copy(wu_hbm.at[layer, e], wu_buf.at[s], sems.at[s]),
                pltpu.make_async_copy(wd_hbm.at[layer, e], wd_buf.at[s], sems.at[s])]

    @pl.when(jnp.logical_and(used, b == 0))
    def _():
        for c in weight_copies(be_ref[0], buf_ref[0]):
            c.start()

    @pl.when(jnp.logical_and(used, new_expert))
    def _():
        s = buf_ref[b]

        @pl.when(nxt_ref[b] >= 0)
        def _():
            for c in weight_copies(nxt_ref[b], 1 - s):
                c.start()

        for c in weight_copies(be_ref[b], s):
            c.wait()
        wgu_s[:, :D_EXPERT] = wg_buf[s].astype(BF16)
        wgu_s[:, D_EXPERT:] = wu_buf[s].astype(BF16)
        wd_s[...] = wd_buf[s].astype(BF16)

    @pl.when(used)
    def _():
        blk = xs_ref.shape[0] // PACKED_SUBLANES
        words = [xs_ref[pl.ds(c, blk, stride=PACKED_SUBLANES), :] for c in range(PACKED_SUBLANES)]
        low = [lax.bitcast_convert_type(w << 16, F32) for w in words]
        high = [lax.bitcast_convert_type(w & jnp.uint32(0xFFFF0000), F32) for w in words]
        x = jnp.concatenate(low + high, axis=1)
        gu = jnp.dot(x.astype(BF16), wgu_s[...], preferred_element_type=F32)
        gt = gu[:, :D_EXPERT]
        hid = gt * jax.nn.sigmoid(gt) * gu[:, D_EXPERT:]
        y = jnp.dot(hid.astype(BF16), wd_s[...], preferred_element_type=F32)
        for c in range(SUBLANES):
            ys_ref[pl.ds(c, blk, stride=SUBLANES), :] = y[:, c * LANES:(c + 1) * LANES]


def _expert_ffn(xs, block_e, n_used, layer, wg, wu, wd):
    n_slots = xs.shape[0]
    D = SUBLANES * LANES
    blk = EXPERT_BLOCK
    n_blocks = n_slots // blk

    def row_map(b, be, nu, buf, nxt):
        return (jnp.minimum(b, nu[0] - 1), 0)

    first = jnp.concatenate([jnp.ones((1,), bool), block_e[1:] != block_e[:-1]])
    buf = ((jnp.cumsum(first.astype(I32)) - 1) % 2).astype(I32)
    nxt_block = jnp.sum(block_e[None, :] <= block_e[:, None], axis=1).astype(I32)
    nxt = jnp.where(nxt_block < n_used[0], block_e[jnp.minimum(nxt_block, n_blocks - 1)], -1).astype(I32)

    hbm = pl.BlockSpec(memory_space=pl.ANY)
    return pl.pallas_call(
        functools.partial(_ffn_kernel, layer=layer),
        grid_spec=pltpu.PrefetchScalarGridSpec(
            num_scalar_prefetch=4,
            grid=(n_blocks,),
            in_specs=[pl.BlockSpec((blk * PACKED_SUBLANES, LANES), row_map), hbm, hbm, hbm],
            out_specs=pl.BlockSpec((blk * SUBLANES, LANES), row_map),
            scratch_shapes=[
                pltpu.VMEM((2, D, D_EXPERT), F32), pltpu.VMEM((2, D, D_EXPERT), F32), pltpu.VMEM((2, D_EXPERT, D), F32),
                pltpu.VMEM((D, 2 * D_EXPERT), BF16), pltpu.VMEM((D_EXPERT, D), BF16),
                pltpu.SemaphoreType.DMA((2,)),
            ],
        ),
        out_shape=jax.ShapeDtypeStruct((n_slots * SUBLANES, LANES), F32),
        compiler_params=_cparams(("arbitrary",)),
        name="moe_ffn",
    )(block_e, n_used, buf, nxt, xs.reshape(n_slots * PACKED_SUBLANES, LANES), wg, wu, wd
      ).reshape(n_slots, SUBLANES, LANES)


def _combine_kernel(dest_hbm, ys_hbm, tok_ref, w_ref, wsg_ref, wsu_ref, wsd_ref, g_ref, b_ref,
                    out_ref, dest_s, buf_ref, sem_idx, sem_rows):
    i = pl.program_id(0)
    tm = tok_ref.shape[0]

    _load_slot_table(i, dest_hbm, dest_s, sem_idx)

    def start(t, c):
        row = pl.multiple_of(t * SUBLANES, SUBLANES)
        for k in range(TOP_K):
            pltpu.make_async_copy(ys_hbm.at[dest_s[k, t]], buf_ref.at[k, pl.ds(row, SUBLANES)], sem_rows).start()
        return c

    def wait(t, c):
        for k in range(TOP_K):
            pltpu.make_async_copy(ys_hbm.at[0], buf_ref.at[0, pl.ds(0, SUBLANES)], sem_rows).wait()
        return c

    lax.fori_loop(0, tm, start, 0)

    h = tok_ref[...]
    hb = h.astype(BF16)
    gt = jnp.dot(hb, wsg_ref[...], preferred_element_type=F32)
    up = jnp.dot(hb, wsu_ref[...], preferred_element_type=F32)
    hid = gt * jax.nn.sigmoid(gt) * up
    f = jnp.dot(hid.astype(BF16), wsd_ref[...], preferred_element_type=F32)

    lax.fori_loop(0, tm, wait, 0)
    w = w_ref[...]
    wb = [jnp.broadcast_to(w[:, k:k + 1], (tm, LANES)) for k in range(TOP_K)]
    chunks = []
    for c in range(SUBLANES):
        acc = f[:, c * LANES:(c + 1) * LANES]
        for k in range(TOP_K):
            acc = acc + buf_ref[k, pl.ds(c, tm, stride=SUBLANES), :] * wb[k]
        chunks.append(acc)
    f = jnp.concatenate(chunks, axis=1)
    out_ref[...] = _layer_norm(DEEPNORM_ALPHA * h + f, g_ref[...], b_ref[...])


def _combine(tok, ys, dest3, w_tok, wsg, wsu, wsd, g, b):
    N, D = tok.shape
    tm = dest3.shape[2]
    assert N % tm == 0 and dest3.shape == (N // tm, TOP_K, tm)
    return pl.pallas_call(
        _combine_kernel,
        grid=(N // tm,),
        in_specs=[
            pl.BlockSpec(memory_space=pl.ANY),
            pl.BlockSpec(memory_space=pl.ANY),
            pl.BlockSpec((tm, D), lambda i: (i, 0)),
            pl.BlockSpec((tm, TOP_K), lambda i: (i, 0)),
            _const_spec(wsg.shape), _const_spec(wsu.shape), _const_spec(wsd.shape),
            _const_spec(g.shape), _const_spec(b.shape),
        ],
        out_specs=pl.BlockSpec((tm, D), lambda i: (i, 0)),
        scratch_shapes=[
            pltpu.SMEM((TOP_K, tm), I32),
            pltpu.VMEM((TOP_K, tm * SUBLANES, LANES), F32),
            pltpu.SemaphoreType.DMA,
            pltpu.SemaphoreType.DMA,
        ],
        out_shape=jax.ShapeDtypeStruct((N, D), F32),
        compiler_params=_cparams(("arbitrary",)),
        name="moe_combine",
    )(dest3, ys, tok, w_tok, wsg, wsu, wsd, g, b)


def _moe_layer(tok, layer, w_router, router_bias, we_gate, we_up, we_down, ws_gate, ws_up, ws_down, ln_g, ln_b):
    N, D = tok.shape
    blk = EXPERT_BLOCK
    idx3, w_t, rank3, cnt, packed = _router(tok, w_router.T.astype(BF16), router_bias.reshape(N_EXPERTS, 1))

    counts = cnt[:, 0].astype(I32)
    pcounts = (counts + blk - 1) // blk * blk
    pend = jnp.cumsum(pcounts).astype(I32)
    pstart = pend - pcounts
    n_blocks = N * TOP_K // blk + N_EXPERTS
    block_start = jnp.arange(n_blocks, dtype=I32) * blk
    block_e = jnp.minimum(jnp.sum(pend[None, :] <= block_start[:, None], axis=1), N_EXPERTS - 1).astype(I32)
    n_used = (pend[-1] // blk).reshape(1)

    dest3 = _slots(idx3, rank3, pstart)
    xs = _dispatch(packed.reshape(N, PACKED_SUBLANES, LANES), dest3, pend, counts, n_blocks * blk)
    ys = _expert_ffn(xs, block_e, n_used, layer, we_gate, we_up, we_down)
    return _combine(tok, ys, dest3, w_t.T,
                    ws_gate.astype(BF16), ws_up.astype(BF16), ws_down.astype(BF16),
                    ln_g.reshape(1, D), ln_b.reshape(1, D))


def _rel_bucket(dist):
    exact = N_BUCKETS // 2
    df = jnp.maximum(dist, 1).astype(F32)
    large = exact + (jnp.log(df / exact) / math.log(MAX_DISTANCE / exact) * (N_BUCKETS - exact)).astype(I32)
    large = jnp.minimum(large, N_BUCKETS - 1)
    return jnp.where(dist < exact, dist, large)


def _qkv_kernel(x_ref, perm_ref, w_ref, wkvt_ref, q_ref, k_ref, v_ref, kt_ref, vt_ref, *, dil, first_kept_tile,
                kept_cols):
    tl = x_ref.shape[0]
    chunk = tl // dil
    x = x_ref[...].astype(BF16)
    xp = x if dil == 1 else jnp.dot(perm_ref[...], x, preferred_element_type=F32).astype(BF16)
    y = jnp.dot(xp, w_ref[...], preferred_element_type=F32)
    for r in range(dil):
        rows = slice(r * chunk, (r + 1) * chunk)
        q_ref[r] = y[rows, :D_ATTN].astype(BF16)
        k_ref[r] = y[rows, D_ATTN:2 * D_ATTN].astype(BF16)
        v_ref[r] = y[rows, 2 * D_ATTN:].astype(BF16)

    @pl.when(pl.program_id(1) >= first_kept_tile)
    def _():
        kvt = lax.dot_general(wkvt_ref[...], x[tl - kept_cols:, :], (((1,), (1,)), ((), ())),
                              preferred_element_type=F32)
        kt_ref[...] = kvt[:D_ATTN]
        vt_ref[...] = kvt[D_ATTN:]


def _qkv_prompt(tok, B, S, w_g, dil, keep):
    D = tok.shape[1]
    tl = min(S, 512)
    kept_cols = min(tl, keep)
    assert S % tl == 0 and tl % dil == 0 and keep % kept_cols == 0 and (S - keep) % kept_cols == 0
    first_kept_tile = (S - keep) // tl
    phase_spec = pl.BlockSpec((None, dil, tl // dil, D_ATTN), lambda b, t: (b, 0, t, 0))
    state_spec = pl.BlockSpec((None, D_ATTN, kept_cols), lambda b, t: (b, 0, jnp.maximum(t - first_kept_tile, 0)))
    wkvt = w_g[:, D_ATTN:].T
    dst = jnp.arange(tl, dtype=I32)
    src = (dst % (tl // dil)) * dil + dst // (tl // dil)
    perm = (src[:, None] == jnp.arange(tl, dtype=I32)[None, :]).astype(BF16)
    return pl.pallas_call(
        functools.partial(_qkv_kernel, dil=dil, first_kept_tile=first_kept_tile, kept_cols=kept_cols),
        grid=(B, S // tl),
        in_specs=[pl.BlockSpec((tl, D), lambda b, t: (b * (S // tl) + t, 0)), _const_spec(perm.shape),
                  _const_spec(w_g.shape), _const_spec(wkvt.shape)],
        out_specs=[phase_spec, phase_spec, phase_spec, state_spec, state_spec],
        out_shape=[jax.ShapeDtypeStruct((B, dil, S // dil, D_ATTN), BF16)] * 3
        + [jax.ShapeDtypeStruct((B, D_ATTN, keep), F32)] * 2,
        compiler_params=_cparams(("arbitrary", "arbitrary")),
        name="qkv_prompt",
    )(tok, perm, w_g, wkvt)


def _attn_prompt_kernel(q_ref, kp_ref, kc_ref, vp_ref, vc_ref, bias_ref, o_ref, lse_ref, *, steps, phases_on_lanes):
    n = pl.program_id(1)
    qb = QUERY_BLOCK
    qi = lax.broadcasted_iota(I32, (qb, 2 * qb), 0)
    kj = lax.broadcasted_iota(I32, (qb, 2 * qb), 1)
    dist = qi + qb - kj
    valid = (dist >= 0) & (dist <= steps) & ((n > 0) | (kj >= qb))
    low = lax.broadcasted_iota(I32, (1, 2 * HEAD_DIM), 1) < HEAD_DIM
    scale = HEAD_DIM ** -0.5
    for i in range(q_ref.shape[0]):
        for p in range(HEADS // 2):
            c0 = p * 2 * HEAD_DIM
            cols = slice(c0, c0 + 2 * HEAD_DIM)
            q2 = q_ref[i, :, cols]
            k2 = jnp.concatenate([kp_ref[i, :, cols], kc_ref[i, :, cols]], axis=0)
            v2 = jnp.concatenate([vp_ref[i, :, cols], vc_ref[i, :, cols]], axis=0)
            outs, lses = [], []
            for half in range(2):
                keep = low if half == 0 else jnp.logical_not(low)
                qh = jnp.where(keep, q2, jnp.zeros_like(q2))
                s = lax.dot_general(qh, k2, (((1,), (1,)), ((), ())), preferred_element_type=F32)
                s = s * scale + bias_ref[2 * p + half]
                s = jnp.where(valid, s, NEG_INF)
                m = jnp.max(s, axis=-1, keepdims=True)
                e = jnp.exp(s - m)
                l = jnp.sum(e, axis=-1, keepdims=True)
                prob = e / l
                outs.append(jnp.dot(prob.astype(BF16), v2, preferred_element_type=F32))
                lses.append(m + jnp.log(l))
            o_val = jnp.where(low, outs[0], outs[1])
            lse_val = jnp.where(low, lses[0], lses[1])
            if phases_on_lanes:
                out_cols = slice(i * D_ATTN + c0, i * D_ATTN + c0 + 2 * HEAD_DIM)
                o_ref[0, :, out_cols] = o_val
                lse_ref[0, :, out_cols] = lse_val
            else:
                o_ref[i, :, cols] = o_val
                lse_ref[i, :, cols] = lse_val


def _attn_prompt(q, k, v, bias, steps):
    B, dil, L, _ = q.shape
    qb = QUERY_BLOCK
    ns = ATTN_STREAMS
    assert L % qb == 0 and (B * dil) % ns == 0 and (dil == 1 or dil % ns == 0)
    cur = pl.BlockSpec((ns, qb, D_ATTN), lambda bd, n: (bd, n, 0))
    prev = pl.BlockSpec((ns, qb, D_ATTN), lambda bd, n: (bd, jnp.maximum(n - 1, 0), 0))
    if dil == 1:
        out = pl.BlockSpec((ns, qb, D_ATTN), lambda bd, n: (bd, n, 0))
    else:
        out = pl.BlockSpec((1, qb, ns * D_ATTN), lambda bd, n: (bd * ns // dil, n, bd % (dil // ns)))
    qv, kv, vv = (a.reshape(B * dil, L, D_ATTN) for a in (q, k, v))
    o, lse = pl.pallas_call(
        functools.partial(_attn_prompt_kernel, steps=steps, phases_on_lanes=dil > 1),
        grid=(B * dil // ns, L // qb),
        in_specs=[cur, prev, cur, prev, cur, _const_spec(bias.shape)],
        out_specs=[out, out],
        out_shape=[jax.ShapeDtypeStruct((B, L, dil * D_ATTN), F32)] * 2,
        compiler_params=_cparams(("arbitrary", "arbitrary")),
        name="attn_prompt",
    )(qv, kv, kv, vv, vv, bias)
    return o.reshape(B * L * dil, D_ATTN), lse.reshape(B * L * dil, D_ATTN)


def _merge_kernel(o1, o2, o3, l1, l2, l3, h_ref, wo_ref, g_ref, b_ref, out_ref):
    a1, a2, a3 = l1[...], l2[...], l3[...]
    m = jnp.maximum(jnp.maximum(a1, a2), a3)
    e1, e2, e3 = jnp.exp(a1 - m), jnp.exp(a2 - m), jnp.exp(a3 - m)
    o = (e1 * o1[...] + e2 * o2[...] + e3 * o3[...]) / (e1 + e2 + e3)
    y = jnp.dot(o.astype(BF16), wo_ref[...], preferred_element_type=F32)
    out_ref[...] = _layer_norm(DEEPNORM_ALPHA * h_ref[...] + y, g_ref[...], b_ref[...])


def _merge(os_, ls_, tok, row0, wo, g, b):
    M = os_[0].shape[0]
    N, D = tok.shape
    tm = math.gcd(MERGE_TILE, M, N)
    assert row0 % tm == 0
    a_spec = pl.BlockSpec((tm, D_ATTN), lambda i: (i, 0))
    row_spec = pl.BlockSpec((tm, D), lambda i: (row0 // tm + i, 0))
    args = (*os_, *ls_, tok, wo, g, b)
    return pl.pallas_call(
        _merge_kernel,
        grid=(M // tm,),
        in_specs=[a_spec] * 6 + [row_spec, _const_spec(wo.shape), _const_spec(g.shape), _const_spec(b.shape)],
        out_specs=row_spec,
        out_shape=jax.ShapeDtypeStruct((N, D), F32),
        input_output_aliases={len(os_) + len(ls_): 0},
        compiler_params=_cparams(("arbitrary",)),
        name="attn_merge",
    )(*args)


def _mm_kernel(x_ref, w_ref, o_ref):
    o_ref[...] = jnp.dot(x_ref[...].astype(BF16), w_ref[...], preferred_element_type=F32)


def _matmul(x, row0, M, w, tn):
    K = x.shape[1]
    _, N = w.shape
    assert N % tn == 0 and row0 % M == 0
    return pl.pallas_call(
        _mm_kernel,
        grid=(N // tn,),
        in_specs=[pl.BlockSpec((M, K), lambda j: (row0 // M, 0)), pl.BlockSpec((K, tn), lambda j: (0, j))],
        out_specs=pl.BlockSpec((M, tn), lambda j: (0, j)),
        out_shape=jax.ShapeDtypeStruct((M, N), F32),
        compiler_params=_cparams(("arbitrary",)),
        name="matmul",
    )(x, w)


def _round_bf16(x):
    return x.astype(BF16).astype(F32)


def _attn_sample_kernel(q_ref, kn_ref, vn_ref, kc_ref, vc_ref, bias_ref, b0_ref, o_ref, lse_ref):
    scale = HEAD_DIM ** -0.5
    tb = kc_ref.shape[0]
    eye = (lax.broadcasted_iota(I32, (HEAD_DIM, HEAD_DIM), 0)
           == lax.broadcasted_iota(I32, (HEAD_DIM, HEAD_DIM), 1)).astype(F32)

    def column(ref):
        rows = _round_bf16(ref[...])
        return jnp.sum(rows * eye[None], axis=2, keepdims=True).reshape(tb, HEADS, HEAD_DIM, 1)

    q = column(q_ref)
    kn = column(kn_ref)
    vn = column(vn_ref)
    kc = _round_bf16(kc_ref[...])
    s = jnp.sum(kc * q, axis=2, keepdims=True) * scale + bias_ref[...][None]
    s0 = jnp.sum(q * kn, axis=2, keepdims=True) * scale + b0_ref[...][None]
    m = jnp.maximum(jnp.max(s, axis=3, keepdims=True), s0)
    e = jnp.exp(s - m)
    e0 = jnp.exp(s0 - m)
    l = jnp.sum(e, axis=3, keepdims=True) + e0
    p = _round_bf16(e / l)
    p0 = _round_bf16(e0 / l)
    vc = _round_bf16(vc_ref[...])
    o = jnp.sum(vc * p, axis=3, keepdims=True) + p0 * vn
    o_ref[...] = jnp.sum(o.reshape(tb * HEADS, HEAD_DIM, 1) * eye[None], axis=1, keepdims=True)
    lse_ref[...] = (m + jnp.log(l)).reshape(tb * HEADS, 1, 1)


def _attn_sample(q, kn, vn, ckt, cvt, j, bias, bias_self, tb=1):
    DB = q.shape[0] // HEADS
    Wb = ckt.shape[3]
    assert DB % tb == 0
    new_spec = pl.BlockSpec((tb * HEADS, 1, HEAD_DIM), lambda i: (i, 0, 0))
    cache_spec = pl.BlockSpec((tb, HEADS, HEAD_DIM, Wb), lambda i: (j * (DB // tb) + i, 0, 0, 0))
    return pl.pallas_call(
        _attn_sample_kernel,
        grid=(DB // tb,),
        in_specs=[new_spec, new_spec, new_spec, cache_spec, cache_spec,
                  _const_spec(bias.shape), _const_spec(bias_self.shape)],
        out_specs=[new_spec, pl.BlockSpec((tb * HEADS, 1, 1), lambda i: (i, 0, 0))],
        out_shape=[jax.ShapeDtypeStruct((DB * HEADS, 1, HEAD_DIM), F32), jax.ShapeDtypeStruct((DB * HEADS, 1, 1), F32)],
        compiler_params=_cparams(("arbitrary",)),
        name="attn_sample",
    )(q, kn, vn, ckt, cvt, bias, bias_self)


def _bias_lookup(tab, buckets):
    onehot = (buckets[..., None] == jnp.arange(N_BUCKETS, dtype=I32)).astype(F32)
    return jnp.einsum('...n,nh->...h', onehot, tab, precision=lax.Precision.HIGHEST)


def _dilated_layer(tok, B, S, caches, j, w_qkv, w_o, rel_bias, g1, b1):
    DB = tok.shape[0] - B * S
    w_qkv_b = w_qkv.astype(BF16)
    w_o_b = w_o.astype(BF16)
    qb = QUERY_BLOCK
    qi = jnp.arange(qb, dtype=I32)[:, None]
    kj = jnp.arange(2 * qb, dtype=I32)[None, :]
    dist = qi + qb - kj
    qkv_s = _matmul(tok, B * S, DB, w_qkv_b, 512)

    def sample_cols(part, g):
        c0 = (part * N_GROUPS + g) * D_ATTN
        return qkv_s[:, c0:c0 + D_ATTN].reshape(DB, HEADS, HEAD_DIM)

    def position_minor(c):
        return jnp.transpose(c, (0, 1, 3, 4, 2)).reshape(c.shape[0] * DB, HEADS, HEAD_DIM, c.shape[2])

    os_, ls_, kp, vp = [], [], [], []
    os_s, ls_s, ks, vs = [], [], [], []
    for g, (window, dil) in enumerate(DIL_CONFIGS):
        steps = window // dil
        assert steps == qb
        tab = rel_bias[:, g * HEADS:(g + 1) * HEADS]
        cols = [w_qkv_b[:, (part * N_GROUPS + g) * D_ATTN:(part * N_GROUPS + g + 1) * D_ATTN] for part in range(3)]
        keep = min(window, S)
        q, k, v, kt, vt = _qkv_prompt(tok, B, S, jnp.concatenate(cols, axis=1), dil, keep)
        bias = _bias_lookup(tab, _rel_bucket(jnp.maximum(dist, 0) * dil)).transpose(2, 0, 1)
        o, lse = _attn_prompt(q, k, v, bias, steps)
        os_.append(o)
        ls_.append(lse)
        kp.append(jnp.transpose(kt.reshape(B, HEADS, HEAD_DIM, keep), (0, 3, 1, 2)))
        vp.append(jnp.transpose(vt.reshape(B, HEADS, HEAD_DIM, keep), (0, 3, 1, 2)))
        ck, cv = caches[g]
        wb = ck.shape[2]
        assert wb == steps * dil and ck.shape[1] == DB
        pos = jnp.arange(wb, dtype=I32)
        cache_bias = jnp.where((pos % dil == 0)[:, None], _bias_lookup(tab, _rel_bucket(wb - pos)), NEG_INF)
        self_bias = _bias_lookup(tab, _rel_bucket(jnp.zeros((1,), I32)))
        q_s, k_s, v_s = sample_cols(0, g), sample_cols(1, g), sample_cols(2, g)
        head_rows = (DB * HEADS, 1, HEAD_DIM)
        o_s, lse_s = _attn_sample(q_s.reshape(head_rows), k_s.reshape(head_rows), v_s.reshape(head_rows),
                                  position_minor(ck), position_minor(cv), j, cache_bias.T[:, None, :],
                                  self_bias.T[:, :, None], tb=max(1, 1024 // wb))
        os_s.append(o_s.reshape(DB, D_ATTN))
        ls_s.append(jnp.broadcast_to(lse_s.reshape(DB, HEADS, 1), (DB, HEADS, HEAD_DIM)).reshape(DB, D_ATTN))
        ks.append(k_s[:, None])
        vs.append(v_s[:, None])
    tok = _merge(os_, ls_, tok, 0, w_o_b, g1, b1)
    tok = _merge(os_s, ls_s, tok, B * S, w_o_b, g1, b1)
    return tok, kp, vp, ks, vs


def _pool_conv_layer(hp, hs, state_pool, state_conv, w_in, pool_w, pool_scale, conv_w, conv_b, ln_g, ln_b, w_out,
                     g1, b1):
    params = (w_in.astype(BF16), pool_w.astype(BF16), pool_scale.reshape(1, D_POOL), conv_w,
              conv_b.reshape(1, D_CONV), ln_g.reshape(1, D_CONV), ln_b.reshape(1, D_CONV), w_out.astype(BF16), g1, b1)
    B, S, _ = hp.shape
    tok = jnp.zeros((B * S + hs.shape[0], hp.shape[2]), F32)
    tok, pst, cst = _l0_prompt(hp, tok, *params)
    tok, u_s, glu_s = _l0_sample(hs, tok, state_pool, state_conv, *params)
    pool_p = pst[:, POOL_HALO - POOL_STATE:]
    conv_p = cst[:, CONV_HALO - CONV_STATE:]
    pool_s = jnp.concatenate([state_pool[:, 1:], u_s[:, None, :]], axis=1)
    conv_s = jnp.concatenate([state_conv[:, 1:], glu_s[:, None, :]], axis=1)
    return tok, pool_p, conv_p, pool_s, conv_s


def kernel(x_prompt, x_sample, state_pool, state_conv, cache_k1, cache_v1, cache_k2, cache_v2, cache_k3, cache_v3,
           w_in_ab, pool_w, pool_scale, conv_w, conv_b, conv_ln_g, conv_ln_b, w_out_ab, w_qkv, w_o, rel_bias,
           ln1_g, ln1_b, ln2_g, ln2_b, w_router, router_bias, we_gate, we_up, we_down, ws_gate, ws_up, ws_down):
    B, S, D = x_prompt.shape
    DB, T, _ = x_sample.shape
    assert T == 1 and D == D_MODEL
    tok = None
    caches_k = (cache_k1, cache_k2, cache_k3)
    caches_v = (cache_v1, cache_v2, cache_v3)
    pool_p, conv_p, pool_s, conv_s = [], [], [], []
    kp = [[] for _ in range(N_GROUPS)]
    vp = [[] for _ in range(N_GROUPS)]
    ksm = [[] for _ in range(N_GROUPS)]
    vsm = [[] for _ in range(N_GROUPS)]
    for layer in range(DEPTH):
        j = layer // 2
        g1 = ln1_g[layer].reshape(1, D)
        b1 = ln1_b[layer].reshape(1, D)
        if layer % 2 == 0:
            hp = x_prompt if tok is None else tok[:B * S].reshape(B, S, D)
            hs = x_sample.reshape(DB, D) if tok is None else tok[B * S:]
            tok, pp, cp, ps, cs = _pool_conv_layer(
                hp, hs, state_pool[j], state_conv[j], w_in_ab[j], pool_w[j], pool_scale[j], conv_w[j], conv_b[j],
                conv_ln_g[j], conv_ln_b[j], w_out_ab[j], g1, b1)
            pool_p.append(pp)
            conv_p.append(cp)
            pool_s.append(ps)
            conv_s.append(cs)
        else:
            caches = [(caches_k[g], caches_v[g]) for g in range(N_GROUPS)]
            tok, nkp, nvp, nks, nvs = _dilated_layer(tok, B, S, caches, j, w_qkv[j], w_o[j], rel_bias, g1, b1)
            for g in range(N_GROUPS):
                kp[g].append(nkp[g])
                vp[g].append(nvp[g])
                ksm[g].append(nks[g])
                vsm[g].append(nvs[g])
        tok = _moe_layer(tok, layer, w_router[layer], router_bias[layer], we_gate, we_up, we_down,
                         ws_gate[layer], ws_up[layer], ws_down[layer], ln2_g[layer], ln2_b[layer])
    return (tok[:B * S].reshape(B, S, D), tok[B * S:].reshape(DB, T, D),
            jnp.stack(pool_p), jnp.stack(conv_p),
            jnp.stack(kp[0]), jnp.stack(vp[0]), jnp.stack(kp[1]), jnp.stack(vp[1]), jnp.stack(kp[2]), jnp.stack(vp[2]),
            jnp.stack(pool_s), jnp.stack(conv_s),
            jnp.stack(ksm[0]), jnp.stack(vsm[0]), jnp.stack(ksm[1]), jnp.stack(vsm[1]),
            jnp.stack(ksm[2]), jnp.stack(vsm[2]))
```
